```python
import math
import jax, jax.numpy as jnp
from jax import lax
import numpy as np


D_MODEL = 1024
BATCH = 4
SEQ = 8192
DEPTH = 2
DEC_BATCH = 128
DEC_SEQ = 4
PAST_LEN = 16384
PAGE_SIZE = 128

N_A_LAYERS = DEPTH // 2
N_B_LAYERS = DEPTH - N_A_LAYERS
EXPAND = 2
BRANCH_WIDTH = EXPAND * D_MODEL
A_HEADS = 4
A_KEY_WIDTH = D_MODEL // 2
A_HEAD_DK = A_KEY_WIDTH // A_HEADS
A_HEAD_DV = BRANCH_WIDTH // A_HEADS
A_GATE_RANK = 16
A_GATE_NORMALIZER = 16.0
A_CHUNK = 64
A_IN_WIDTH = 2 * A_KEY_WIDTH + 2 * BRANCH_WIDTH + A_GATE_RANK
B_HEAD_DIM = 64
B_HEADS = BRANCH_WIDTH // B_HEAD_DIM
B_KV_HEADS = 4
B_GROUP = B_HEADS // B_KV_HEADS
B_KV_WIDTH = B_KV_HEADS * B_HEAD_DIM
B_IN_WIDTH = 2 * BRANCH_WIDTH
WINDOW = 128
SWA_BLOCK = 128
ROPE_THETA = 10000.0
RMS_EPS = 1e-6

kernel_name = 'yoco_gla_swa_sink_decoder_step'


def rmsnorm(x, g):
    xf = x.astype(jnp.float32)
    xf = xf * lax.rsqrt(jnp.mean(xf * xf, axis=-1, keepdims=True) + RMS_EPS)
    return (xf * g.astype(jnp.float32)).astype(x.dtype)


def rope(x, pos):
    half = x.shape[-1] // 2
    inv_freq = ROPE_THETA ** (-jnp.arange(half, dtype=jnp.float32) / half)
    ang = pos.astype(jnp.float32)[:, None] * inv_freq[None, :]
    cos = jnp.cos(ang)[None, :, None, :]
    sin = jnp.sin(ang)[None, :, None, :]
    xf = x.astype(jnp.float32)
    x1, x2 = xf[..., :half], xf[..., half:]
    return jnp.concatenate([x1 * cos - x2 * sin, x2 * cos + x1 * sin], axis=-1).astype(x.dtype)


def gla_recurrence(q, k, v, log_a, state0):
    bsz, seq_len, n_heads, _ = q.shape
    dv = v.shape[-1]
    chunk = math.gcd(seq_len, A_CHUNK)
    n_chunks = seq_len // chunk

    def to_chunks(t):
        return t.reshape(bsz, n_chunks, chunk, n_heads, t.shape[-1]).transpose(1, 0, 3, 2, 4)

    causal = jnp.tril(jnp.ones((chunk, chunk), dtype=bool))[:, :, None]

    def step(state, blk):
        qc, kc, vc, ac = blk
        b = jnp.cumsum(ac, axis=2)
        rel = jnp.where(causal, b[:, :, :, None, :] - b[:, :, None, :, :], -jnp.inf)
        scores = jnp.einsum('bhtd,bhsd,bhtsd->bhts', qc, kc, jnp.exp(rel))
        out = (jnp.einsum('bhtd,bhdv->bhtv', qc * jnp.exp(b), state)
               + jnp.einsum('bhts,bhsv->bhtv', scores, vc))
        b_last = b[:, :, -1:, :]
        new_state = (jnp.exp(b_last[:, :, 0, :, None]) * state
                     + jnp.einsum('bhsd,bhsv->bhdv', kc * jnp.exp(b_last - b), vc))
        return new_state, out

    state, out = lax.scan(step, state0, (to_chunks(q), to_chunks(k), to_chunks(v), to_chunks(log_a)))
    return out.transpose(1, 0, 3, 2, 4).reshape(bsz, seq_len, n_heads, dv), state


def gla_mixer(h, state0, norm_g, w_in, w_gate2, b_gate, out_norm_g, w_out):
    bsz, seq_len, _ = h.shape
    u = rmsnorm(h, norm_g)
    proj = u @ w_in
    q, k, v, gate, g_low = jnp.split(
        proj, [A_KEY_WIDTH, 2 * A_KEY_WIDTH, 2 * A_KEY_WIDTH + BRANCH_WIDTH,
               2 * A_KEY_WIDTH + 2 * BRANCH_WIDTH], axis=-1)
    log_a = jax.nn.log_sigmoid((g_low @ w_gate2 + b_gate).astype(jnp.float32)) / A_GATE_NORMALIZER

    def heads(t, d):
        return t.reshape(bsz, seq_len, A_HEADS, d).astype(jnp.float32)

    o, state = gla_recurrence(heads(q, A_HEAD_DK) * (A_HEAD_DK ** -0.5), heads(k, A_HEAD_DK),
                              heads(v, A_HEAD_DV), heads(log_a, A_HEAD_DK),
                              state0.astype(jnp.float32))
    o = rmsnorm(o, out_norm_g).reshape(bsz, seq_len, BRANCH_WIDTH).astype(h.dtype)
    return (o * jax.nn.silu(gate)) @ w_out, state


def shared_kv(h, kv_norm, w_k, w_v, k_norm, pos):
    bsz, seq_len, _ = h.shape
    u = rmsnorm(h, kv_norm)
    k = rope(rmsnorm((u @ w_k).reshape(bsz, seq_len, B_KV_HEADS, B_HEAD_DIM), k_norm), pos)
    v = (u @ w_v).reshape(bsz, seq_len, B_KV_HEADS, B_HEAD_DIM)
    return k, v


def swa_query(h, norm_g, w_in, q_norm_g, pos):
    bsz, seq_len, _ = h.shape
    q, gate = jnp.split(rmsnorm(h, norm_g) @ w_in, [BRANCH_WIDTH], axis=-1)
    q = rope(rmsnorm(q.reshape(bsz, seq_len, B_HEADS, B_HEAD_DIM), q_norm_g), pos)
    return q, gate


def attend(q, k, v, qpos, kpos, sinks):
    bsz, lq = q.shape[:2]
    qg = q.reshape(bsz, lq, B_KV_HEADS, B_GROUP, B_HEAD_DIM)
    s = jnp.einsum('bqgrd,bkgd->bgrqk', qg, k).astype(jnp.float32) * (B_HEAD_DIM ** -0.5)
    dpos = qpos[:, None] - kpos[None, :]
    mask = (dpos >= 0) & (dpos <= WINDOW) & (kpos[None, :] >= 0)
    s = jnp.where(mask, s, -jnp.inf)
    sink = sinks.astype(jnp.float32).reshape(B_KV_HEADS, B_GROUP)[None, :, :, None, None]
    m = jnp.maximum(jnp.max(s, axis=-1, keepdims=True), sink)
    p = jnp.exp(s - m)
    p = p / (jnp.sum(p, axis=-1, keepdims=True) + jnp.exp(sink - m))
    o = jnp.einsum('bgrqk,bkgd->bqgrd', p.astype(v.dtype), v)
    return o.reshape(bsz, lq, B_HEADS, B_HEAD_DIM)


def swa_banded(q, k, v, sinks):
    bsz, seq_len = q.shape[:2]
    n_blocks = seq_len // SWA_BLOCK
    span = SWA_BLOCK + WINDOW
    pad = ((0, 0), (WINDOW, 0), (0, 0), (0, 0))
    kpad, vpad = jnp.pad(k, pad), jnp.pad(v, pad)

    def block(i):
        start = i * SWA_BLOCK
        qb = lax.dynamic_slice_in_dim(q, start, SWA_BLOCK, axis=1)
        kb = lax.dynamic_slice_in_dim(kpad, start, span, axis=1)
        vb = lax.dynamic_slice_in_dim(vpad, start, span, axis=1)
        qpos = start + jnp.arange(SWA_BLOCK, dtype=jnp.int32)
        kpos = start - WINDOW + jnp.arange(span, dtype=jnp.int32)
        return attend(qb, kb, vb, qpos, kpos, sinks)

    o = lax.map(block, jnp.arange(n_blocks, dtype=jnp.int32))
    return o.transpose(1, 0, 2, 3, 4).reshape(bsz, seq_len, B_HEADS, B_HEAD_DIM)


def swa_out(o, gate, w_out):
    bsz, seq_len = o.shape[:2]
    return (o.reshape(bsz, seq_len, BRANCH_WIDTH) * jax.nn.silu(gate)) @ w_out


def setup_inputs(seed: int = 0) -> dict:
    key = jax.random.key(seed)
    ks = jax.random.split(key, 24)

    def nrm(k, shape, scale):
        return jax.random.normal(k, shape, jnp.float32) * scale

    win_buf = min(WINDOW, PAST_LEN)
    return {
        'x_prompt': nrm(ks[0], (BATCH, SEQ, D_MODEL), 1.0),
        'x_sample': nrm(ks[1], (DEC_BATCH, DEC_SEQ, D_MODEL), 1.0),
        'state_gla': nrm(ks[2], (N_A_LAYERS, DEC_BATCH, A_HEADS, A_HEAD_DK, A_HEAD_DV), 0.5),
        'cache_swa_k': nrm(ks[3], (DEC_BATCH, win_buf, B_KV_HEADS, B_HEAD_DIM), 1.0),
        'cache_swa_v': nrm(ks[4], (DEC_BATCH, win_buf, B_KV_HEADS, B_HEAD_DIM), 1.0),
        'a_norm': 1.0 + nrm(ks[5], (N_A_LAYERS, D_MODEL), 0.02),
        'a_w_in': nrm(ks[6], (N_A_LAYERS, D_MODEL, A_IN_WIDTH), D_MODEL ** -0.5),
        'a_w_gate2': nrm(ks[7], (N_A_LAYERS, A_GATE_RANK, A_KEY_WIDTH), A_GATE_RANK ** -0.5),
        'a_b_gate': nrm(ks[8], (N_A_LAYERS, A_KEY_WIDTH), 0.1),
        'a_out_norm': 1.0 + nrm(ks[9], (N_A_LAYERS, A_HEAD_DV), 0.02),
        'a_w_out': nrm(ks[10], (N_A_LAYERS, BRANCH_WIDTH, D_MODEL), BRANCH_WIDTH ** -0.5),
        'kv_norm': 1.0 + nrm(ks[11], (D_MODEL,), 0.02),
        'w_k': nrm(ks[12], (D_MODEL, B_KV_WIDTH), D_MODEL ** -0.5),
        'w_v': nrm(ks[13], (D_MODEL, B_KV_WIDTH), D_MODEL ** -0.5),
        'k_norm': 1.0 + nrm(ks[14], (B_HEAD_DIM,), 0.02),
        'b_norm': 1.0 + nrm(ks[15], (N_B_LAYERS, D_MODEL), 0.02),
        'b_w_in': nrm(ks[16], (N_B_LAYERS, D_MODEL, B_IN_WIDTH), D_MODEL ** -0.5),
        'b_q_norm': 1.0 + nrm(ks[17], (N_B_LAYERS, B_HEAD_DIM), 0.02),
        'b_sinks': nrm(ks[18], (N_B_LAYERS, B_HEADS), 0.5),
        'b_w_out': nrm(ks[19], (N_B_LAYERS, BRANCH_WIDTH, D_MODEL), BRANCH_WIDTH ** -0.5),
    }


def reference(x_prompt, x_sample, state_gla, cache_swa_k, cache_swa_v,
              a_norm, a_w_in, a_w_gate2, a_b_gate, a_out_norm, a_w_out,
              kv_norm, w_k, w_v, k_norm,
              b_norm, b_w_in, b_q_norm, b_sinks, b_w_out):
    bsz_p, seq_p = x_prompt.shape[:2]
    seq_s = x_sample.shape[1]
    win_buf = cache_swa_k.shape[1]
    pos_p = jnp.arange(seq_p, dtype=jnp.int32)
    pos_s = PAST_LEN + jnp.arange(seq_s, dtype=jnp.int32)
    kpos_s = PAST_LEN - win_buf + jnp.arange(win_buf + seq_s, dtype=jnp.int32)

    hp, hs = x_prompt, x_sample
    gla_states_p, gla_states_s = [], []
    for layer in range(DEPTH):
        if layer < N_A_LAYERS:
            i = layer
            w = (a_norm[i], a_w_in[i], a_w_gate2[i], a_b_gate[i], a_out_norm[i], a_w_out[i])
            zero_state = jnp.zeros((bsz_p, A_HEADS, A_HEAD_DK, A_HEAD_DV), jnp.float32)
            dp, st_p = gla_mixer(hp, zero_state, *w)
            ds, st_s = gla_mixer(hs, state_gla[i], *w)
            hp, hs = hp + dp, hs + ds
            gla_states_p.append(st_p.astype(x_prompt.dtype))
            gla_states_s.append(st_s.astype(state_gla.dtype))
            if layer == N_A_LAYERS - 1:
                k_p, v_p = shared_kv(hp, kv_norm, w_k, w_v, k_norm, pos_p)
                k_new, v_new = shared_kv(hs, kv_norm, w_k, w_v, k_norm, pos_s)
                k_s = jnp.concatenate([cache_swa_k.astype(k_new.dtype), k_new], axis=1)
                v_s = jnp.concatenate([cache_swa_v.astype(v_new.dtype), v_new], axis=1)
        else:
            j = layer - N_A_LAYERS
            q_p, g_p = swa_query(hp, b_norm[j], b_w_in[j], b_q_norm[j], pos_p)
            hp = hp + swa_out(swa_banded(q_p, k_p, v_p, b_sinks[j]), g_p, b_w_out[j])
            q_s, g_s = swa_query(hs, b_norm[j], b_w_in[j], b_q_norm[j], pos_s)
            hs = hs + swa_out(attend(q_s, k_s, v_s, pos_s, kpos_s, b_sinks[j]), g_s, b_w_out[j])

    prompt_keep = min(WINDOW, seq_p)
    new_state_gla_prompt = jnp.stack(gla_states_p, axis=0)
    new_state_gla_sample = jnp.stack(gla_states_s, axis=0)
    new_cache_swa_k_prompt = k_p[:, seq_p - prompt_keep:]
    new_cache_swa_v_prompt = v_p[:, seq_p - prompt_keep:]
    new_cache_swa_k_sample = k_s[:, seq_s:]
    new_cache_swa_v_sample = v_s[:, seq_s:]
    return (hp, hs, new_state_gla_prompt, new_state_gla_sample,
            new_cache_swa_k_prompt, new_cache_swa_v_prompt,
            new_cache_swa_k_sample, new_cache_swa_v_sample)
```

```python
import functools

import jax
import jax.numpy as jnp
from jax import lax
from jax.experimental import pallas as pl
from jax.experimental.pallas import tpu as pltpu

F32 = jnp.float32
BF16 = jnp.bfloat16

A_HEADS = 4
A_HEAD_DK = 128
A_HEAD_DV = 512
A_KEY_WIDTH = A_HEADS * A_HEAD_DK
BRANCH_WIDTH = A_HEADS * A_HEAD_DV
A_GATE_RANK = 16
A_GATE_NORMALIZER = 16.0
B_HEAD_DIM = 64
B_HEADS = BRANCH_WIDTH // B_HEAD_DIM
B_KV_HEADS = 4
B_GROUP = B_HEADS // B_KV_HEADS
B_KV_WIDTH = B_KV_HEADS * B_HEAD_DIM
WINDOW = 128
ROPE_THETA = 10000.0
RMS_EPS = 1e-6
PAST_LEN = 16384

LANES = 128
SUBLANES = 8
VMEM_LIMIT_BYTES = 56 * 1024 * 1024

PROMPT_TILE = 256
GLA_CHUNK = 128
GLA_SAFE_DECAY = 80.0
SAMPLE_BATCH_BLOCK = 8
PROJ_COL_BLOCK = 512


def _dot(a, b):
    return jnp.dot(a, b, preferred_element_type=F32)


def _dot_nt(a, b):
    return lax.dot_general(a, b, (((1,), (1,)), ((), ())), preferred_element_type=F32)


def _split2(x):
    hi = x.astype(BF16)
    lo = (x - hi.astype(F32)).astype(BF16)
    return hi, lo


def _rmsnorm(x, g):
    ms = jnp.mean(x * x, axis=-1, keepdims=True)
    return x * lax.rsqrt(ms + RMS_EPS) * g


def _log_sigmoid(x):
    return jnp.minimum(x, 0.0) - jnp.log1p(jnp.exp(-jnp.abs(x)))


def _silu(x):
    return x * (1.0 / (1.0 + jnp.exp(-x)))


def _head_norm(x, bd, gain):
    hi, lo = _split2(x * x)
    ss = _dot(hi, bd) + _dot(lo, bd)
    return x * lax.rsqrt(ss * (1.0 / B_HEAD_DIM) + RMS_EPS) * gain


def _rope_slab(x, cos, sin_signed, first_half):
    xr = jnp.where(first_half, pltpu.roll(x, 96, 1), pltpu.roll(x, 32, 1))
    return x * cos + xr * sin_signed


def _lane_masks(rows):
    lane = lax.broadcasted_iota(jnp.int32, (rows, LANES), 1)
    first_half = (lane % B_HEAD_DIM) < (B_HEAD_DIM // 2)
    lo64 = lane < B_HEAD_DIM
    return first_half, lo64


def _split_group(slab, g, lo64):
    swapped = pltpu.roll(slab, B_HEAD_DIM, 1)
    zero = jnp.zeros_like(slab)
    if g % 2 == 0:
        return jnp.where(lo64, slab, zero), jnp.where(lo64, zero, swapped)
    return jnp.where(lo64, swapped, zero), jnp.where(lo64, zero, slab)


def _attend_pair(q, kpad, vpad, mask, sink0, sink1, lo64):
    s = _dot_nt(q, kpad)
    nk = kpad.shape[0] // 2
    ps, invs = [], []
    for half, sink in ((0, sink0), (1, sink1)):
        sh = jnp.where(mask, s[:, half * nk:(half + 1) * nk], -jnp.inf)
        m = jnp.maximum(jnp.max(sh, axis=-1, keepdims=True), sink)
        p = jnp.exp(sh - m)
        den = jnp.sum(p, axis=-1, keepdims=True) + jnp.exp(sink - m)
        ps.append(p.astype(BF16))
        invs.append(1.0 / den)
    o = _dot(jnp.concatenate(ps, axis=1), vpad)
    return o * jnp.where(lo64, invs[0], invs[1])


def _gla_prompt_kernel(x_ref, an_ref, wqk_ref, wv_ref, wg_ref, wlow_ref, wg2_ref, bg_ref, on_ref, wout_ref,
                       h1_ref, st_ref,
                       qk_s, v_s, gate_s, b_s, o_s, og_s):
    tile = x_ref.shape[0]
    chunk = GLA_CHUNK
    n_chunks = tile // chunk
    dk, dv, kw = A_HEAD_DK, A_HEAD_DV, A_KEY_WIDTH

    @pl.when(pl.program_id(1) == 0)
    def _():
        st_ref[...] = jnp.zeros_like(st_ref)

    x = x_ref[...]
    u = _rmsnorm(x, an_ref[...]).astype(BF16)
    qk_s[...] = _dot(u, wqk_ref[...])
    v_s[...] = _dot(u, wv_ref[...])
    gate_s[...] = _dot(u, wg_ref[...])
    glow = _dot(u, wlow_ref[...]).astype(BF16)
    pre = _dot(glow, wg2_ref[...]) + bg_ref[...]
    loga = _log_sigmoid(pre) * (1.0 / A_GATE_NORMALIZER)

    row = lax.broadcasted_iota(jnp.int32, (chunk, chunk), 0)
    col = lax.broadcasted_iota(jnp.int32, (chunk, chunk), 1)
    lower = row >= col
    tri = jnp.where(lower, 1.0, 0.0).astype(BF16)
    for c in range(n_chunks):
        hi, lo = _split2(loga[c * chunk:(c + 1) * chunk])
        b_s[c * chunk:(c + 1) * chunk, :] = _dot(tri, hi) + _dot(tri, lo)
    safe = jnp.min(b_s[...]) >= -GLA_SAFE_DECAY

    scale = A_HEAD_DK ** -0.5
    for c in range(n_chunks):
        rows = slice(c * chunk, (c + 1) * chunk)
        b = b_s[rows, :]
        eb = jnp.exp(b)
        blast = b[chunk - 1:chunk, :]
        ekl = jnp.exp(blast - b)
        eblast = jnp.exp(blast)
        for h in range(A_HEADS):
            ks = slice(h * dk, (h + 1) * dk)
            vs = slice(h * dv, (h + 1) * dv)
            q = qk_s[rows, ks] * scale
            k = qk_s[rows, kw + h * dk: kw + (h + 1) * dk]
            vh = v_s[rows, vs].astype(BF16)
            state = st_ref[h]
            o_s[rows, vs] = _dot((q * eb[:, ks]).astype(BF16), state.astype(BF16))
            kd_t = jnp.transpose(k * ekl[:, ks]).astype(BF16)
            dec = jnp.transpose(jnp.broadcast_to(eblast[:, ks], (dk, dk)))
            st_ref[h] = state * jnp.concatenate([dec] * (dv // dk), axis=1) + _dot(kd_t, vh)

    @pl.when(safe)
    def _():
        for c in range(n_chunks):
            rows = slice(c * chunk, (c + 1) * chunk)
            b = b_s[rows, :]
            eb = jnp.exp(b)
            enb = jnp.exp(-b)
            for h in range(A_HEADS):
                ks = slice(h * dk, (h + 1) * dk)
                vs = slice(h * dv, (h + 1) * dv)
                q = qk_s[rows, ks] * scale
                k = qk_s[rows, kw + h * dk: kw + (h + 1) * dk]
                a = _dot_nt((q * eb[:, ks]).astype(BF16), (k * enb[:, ks]).astype(BF16))
                a = jnp.where(lower, a, 0.0).astype(BF16)
                o_s[rows, vs] += _dot(a, v_s[rows, vs].astype(BF16))

    @pl.when(jnp.logical_not(safe))
    def _():
        trow = lax.broadcasted_iota(jnp.int32, (chunk, dk), 0)
        sub8 = lax.broadcasted_iota(jnp.int32, (SUBLANES, 1), 0)
        for c in range(n_chunks):
            rows = slice(c * chunk, (c + 1) * chunk)
            for h in range(A_HEADS):
                ks = slice(h * dk, (h + 1) * dk)
                vs = slice(h * dv, (h + 1) * dv)
                q = qk_s[rows, ks] * scale
                b = b_s[rows, ks]

                def body(s, acc, c=c, h=h, q=q, b=b):
                    r8 = pl.multiple_of(c * chunk + (s // SUBLANES) * SUBLANES, SUBLANES)
                    pick = sub8 == s % SUBLANES

                    def row_of(ref, lanes):
                        return jnp.sum(jnp.where(pick, ref[pl.ds(r8, SUBLANES), lanes], 0.0), axis=0, keepdims=True)

                    brow = row_of(b_s, slice(h * dk, (h + 1) * dk))
                    krow = row_of(qk_s, slice(kw + h * dk, kw + (h + 1) * dk))
                    vrow = row_of(v_s, slice(h * dv, (h + 1) * dv))
                    w = jnp.where(trow >= s, jnp.exp(jnp.minimum(b - brow, 0.0)), 0.0)
                    colv = jnp.sum(q * krow * w, axis=-1, keepdims=True)
                    return acc + colv * vrow

                o_s[rows, vs] += lax.fori_loop(0, chunk, body, jnp.zeros((chunk, dv), F32))

    for h in range(A_HEADS):
        vs = slice(h * dv, (h + 1) * dv)
        o = _rmsnorm(o_s[:, vs], on_ref[...])
        og_s[:, vs] = (o * _silu(gate_s[:, vs])).astype(BF16)
    h1_ref[...] = _dot(og_s[...], wout_ref[...]) + x


def _gla_prompt(x, a_norm, wqk, wv, wg, wlow, wg2, bg, out_norm, wout):
    bsz, seq, d = x.shape
    tile = min(PROMPT_TILE, seq)
    assert seq % tile == 0 and tile % GLA_CHUNK == 0
    const = lambda shape: pl.BlockSpec(shape, lambda b, l: (0,) * len(shape), pipeline_mode=pl.Buffered(1))
    return pl.pallas_call(
        _gla_prompt_kernel,
        grid=(bsz, seq // tile),
        in_specs=[
            pl.BlockSpec((None, tile, d), lambda b, l: (b, l, 0)),
            const(a_norm.shape), const(wqk.shape), const(wv.shape), const(wg.shape), const(wlow.shape),
            const(wg2.shape), const(bg.shape), const(out_norm.shape), const(wout.shape),
        ],
        out_specs=[
            pl.BlockSpec((None, tile, d), lambda b, l: (b, l, 0)),
            pl.BlockSpec((None, A_HEADS, A_HEAD_DK, A_HEAD_DV), lambda b, l: (b, 0, 0, 0)),
        ],
        out_shape=[
            jax.ShapeDtypeStruct((bsz, seq, d), F32),
            jax.ShapeDtypeStruct((bsz, A_HEADS, A_HEAD_DK, A_HEAD_DV), F32),
        ],
        scratch_shapes=[
            pltpu.VMEM((tile, 2 * A_KEY_WIDTH), F32),
            pltpu.VMEM((tile, BRANCH_WIDTH), F32),
            pltpu.VMEM((tile, BRANCH_WIDTH), F32),
            pltpu.VMEM((tile, A_KEY_WIDTH), F32),
            pltpu.VMEM((tile, BRANCH_WIDTH), F32),
            pltpu.VMEM((tile, BRANCH_WIDTH), BF16),
        ],
        compiler_params=pltpu.CompilerParams(
            dimension_semantics=("arbitrary", "arbitrary"), vmem_limit_bytes=VMEM_LIMIT_BYTES),
        name="gla_prompt",
    )(x, a_norm, wqk, wv, wg, wlow, wg2, bg, out_norm, wout)


def _swa_prompt_kernel(h_ref, kvn_ref, wkv_ref, kn_ref, bd_ref, cos_ref, sin_ref, bn_ref, wq_ref, wg_ref,
                       qn_ref, sink_ref, wout_ref,
                       y_ref, kc_ref, vc_ref,
                       ktop_s, kbot_s, vtop_s, vbot_s, qb_s, gate_s, og_s):
    tile = h_ref.shape[0]
    l = pl.program_id(1)
    blk = WINDOW
    kvw = B_KV_WIDTH
    kv_bufs = (ktop_s, kbot_s, vtop_s, vbot_s)

    @pl.when(l == 0)
    def _():
        for buf in kv_bufs:
            buf[:, 0:WINDOW, :] = jnp.zeros((B_KV_HEADS, WINDOW, LANES), BF16)

    h = h_ref[...]
    cos = cos_ref[...]
    sin = sin_ref[...]
    first_half, lo64 = _lane_masks(tile)
    bd = bd_ref[...]

    kv = _dot(_rmsnorm(h, kvn_ref[...]).astype(BF16), wkv_ref[...])
    kn = _head_norm(kv[:, :kvw], bd, kn_ref[...])
    v = kv[:, kvw:]
    k_slabs = [_rope_slab(kn[:, s * LANES:(s + 1) * LANES], cos, sin, first_half) for s in range(kvw // LANES)]

    @pl.when(l == pl.num_programs(1) - 1)
    def _():
        kc_ref[...] = jnp.concatenate(k_slabs, axis=1)[tile - WINDOW:, :]
        vc_ref[...] = v[tile - WINDOW:, :]

    for g in range(B_KV_HEADS):
        s = g // 2
        top, bot = _split_group(k_slabs[s], g, lo64)
        ktop_s[g, WINDOW:WINDOW + tile, :] = top.astype(BF16)
        kbot_s[g, WINDOW:WINDOW + tile, :] = bot.astype(BF16)
        top, bot = _split_group(v[:, s * LANES:(s + 1) * LANES], g, lo64)
        vtop_s[g, WINDOW:WINDOW + tile, :] = top.astype(BF16)
        vbot_s[g, WINDOW:WINDOW + tile, :] = bot.astype(BF16)

    ub = _rmsnorm(h, bn_ref[...]).astype(BF16)
    qgain = qn_ref[...] * (B_HEAD_DIM ** -0.5)
    for c in range(BRANCH_WIDTH // kvw):
        qc = _head_norm(_dot(ub, wq_ref[:, c * kvw:(c + 1) * kvw]), bd, qgain)
        for s in range(kvw // LANES):
            lanes = slice(c * kvw + s * LANES, c * kvw + (s + 1) * LANES)
            qb_s[:, lanes] = _rope_slab(qc[:, s * LANES:(s + 1) * LANES], cos, sin, first_half).astype(BF16)
    gate_s[...] = _dot(ub, wg_ref[...])

    rowi = lax.broadcasted_iota(jnp.int32, (blk, 2 * blk), 0)
    coli = lax.broadcasted_iota(jnp.int32, (blk, 2 * blk), 1)
    band = jnp.logical_and(coli - rowi >= 0, coli - rowi <= WINDOW)
    _, lo64_b = _lane_masks(blk)
    for i in range(tile // blk):
        r0 = i * blk
        first_col = jnp.where(l * tile + r0 > 0, 0, WINDOW)
        mask = jnp.logical_and(band, coli >= first_col)
        for g in range(B_KV_HEADS):
            kpad = jnp.concatenate([ktop_s[g, r0:r0 + 2 * blk, :], kbot_s[g, r0:r0 + 2 * blk, :]], axis=0)
            vpad = jnp.concatenate([vtop_s[g, r0:r0 + 2 * blk, :], vbot_s[g, r0:r0 + 2 * blk, :]], axis=0)
            for j in range(B_GROUP // 2):
                lanes = slice((g * (B_GROUP // 2) + j) * LANES, (g * (B_GROUP // 2) + j + 1) * LANES)
                head = g * B_GROUP + 2 * j
                o = _attend_pair(qb_s[r0:r0 + blk, lanes], kpad, vpad, mask,
                                 sink_ref[head], sink_ref[head + 1], lo64_b)
                og_s[r0:r0 + blk, lanes] = (o * _silu(gate_s[r0:r0 + blk, lanes])).astype(BF16)

    for buf in kv_bufs:
        buf[:, 0:WINDOW, :] = buf[:, tile:tile + WINDOW, :]
    y_ref[...] = _dot(og_s[...], wout_ref[...]) + h


def _swa_prompt(h, kv_norm, wkv, k_norm, bd, cos, sin, b_norm, wq, wg, q_norm, sinks, wout):
    bsz, seq, d = h.shape
    tile = min(PROMPT_TILE, seq)
    assert seq % tile == 0 and tile % WINDOW == 0 and seq >= WINDOW
    const = lambda shape: pl.BlockSpec(shape, lambda b, l: (0,) * len(shape), pipeline_mode=pl.Buffered(1))
    kv_scratch = pltpu.VMEM((B_KV_HEADS, WINDOW + tile, LANES), BF16)
    return pl.pallas_call(
        _swa_prompt_kernel,
        grid=(bsz, seq // tile),
        in_specs=[
            pl.BlockSpec((None, tile, d), lambda b, l: (b, l, 0)),
            const(kv_norm.shape), const(wkv.shape), const(k_norm.shape), const(bd.shape),
            pl.BlockSpec((tile, LANES), lambda b, l: (l, 0)),
            pl.BlockSpec((tile, LANES), lambda b, l: (l, 0)),
            const(b_norm.shape), const(wq.shape), const(wg.shape), const(q_norm.shape),
            pl.BlockSpec(memory_space=pltpu.SMEM),
            const(wout.shape),
        ],
        out_specs=[
            pl.BlockSpec((None, tile, d), lambda b, l: (b, l, 0)),
            pl.BlockSpec((None, WINDOW, B_KV_WIDTH), lambda b, l: (b, 0, 0)),
            pl.BlockSpec((None, WINDOW, B_KV_WIDTH), lambda b, l: (b, 0, 0)),
        ],
        out_shape=[
            jax.ShapeDtypeStruct((bsz, seq, d), F32),
            jax.ShapeDtypeStruct((bsz, WINDOW, B_KV_WIDTH), F32),
            jax.ShapeDtypeStruct((bsz, WINDOW, B_KV_WIDTH), F32),
        ],
        scratch_shapes=[
            kv_scratch, kv_scratch, kv_scratch, kv_scratch,
            pltpu.VMEM((tile, BRANCH_WIDTH), BF16),
            pltpu.VMEM((tile, BRANCH_WIDTH), F32),
            pltpu.VMEM((tile, BRANCH_WIDTH), BF16),
        ],
        compiler_params=pltpu.CompilerParams(
            dimension_semantics=("arbitrary", "arbitrary"), vmem_limit_bytes=VMEM_LIMIT_BYTES),
        name="swa_prompt",
    )(h, kv_norm, wkv, k_norm, bd, cos, sin, b_norm, wq, wg, q_norm, sinks, wout)


def _norm_proj_kernel(x_ref, g_ref, w_ref, o_ref):
    o_ref[...] = _dot(_rmsnorm(x_ref[...], g_ref[...]).astype(BF16), w_ref[...])


def _norm_proj(x, g, w, name):
    m, d = x.shape
    n = w.shape[1]
    assert n % PROJ_COL_BLOCK == 0
    return pl.pallas_call(
        _norm_proj_kernel,
        grid=(n // PROJ_COL_BLOCK,),
        in_specs=[
            pl.BlockSpec((m, d), lambda j: (0, 0)),
            pl.BlockSpec((1, d), lambda j: (0, 0)),
            pl.BlockSpec((d, PROJ_COL_BLOCK), lambda j: (0, j)),
        ],
        out_specs=pl.BlockSpec((m, PROJ_COL_BLOCK), lambda j: (0, j)),
        out_shape=jax.ShapeDtypeStruct((m, n), F32),
        compiler_params=pltpu.CompilerParams(dimension_semantics=("arbitrary",)),
        name=name,
    )(x, g, w)


def _gla_out_kernel(o_ref, gate_ref, x_ref, on_ref, w_ref, y_ref, og_s):
    for h in range(A_HEADS):
        vs = slice(h * A_HEAD_DV, (h + 1) * A_HEAD_DV)
        o = _rmsnorm(o_ref[:, vs], on_ref[...])
        og_s[:, vs] = (o * _silu(gate_ref[:, vs])).astype(BF16)
    y_ref[...] = _dot(og_s[...], w_ref[...]) + x_ref[...]


def _gla_out(o, gate, x, out_norm, wout):
    m = o.shape[0]
    return pl.pallas_call(
        _gla_out_kernel,
        out_shape=jax.ShapeDtypeStruct(x.shape, F32),
        scratch_shapes=[pltpu.VMEM((m, BRANCH_WIDTH), BF16)],
        compiler_params=pltpu.CompilerParams(vmem_limit_bytes=VMEM_LIMIT_BYTES),
        name="gla_out_sample",
    )(o, gate, x, out_norm, wout)


def _out_proj_kernel(og_ref, x_ref, w_ref, y_ref):
    y_ref[...] = _dot(og_ref[...].astype(BF16), w_ref[...]) + x_ref[...]


def _out_proj(og, x, wout):
    return pl.pallas_call(
        _out_proj_kernel,
        out_shape=jax.ShapeDtypeStruct(x.shape, F32),
        compiler_params=pltpu.CompilerParams(vmem_limit_bytes=VMEM_LIMIT_BYTES),
        name="swa_out_sample",
    )(og, x, wout)


def _gla_sample_kernel(pj_ref, wg2_ref, bg_ref, st_ref, o_ref, nst_ref):
    nb, seq = pj_ref.shape[0], pj_ref.shape[1]
    dk, dv, kw = A_HEAD_DK, A_HEAD_DV, A_KEY_WIDTH
    scale = A_HEAD_DK ** -0.5
    pad_rows = SUBLANES - seq

    def per_seq(i, carry):
        pj = pj_ref[i]
        q = pj[:, 0:kw] * scale
        k = pj[:, kw:2 * kw]
        v = pj[:, 2 * kw:2 * kw + BRANCH_WIDTH]
        glow = pj[:, 2 * kw + 2 * BRANCH_WIDTH:2 * kw + 2 * BRANCH_WIDTH + LANES]
        glow8 = jnp.concatenate([glow, jnp.zeros((pad_rows, LANES), F32)], axis=0).astype(BF16)
        pre = _dot(glow8, wg2_ref[...])[0:seq] + bg_ref[...]
        loga = _log_sigmoid(pre) * (1.0 / A_GATE_NORMALIZER)
        bs = [loga[0:1]]
        for t in range(1, seq):
            bs.append(bs[-1] + loga[t:t + 1])
        bmat = jnp.concatenate(bs, axis=0)
        blast = bs[-1]
        qt = q * jnp.exp(bmat)
        kd = k * jnp.exp(blast - bmat)
        eblast = jnp.exp(blast)

        outs = [[jnp.zeros((1, dv), F32) for _ in range(A_HEADS)] for _ in range(seq)]
        for t in range(seq):
            for s in range(t + 1):
                w = q[t:t + 1] * k[s:s + 1] * jnp.exp(bs[t] - bs[s])
                for h in range(A_HEADS):
                    a = jnp.sum(w[:, h * dk:(h + 1) * dk], axis=-1, keepdims=True)
                    outs[t][h] = outs[t][h] + a * v[s:s + 1, h * dv:(h + 1) * dv]
        intra = jnp.concatenate([jnp.concatenate(outs[t], axis=1) for t in range(seq)], axis=0)

        qt8 = jnp.concatenate([qt, jnp.zeros((pad_rows, kw), F32)], axis=0).astype(BF16)
        inter = []
        for h in range(A_HEADS):
            ks = slice(h * dk, (h + 1) * dk)
            state = st_ref[i, h]
            inter.append(_dot(qt8[:, ks], state.astype(BF16))[0:seq])
            m = jnp.concatenate([kd[:, ks], eblast[:, ks], jnp.zeros((dk - seq - 1, dk), F32)], axis=0)
            mt = jnp.transpose(m)
            vpad = jnp.concatenate([v[:, h * dv:(h + 1) * dv], jnp.zeros((dk - seq, dv), F32)], axis=0)
            nst_ref[i, h] = state * mt[:, seq:seq + 1] + _dot(mt.astype(BF16), vpad.astype(BF16))
        o_ref[i] = intra + jnp.concatenate(inter, axis=1)
        return carry

    lax.fori_loop(0, nb, per_seq, 0)


def _gla_sample(pj, wg2, bg, state):
    nbatch, seq, n = pj.shape
    nb = min(SAMPLE_BATCH_BLOCK // 2, nbatch)
    assert nbatch % nb == 0
    st_spec = pl.BlockSpec((nb, A_HEADS, A_HEAD_DK, A_HEAD_DV), lambda b: (b, 0, 0, 0))
    return pl.pallas_call(
        _gla_sample_kernel,
        grid=(nbatch // nb,),
        in_specs=[
            pl.BlockSpec((nb, seq, n), lambda b: (b, 0, 0)),
            pl.BlockSpec(wg2.shape, lambda b: (0, 0)),
            pl.BlockSpec(bg.shape, lambda b: (0, 0)),
            st_spec,
        ],
        out_specs=[pl.BlockSpec((nb, seq, BRANCH_WIDTH), lambda b: (b, 0, 0)), st_spec],
        out_shape=[
            jax.ShapeDtypeStruct((nbatch, seq, BRANCH_WIDTH), F32),
            jax.ShapeDtypeStruct(state.shape, F32),
        ],
        compiler_params=pltpu.CompilerParams(
            dimension_semantics=("arbitrary",), vmem_limit_bytes=VMEM_LIMIT_BYTES),
        name="gla_sample",
    )(pj, wg2, bg, state)


def _swa_sample_kernel(pj_ref, ck_ref, cv_ref, kn_ref, bd_ref, cos_ref, sin_ref, qn_ref, sink_ref,
                       og_ref, nk_ref, nv_ref):
    nb, seq = pj_ref.shape[0], pj_ref.shape[1]
    kvw = B_KV_WIDTH
    qrows = SUBLANES
    nkeys = 2 * WINDOW
    cos = cos_ref[...]
    sin = sin_ref[...]
    bd = bd_ref[...]
    first_half, lo64 = _lane_masks(qrows)
    lo64_k = lax.broadcasted_iota(jnp.int32, (nkeys, LANES), 1) < B_HEAD_DIM
    qgain = qn_ref[...] * (B_HEAD_DIM ** -0.5)
    rowi = lax.broadcasted_iota(jnp.int32, (qrows, nkeys), 0)
    coli = lax.broadcasted_iota(jnp.int32, (qrows, nkeys), 1)
    mask = jnp.logical_and(coli - rowi >= 0, coli - rowi <= WINDOW)
    zpad = jnp.zeros((qrows - seq, kvw), F32)

    def per_seq(i, carry):
        pj = pj_ref[i]
        kv8 = jnp.concatenate([pj[:, 0:2 * kvw], jnp.zeros((qrows - seq, 2 * kvw), F32)], axis=0)
        kn = _head_norm(kv8[:, :kvw], bd, kn_ref[...])
        k_new = jnp.concatenate(
            [_rope_slab(kn[:, s * LANES:(s + 1) * LANES], cos, sin, first_half) for s in range(kvw // LANES)], axis=1)
        v_new = kv8[:, kvw:]
        ck = ck_ref[i]
        cv = cv_ref[i]
        nk_ref[i, 0:WINDOW - seq, :] = ck[seq:, :]
        nk_ref[i, WINDOW - seq:WINDOW, :] = k_new[0:seq]
        nv_ref[i, 0:WINDOW - seq, :] = cv[seq:, :]
        nv_ref[i, WINDOW - seq:WINDOW, :] = v_new[0:seq]

        zeros_tail = jnp.zeros((WINDOW - qrows, kvw), F32)
        k_all = jnp.concatenate([ck, k_new[0:seq], zpad, zeros_tail], axis=0)
        v_all = jnp.concatenate([cv, v_new[0:seq], zpad, zeros_tail], axis=0)

        q8 = jnp.concatenate([pj[:, 2 * kvw:2 * kvw + BRANCH_WIDTH],
                              jnp.zeros((qrows - seq, BRANCH_WIDTH), F32)], axis=0)
        gate = pj[:, 2 * kvw + BRANCH_WIDTH:2 * kvw + 2 * BRANCH_WIDTH]
        out_slabs = [None] * (BRANCH_WIDTH // LANES)
        q_slabs = []
        for c in range(BRANCH_WIDTH // kvw):
            qc = _head_norm(q8[:, c * kvw:(c + 1) * kvw], bd, qgain)
            for s in range(kvw // LANES):
                q_slabs.append(_rope_slab(qc[:, s * LANES:(s + 1) * LANES], cos, sin, first_half).astype(BF16))
        for g in range(B_KV_HEADS):
            s = g // 2
            ktop, kbot = _split_group(k_all[:, s * LANES:(s + 1) * LANES], g, lo64_k)
            vtop, vbot = _split_group(v_all[:, s * LANES:(s + 1) * LANES], g, lo64_k)
            kpad = jnp.concatenate([ktop, kbot], axis=0).astype(BF16)
            vpad = jnp.concatenate([vtop, vbot], axis=0).astype(BF16)
            for j in range(B_GROUP // 2):
                idx = g * (B_GROUP // 2) + j
                head = g * B_GROUP + 2 * j
                out_slabs[idx] = _attend_pair(q_slabs[idx], kpad, vpad, mask,
                                              sink_ref[head], sink_ref[head + 1], lo64)
        o = jnp.concatenate(out_slabs, axis=1)[0:seq]
        og_ref[i] = o * _silu(gate)
        return carry

    lax.fori_loop(0, nb, per_seq, 0)


def _swa_sample(pj, ck, cv, k_norm, bd, cos, sin, q_norm, sinks):
    nbatch, seq, n = pj.shape
    nb = min(SAMPLE_BATCH_BLOCK, nbatch)
    assert nbatch % nb == 0 and ck.shape[1] == WINDOW
    cache_spec = pl.BlockSpec((nb, WINDOW, B_KV_WIDTH), lambda b: (b, 0, 0))
    full = lambda a: pl.BlockSpec(a.shape, lambda b: (0,) * a.ndim)
    return pl.pallas_call(
        _swa_sample_kernel,
        grid=(nbatch // nb,),
        in_specs=[
            pl.BlockSpec((nb, seq, n), lambda b: (b, 0, 0)),
            cache_spec, cache_spec,
            full(k_norm), full(bd), full(cos), full(sin), full(q_norm),
            pl.BlockSpec(memory_space=pltpu.SMEM),
        ],
        out_specs=[pl.BlockSpec((nb, seq, BRANCH_WIDTH), lambda b: (b, 0, 0)), cache_spec, cache_spec],
        out_shape=[
            jax.ShapeDtypeStruct((nbatch, seq, BRANCH_WIDTH), F32),
            jax.ShapeDtypeStruct(ck.shape, F32),
            jax.ShapeDtypeStruct(cv.shape, F32),
        ],
        compiler_params=pltpu.CompilerParams(
            dimension_semantics=("arbitrary",), vmem_limit_bytes=VMEM_LIMIT_BYTES),
        name="swa_sample",
    )(pj, ck, cv, k_norm, bd, cos, sin, q_norm, sinks)


def _rope_tables(pos, rows):
    half = B_HEAD_DIM // 2
    inv_freq = ROPE_THETA ** (-jnp.arange(half, dtype=F32) / half)
    ang = pos.astype(F32)[:, None] * inv_freq[None, :]
    cos, sin = jnp.cos(ang), jnp.sin(ang)
    cos_t = jnp.concatenate([cos, cos, cos, cos], axis=1)
    sin_t = jnp.concatenate([-sin, sin, -sin, sin], axis=1)
    pad = rows - pos.shape[0]
    if pad:
        cos_t = jnp.pad(cos_t, ((0, pad), (0, 0)))
        sin_t = jnp.pad(sin_t, ((0, pad), (0, 0)))
    return cos_t, sin_t


def kernel(x_prompt, x_sample, state_gla, cache_swa_k, cache_swa_v, a_norm, a_w_in, a_w_gate2, a_b_gate,
           a_out_norm, a_w_out, kv_norm, w_k, w_v, k_norm, b_norm, b_w_in, b_q_norm, b_sinks, b_w_out):
    assert a_norm.shape[0] == 1 and b_norm.shape[0] == 1
    bsz_p, seq_p, d = x_prompt.shape
    bsz_s, seq_s, _ = x_sample.shape
    kw, bw = A_KEY_WIDTH, BRANCH_WIDTH

    w_in = a_w_in[0]
    wa_qk = w_in[:, :2 * kw].astype(BF16)
    wa_v = w_in[:, 2 * kw:2 * kw + bw].astype(BF16)
    wa_g = w_in[:, 2 * kw + bw:2 * kw + 2 * bw].astype(BF16)
    wa_low = jnp.pad(w_in[:, 2 * kw + 2 * bw:], ((0, 0), (0, LANES - A_GATE_RANK))).astype(BF16)
    wa_g2 = jnp.pad(a_w_gate2[0], ((0, LANES - A_GATE_RANK), (0, 0))).astype(BF16)
    a_bg = a_b_gate[0][None, :]
    a_n = a_norm[0][None, :]
    a_on = a_out_norm[0][None, :]
    wa_out = a_w_out[0].astype(BF16)
    w_kv = jnp.concatenate([w_k, w_v], axis=1).astype(BF16)
    wb_q = b_w_in[0][:, :bw].astype(BF16)
    wb_g = b_w_in[0][:, bw:].astype(BF16)
    wb_out = b_w_out[0].astype(BF16)
    kvn = kv_norm[None, :]
    bn = b_norm[0][None, :]
    kn_t = jnp.tile(k_norm, B_KV_WIDTH // B_HEAD_DIM)[None, :]
    qn_t = jnp.tile(b_q_norm[0], B_KV_WIDTH // B_HEAD_DIM)[None, :]
    sinks = b_sinks[0]
    grp = jnp.arange(B_KV_WIDTH) // B_HEAD_DIM
    bd = (grp[:, None] == grp[None, :]).astype(BF16)

    cos_p, sin_p = _rope_tables(jnp.arange(seq_p, dtype=jnp.int32), seq_p)
    h1_p, st_p = _gla_prompt(x_prompt, a_n, wa_qk, wa_v, wa_g, wa_low, wa_g2, a_bg, a_on, wa_out)
    y_p, kc_p, vc_p = _swa_prompt(h1_p, kvn, w_kv, kn_t, bd, cos_p, sin_p, bn, wb_q, wb_g, qn_t, sinks, wb_out)

    m = bsz_s * seq_s
    xs = x_sample.reshape(m, d)
    low_pad = PROJ_COL_BLOCK - A_GATE_RANK
    wa_all = jnp.concatenate(
        [wa_qk, wa_v, wa_g, jnp.pad(w_in[:, 2 * kw + 2 * bw:], ((0, 0), (0, low_pad))).astype(BF16)], axis=1)
    pj_a = _norm_proj(xs, a_n, wa_all, "gla_proj_sample")
    o_s, st_s = _gla_sample(pj_a.reshape(bsz_s, seq_s, -1), wa_g2, a_bg, state_gla[0])
    gate_a = pj_a[:, 2 * kw + bw:2 * kw + 2 * bw]
    h1_s = _gla_out(o_s.reshape(m, bw), gate_a, xs, a_on, wa_out)

    pj_kv = _norm_proj(h1_s, kvn, w_kv, "kv_proj_sample")
    pj_qg = _norm_proj(h1_s, bn, jnp.concatenate([wb_q, wb_g], axis=1), "q_proj_sample")
    pj_b = jnp.concatenate([pj_kv, pj_qg], axis=1).reshape(bsz_s, seq_s, -1)
    pos_s = PAST_LEN + jnp.arange(seq_s, dtype=jnp.int32)
    cos_s, sin_s = _rope_tables(pos_s, SUBLANES)
    og_s, nk_s, nv_s = _swa_sample(
        pj_b, cache_swa_k.reshape(bsz_s, WINDOW, B_KV_WIDTH), cache_swa_v.reshape(bsz_s, WINDOW, B_KV_WIDTH),
        kn_t, bd, cos_s, sin_s, qn_t, sinks)
    y_s = _out_proj(og_s.reshape(m, bw), h1_s, wb_out)

    return (y_p, y_s.reshape(bsz_s, seq_s, d),
            st_p[None], st_s[None],
            kc_p.reshape(bsz_p, WINDOW, B_KV_HEADS, B_HEAD_DIM), vc_p.reshape(bsz_p, WINDOW, B_KV_HEADS, B_HEAD_DIM),
            nk_s.reshape(bsz_s, WINDOW, B_KV_HEADS, B_HEAD_DIM), nv_s.reshape(bsz_s, WINDOW, B_KV_HEADS, B_HEAD_DIM))
```

```python
import functools

import jax
import jax.numpy as jnp
from jax import lax
from jax.experimental import pallas as pl
from jax.experimental.pallas import tpu as pltpu

F32 = jnp.float32
BF16 = jnp.bfloat16

A_HEADS = 4
A_HEAD_DK = 128
A_HEAD_DV = 512
A_KEY_WIDTH = A_HEADS * A_HEAD_DK
BRANCH_WIDTH = A_HEADS * A_HEAD_DV
A_GATE_RANK = 16
A_GATE_NORMALIZER = 16.0
B_HEAD_DIM = 64
B_HEADS = BRANCH_WIDTH // B_HEAD_DIM
B_KV_HEADS = 4
B_GROUP = B_HEADS // B_KV_HEADS
B_KV_WIDTH = B_KV_HEADS * B_HEAD_DIM
WINDOW = 128
ROPE_THETA = 10000.0
RMS_EPS = 1e-6
PAST_LEN = 16384

LANES = 128
SUBLANES = 8
VMEM_LIMIT_BYTES = 56 * 1024 * 1024

PROMPT_TILE = 256
GLA_CHUNK = 128
GLA_SAFE_DECAY = 80.0
SAMPLE_BATCH_BLOCK = 8
PROJ_COL_BLOCK = 512


def _dot(a, b):
    return jnp.dot(a, b, preferred_element_type=F32)


def _dot_nt(a, b):
    return lax.dot_general(a, b, (((1,), (1,)), ((), ())), preferred_element_type=F32)


def _split2(x):
    hi = x.astype(BF16)
    lo = (x - hi.astype(F32)).astype(BF16)
    return hi, lo


def _rmsnorm(x, g):
    ms = jnp.mean(x * x, axis=-1, keepdims=True)
    return x * lax.rsqrt(ms + RMS_EPS) * g


def _log_sigmoid(x):
    return jnp.minimum(x, 0.0) - jnp.log1p(jnp.exp(-jnp.abs(x)))


def _silu(x):
    return x * (1.0 / (1.0 + jnp.exp(-x)))


def _head_norm(x, bd, gain):
    hi, lo = _split2(x * x)
    ss = _dot(hi, bd) + _dot(lo, bd)
    return x * lax.rsqrt(ss * (1.0 / B_HEAD_DIM) + RMS_EPS) * gain


def _rope_slab(x, cos, sin_signed, first_half):
    xr = jnp.where(first_half, pltpu.roll(x, 96, 1), pltpu.roll(x, 32, 1))
    return x * cos + xr * sin_signed


def _lane_masks(rows):
    lane = lax.broadcasted_iota(jnp.int32, (rows, LANES), 1)
    first_half = (lane % B_HEAD_DIM) < (B_HEAD_DIM // 2)
    lo64 = lane < B_HEAD_DIM
    return first_half, lo64


def _split_group(slab, g, lo64):
    swapped = pltpu.roll(slab, B_HEAD_DIM, 1)
    zero = jnp.zeros_like(slab)
    if g % 2 == 0:
        return jnp.where(lo64, slab, zero), jnp.where(lo64, zero, swapped)
    return jnp.where(lo64, swapped, zero), jnp.where(lo64, zero, slab)


def _attend_group(qg, kpad, vpad, mask, sinks, lo64, rows):
    s = _dot_nt(qg, kpad)
    nk = kpad.shape[0] // 2
    ps, invs = [], []
    for j in range(len(sinks) // 2):
        pj, ij = [], []
        for half in range(2):
            sink = sinks[2 * j + half]
            sh = jnp.where(mask, s[j * rows:(j + 1) * rows, half * nk:(half + 1) * nk], -jnp.inf)
            m = jnp.maximum(jnp.max(sh, axis=-1, keepdims=True), sink)
            p = jnp.exp(sh - m)
            den = jnp.sum(p, axis=-1, keepdims=True) + jnp.exp(sink - m)
            pj.append(p.astype(BF16))
            ij.append(1.0 / den)
        ps.append(jnp.concatenate(pj, axis=1))
        invs.append(jnp.where(lo64, ij[0], ij[1]))
    o = _dot(jnp.concatenate(ps, axis=0), vpad)
    return [o[j * rows:(j + 1) * rows] * invs[j] for j in range(len(invs))]


def _gla_prompt_kernel(x_ref, an_ref, wqk_ref, wv_ref, wg_ref, wlow_ref, wg2_ref, bg_ref, on_ref, wout_ref,
                       h1_ref, st_ref,
                       qk_s, v_s, gate_s, b_s, o_s, og_s):
    tile = x_ref.shape[0]
    chunk = GLA_CHUNK
    n_chunks = tile // chunk
    dk, dv, kw = A_HEAD_DK, A_HEAD_DV, A_KEY_WIDTH

    @pl.when(pl.program_id(1) == 0)
    def _():
        st_ref[...] = jnp.zeros_like(st_ref)

    x = x_ref[...]
    u = _rmsnorm(x, an_ref[...]).astype(BF16)
    qk_s[...] = _dot(u, wqk_ref[...])
    v_s[...] = _dot(u, wv_ref[...])
    gate_s[...] = _dot(u, wg_ref[...])
    glow = _dot(u, wlow_ref[...]).astype(BF16)
    pre = _dot(glow, wg2_ref[...]) + bg_ref[...]
    loga = _log_sigmoid(pre) * (1.0 / A_GATE_NORMALIZER)

    row = lax.broadcasted_iota(jnp.int32, (chunk, chunk), 0)
    col = lax.broadcasted_iota(jnp.int32, (chunk, chunk), 1)
    lower = row >= col
    tri = jnp.where(lower, 1.0, 0.0).astype(BF16)
    for c in range(n_chunks):
        hi, lo = _split2(loga[c * chunk:(c + 1) * chunk])
        b_s[c * chunk:(c + 1) * chunk, :] = _dot(tri, hi) + _dot(tri, lo)
    safe = jnp.min(b_s[...]) >= -GLA_SAFE_DECAY

    scale = A_HEAD_DK ** -0.5
    for c in range(n_chunks):
        rows = slice(c * chunk, (c + 1) * chunk)
        b = b_s[rows, :]
        eb = jnp.exp(b)
        blast = b[chunk - 1:chunk, :]
        ekl = jnp.exp(blast - b)
        eblast = jnp.exp(blast)
        for h in range(A_HEADS):
            ks = slice(h * dk, (h + 1) * dk)
            vs = slice(h * dv, (h + 1) * dv)
            q = qk_s[rows, ks] * scale
            k = qk_s[rows, kw + h * dk: kw + (h + 1) * dk]
            vh = v_s[rows, vs].astype(BF16)
            state = st_ref[h]
            o_s[rows, vs] = _dot((q * eb[:, ks]).astype(BF16), state.astype(BF16))
            kd_t = jnp.transpose(k * ekl[:, ks]).astype(BF16)
            dec = jnp.transpose(jnp.broadcast_to(eblast[:, ks], (dk, dk)))
            st_ref[h] = state * jnp.concatenate([dec] * (dv // dk), axis=1) + _dot(kd_t, vh)

    @pl.when(safe)
    def _():
        for c in range(n_chunks):
            rows = slice(c * chunk, (c + 1) * chunk)
            b = b_s[rows, :]
            eb = jnp.exp(b)
            enb = jnp.exp(-b)
            for h in range(A_HEADS):
                ks = slice(h * dk, (h + 1) * dk)
                vs = slice(h * dv, (h + 1) * dv)
                q = qk_s[rows, ks] * scale
                k = qk_s[rows, kw + h * dk: kw + (h + 1) * dk]
                a = _dot_nt((q * eb[:, ks]).astype(BF16), (k * enb[:, ks]).astype(BF16))
                a = jnp.where(lower, a, 0.0).astype(BF16)
                o_s[rows, vs] += _dot(a, v_s[rows, vs].astype(BF16))

    @pl.when(jnp.logical_not(safe))
    def _():
        trow = lax.broadcasted_iota(jnp.int32, (chunk, dk), 0)
        sub8 = lax.broadcasted_iota(jnp.int32, (SUBLANES, 1), 0)
        for c in range(n_chunks):
            rows = slice(c * chunk, (c + 1) * chunk)
            for h in range(A_HEADS):
                ks = slice(h * dk, (h + 1) * dk)
                vs = slice(h * dv, (h + 1) * dv)
                q = qk_s[rows, ks] * scale
                b = b_s[rows, ks]

                def body(s, acc, c=c, h=h, q=q, b=b):
                    r8 = pl.multiple_of(c * chunk + (s // SUBLANES) * SUBLANES, SUBLANES)
                    pick = sub8 == s % SUBLANES

                    def row_of(ref, lanes):
                        return jnp.sum(jnp.where(pick, ref[pl.ds(r8, SUBLANES), lanes], 0.0), axis=0, keepdims=True)

                    brow = row_of(b_s, slice(h * dk, (h + 1) * dk))
                    krow = row_of(qk_s, slice(kw + h * dk, kw + (h + 1) * dk))
                    vrow = row_of(v_s, slice(h * dv, (h + 1) * dv))
                    w = jnp.where(trow >= s, jnp.exp(jnp.minimum(b - brow, 0.0)), 0.0)
                    colv = jnp.sum(q * krow * w, axis=-1, keepdims=True)
                    return acc + colv * vrow

                o_s[rows, vs] += lax.fori_loop(0, chunk, body, jnp.zeros((chunk, dv), F32))

    for h in range(A_HEADS):
        vs = slice(h * dv, (h + 1) * dv)
        o = _rmsnorm(o_s[:, vs], on_ref[...])
        og_s[:, vs] = (o * _silu(gate_s[:, vs])).astype(BF16)
    h1_ref[...] = _dot(og_s[...], wout_ref[...]) + x


def _gla_prompt(x, a_norm, wqk, wv, wg, wlow, wg2, bg, out_norm, wout):
    bsz, seq, d = x.shape
    tile = min(PROMPT_TILE, seq)
    assert seq % tile == 0 and tile % GLA_CHUNK == 0
    const = lambda shape: pl.BlockSpec(shape, lambda b, l: (0,) * len(shape), pipeline_mode=pl.Buffered(1))
    return pl.pallas_call(
        _gla_prompt_kernel,
        grid=(bsz, seq // tile),
        in_specs=[
            pl.BlockSpec((None, tile, d), lambda b, l: (b, l, 0)),
            const(a_norm.shape), const(wqk.shape), const(wv.shape), const(wg.shape), const(wlow.shape),
            const(wg2.shape), const(bg.shape), const(out_norm.shape), const(wout.shape),
        ],
        out_specs=[
            pl.BlockSpec((None, tile, d), lambda b, l: (b, l, 0)),
            pl.BlockSpec((None, A_HEADS, A_HEAD_DK, A_HEAD_DV), lambda b, l: (b, 0, 0, 0)),
        ],
        out_shape=[
            jax.ShapeDtypeStruct((bsz, seq, d), F32),
            jax.ShapeDtypeStruct((bsz, A_HEADS, A_HEAD_DK, A_HEAD_DV), F32),
        ],
        scratch_shapes=[
            pltpu.VMEM((tile, 2 * A_KEY_WIDTH), F32),
            pltpu.VMEM((tile, BRANCH_WIDTH), F32),
            pltpu.VMEM((tile, BRANCH_WIDTH), F32),
            pltpu.VMEM((tile, A_KEY_WIDTH), F32),
            pltpu.VMEM((tile, BRANCH_WIDTH), F32),
            pltpu.VMEM((tile, BRANCH_WIDTH), BF16),
        ],
        compiler_params=pltpu.CompilerParams(
            dimension_semantics=("arbitrary", "arbitrary"), vmem_limit_bytes=VMEM_LIMIT_BYTES),
        name="gla_prompt",
    )(x, a_norm, wqk, wv, wg, wlow, wg2, bg, out_norm, wout)


def _swa_prompt_kernel(h_ref, kvn_ref, wkv_ref, kn_ref, bd_ref, cos_ref, sin_ref, bn_ref, wq_ref, wg_ref,
                       qn_ref, sink_ref, wout_ref,
                       y_ref, kc_ref, vc_ref,
                       ktop_s, kbot_s, vtop_s, vbot_s, qb_s, gate_s, og_s):
    tile = h_ref.shape[0]
    l = pl.program_id(1)
    blk = WINDOW
    kvw = B_KV_WIDTH
    kv_bufs = (ktop_s, kbot_s, vtop_s, vbot_s)

    @pl.when(l == 0)
    def _():
        for buf in kv_bufs:
            buf[:, 0:WINDOW, :] = jnp.zeros((B_KV_HEADS, WINDOW, LANES), BF16)

    h = h_ref[...]
    cos = cos_ref[...]
    sin = sin_ref[...]
    first_half, lo64 = _lane_masks(tile)
    bd = bd_ref[...]

    kv = _dot(_rmsnorm(h, kvn_ref[...]).astype(BF16), wkv_ref[...])
    kn = _head_norm(kv[:, :kvw], bd, kn_ref[...])
    v = kv[:, kvw:]
    k_slabs = [_rope_slab(kn[:, s * LANES:(s + 1) * LANES], cos, sin, first_half) for s in range(kvw // LANES)]

    @pl.when(l == pl.num_programs(1) - 1)
    def _():
        kc_ref[...] = jnp.concatenate(k_slabs, axis=1)[tile - WINDOW:, :]
        vc_ref[...] = v[tile - WINDOW:, :]

    for g in range(B_KV_HEADS):
        s = g // 2
        top, bot = _split_group(k_slabs[s], g, lo64)
        ktop_s[g, WINDOW:WINDOW + tile, :] = top.astype(BF16)
        kbot_s[g, WINDOW:WINDOW + tile, :] = bot.astype(BF16)
        top, bot = _split_group(v[:, s * LANES:(s + 1) * LANES], g, lo64)
        vtop_s[g, WINDOW:WINDOW + tile, :] = top.astype(BF16)
        vbot_s[g, WINDOW:WINDOW + tile, :] = bot.astype(BF16)

    ub = _rmsnorm(h, bn_ref[...]).astype(BF16)
    qgain = qn_ref[...] * (B_HEAD_DIM ** -0.5)
    for c in range(BRANCH_WIDTH // kvw):
        qc = _head_norm(_dot(ub, wq_ref[:, c * kvw:(c + 1) * kvw]), bd, qgain)
        for s in range(kvw // LANES):
            qr = _rope_slab(qc[:, s * LANES:(s + 1) * LANES], cos, sin, first_half).astype(BF16)
            for i in range(tile // blk):
                qb_s[i, c * (kvw // LANES) + s] = qr[i * blk:(i + 1) * blk]
    gate_s[...] = _dot(ub, wg_ref[...])

    rowi = lax.broadcasted_iota(jnp.int32, (blk, 2 * blk), 0)
    coli = lax.broadcasted_iota(jnp.int32, (blk, 2 * blk), 1)
    band = jnp.logical_and(coli - rowi >= 0, coli - rowi <= WINDOW)
    _, lo64_b = _lane_masks(blk)
    for i in range(tile // blk):
        r0 = i * blk
        first_col = jnp.where(l * tile + r0 > 0, 0, WINDOW)
        mask = jnp.logical_and(band, coli >= first_col)
        for g in range(B_KV_HEADS):
            kpad = jnp.concatenate([ktop_s[g, r0:r0 + 2 * blk, :], kbot_s[g, r0:r0 + 2 * blk, :]], axis=0)
            vpad = jnp.concatenate([vtop_s[g, r0:r0 + 2 * blk, :], vbot_s[g, r0:r0 + 2 * blk, :]], axis=0)
            pairs = B_GROUP // 2
            qg = qb_s[i, g * pairs:(g + 1) * pairs].reshape(pairs * blk, LANES)
            outs = _attend_group(qg, kpad, vpad, mask, [sink_ref[g * B_GROUP + r] for r in range(B_GROUP)],
                                 lo64_b, blk)
            for j in range(pairs):
                lanes = slice((g * pairs + j) * LANES, (g * pairs + j + 1) * LANES)
                og_s[r0:r0 + blk, lanes] = (outs[j] * _silu(gate_s[r0:r0 + blk, lanes])).astype(BF16)

    for buf in kv_bufs:
        buf[:, 0:WINDOW, :] = buf[:, tile:tile + WINDOW, :]
    y_ref[...] = _dot(og_s[...], wout_ref[...]) + h


def _swa_prompt(h, kv_norm, wkv, k_norm, bd, cos, sin, b_norm, wq, wg, q_norm, sinks, wout):
    bsz, seq, d = h.shape
    tile = min(PROMPT_TILE, seq)
    assert seq % tile == 0 and tile % WINDOW == 0 and seq >= WINDOW
    const = lambda shape: pl.BlockSpec(shape, lambda b, l: (0,) * len(shape), pipeline_mode=pl.Buffered(1))
    kv_scratch = pltpu.VMEM((B_KV_HEADS, WINDOW + tile, LANES), BF16)
    return pl.pallas_call(
        _swa_prompt_kernel,
        grid=(bsz, seq // tile),
        in_specs=[
            pl.BlockSpec((None, tile, d), lambda b, l: (b, l, 0)),
            const(kv_norm.shape), const(wkv.shape), const(k_norm.shape), const(bd.shape),
            pl.BlockSpec((tile, LANES), lambda b, l: (l, 0)),
            pl.BlockSpec((tile, LANES), lambda b, l: (l, 0)),
            const(b_norm.shape), const(wq.shape), const(wg.shape), const(q_norm.shape),
            pl.BlockSpec(memory_space=pltpu.SMEM),
            const(wout.shape),
        ],
        out_specs=[
            pl.BlockSpec((None, tile, d), lambda b, l: (b, l, 0)),
            pl.BlockSpec((None, WINDOW, B_KV_WIDTH), lambda b, l: (b, 0, 0)),
            pl.BlockSpec((None, WINDOW, B_KV_WIDTH), lambda b, l: (b, 0, 0)),
        ],
        out_shape=[
            jax.ShapeDtypeStruct((bsz, seq, d), F32),
            jax.ShapeDtypeStruct((bsz, WINDOW, B_KV_WIDTH), F32),
            jax.ShapeDtypeStruct((bsz, WINDOW, B_KV_WIDTH), F32),
        ],
        scratch_shapes=[
            kv_scratch, kv_scratch, kv_scratch, kv_scratch,
            pltpu.VMEM((tile // WINDOW, BRANCH_WIDTH // LANES, WINDOW, LANES), BF16),
            pltpu.VMEM((tile, BRANCH_WIDTH), F32),
            pltpu.VMEM((tile, BRANCH_WIDTH), BF16),
        ],
        compiler_params=pltpu.CompilerParams(
            dimension_semantics=("arbitrary", "arbitrary"), vmem_limit_bytes=VMEM_LIMIT_BYTES),
        name="swa_prompt",
    )(h, kv_norm, wkv, k_norm, bd, cos, sin, b_norm, wq, wg, q_norm, sinks, wout)


def _norm_proj_kernel(x_ref, g_ref, w_ref, o_ref):
    o_ref[...] = _dot(_rmsnorm(x_ref[...], g_ref[...]).astype(BF16), w_ref[...])


def _norm_proj(x, g, w, name):
    m, d = x.shape
    n = w.shape[1]
    assert n % PROJ_COL_BLOCK == 0
    return pl.pallas_call(
        _norm_proj_kernel,
        grid=(n // PROJ_COL_BLOCK,),
        in_specs=[
            pl.BlockSpec((m, d), lambda j: (0, 0)),
            pl.BlockSpec((1, d), lambda j: (0, 0)),
            pl.BlockSpec((d, PROJ_COL_BLOCK), lambda j: (0, j)),
        ],
        out_specs=pl.BlockSpec((m, PROJ_COL_BLOCK), lambda j: (0, j)),
        out_shape=jax.ShapeDtypeStruct((m, n), F32),
        compiler_params=pltpu.CompilerParams(dimension_semantics=("arbitrary",)),
        name=name,
    )(x, g, w)


def _gla_out_kernel(o_ref, gate_ref, x_ref, on_ref, w_ref, y_ref, og_s):
    for h in range(A_HEADS):
        vs = slice(h * A_HEAD_DV, (h + 1) * A_HEAD_DV)
        o = _rmsnorm(o_ref[:, vs], on_ref[...])
        og_s[:, vs] = (o * _silu(gate_ref[:, vs])).astype(BF16)
    y_ref[...] = _dot(og_s[...], w_ref[...]) + x_ref[...]


def _gla_out(o, gate, x, out_norm, wout):
    m = o.shape[0]
    return pl.pallas_call(
        _gla_out_kernel,
        out_shape=jax.ShapeDtypeStruct(x.shape, F32),
        scratch_shapes=[pltpu.VMEM((m, BRANCH_WIDTH), BF16)],
        compiler_params=pltpu.CompilerParams(vmem_limit_bytes=VMEM_LIMIT_BYTES),
        name="gla_out_sample",
    )(o, gate, x, out_norm, wout)


def _out_proj_kernel(og_ref, x_ref, w_ref, y_ref):
    y_ref[...] = _dot(og_ref[...].astype(BF16), w_ref[...]) + x_ref[...]


def _out_proj(og, x, wout):
    return pl.pallas_call(
        _out_proj_kernel,
        out_shape=jax.ShapeDtypeStruct(x.shape, F32),
        compiler_params=pltpu.CompilerParams(vmem_limit_bytes=VMEM_LIMIT_BYTES),
        name="swa_out_sample",
    )(og, x, wout)


def _gla_sample_kernel(pj_ref, wg2_ref, bg_ref, st_ref, o_ref, nst_ref):
    nb, seq = pj_ref.shape[0], pj_ref.shape[1]
    dk, dv, kw = A_HEAD_DK, A_HEAD_DV, A_KEY_WIDTH
    scale = A_HEAD_DK ** -0.5
    pad_rows = SUBLANES - seq

    def per_seq(i, carry):
        pj = pj_ref[i]
        q = pj[:, 0:kw] * scale
        k = pj[:, kw:2 * kw]
        v = pj[:, 2 * kw:2 * kw + BRANCH_WIDTH]
        glow = pj[:, 2 * kw + 2 * BRANCH_WIDTH:2 * kw + 2 * BRANCH_WIDTH + LANES]
        glow8 = jnp.concatenate([glow, jnp.zeros((pad_rows, LANES), F32)], axis=0).astype(BF16)
        pre = _dot(glow8, wg2_ref[...])[0:seq] + bg_ref[...]
        loga = _log_sigmoid(pre) * (1.0 / A_GATE_NORMALIZER)
        bs = [loga[0:1]]
        for t in range(1, seq):
            bs.append(bs[-1] + loga[t:t + 1])
        bmat = jnp.concatenate(bs, axis=0)
        blast = bs[-1]
        qt = q * jnp.exp(bmat)
        kd = k * jnp.exp(blast - bmat)
        eblast = jnp.exp(blast)

        outs = [[jnp.zeros((1, dv), F32) for _ in range(A_HEADS)] for _ in range(seq)]
        for t in range(seq):
            for s in range(t + 1):
                w = q[t:t + 1] * k[s:s + 1] * jnp.exp(bs[t] - bs[s])
                for h in range(A_HEADS):
                    a = jnp.sum(w[:, h * dk:(h + 1) * dk], axis=-1, keepdims=True)
                    outs[t][h] = outs[t][h] + a * v[s:s + 1, h * dv:(h + 1) * dv]
        intra = jnp.concatenate([jnp.concatenate(outs[t], axis=1) for t in range(seq)], axis=0)

        qt8 = jnp.concatenate([qt, jnp.zeros((pad_rows, kw), F32)], axis=0).astype(BF16)
        inter = []
        for h in range(A_HEADS):
            ks = slice(h * dk, (h + 1) * dk)
            state = st_ref[i, h]
            inter.append(_dot(qt8[:, ks], state.astype(BF16))[0:seq])
            m = jnp.concatenate([kd[:, ks], eblast[:, ks], jnp.zeros((dk - seq - 1, dk), F32)], axis=0)
            mt = jnp.transpose(m)
            vpad = jnp.concatenate([v[:, h * dv:(h + 1) * dv], jnp.zeros((dk - seq, dv), F32)], axis=0)
            nst_ref[i, h] = state * mt[:, seq:seq + 1] + _dot(mt.astype(BF16), vpad.astype(BF16))
        o_ref[i] = intra + jnp.concatenate(inter, axis=1)
        return carry

    lax.fori_loop(0, nb, per_seq, 0)


def _gla_sample(pj, wg2, bg, state):
    nbatch, seq, n = pj.shape
    nb = min(SAMPLE_BATCH_BLOCK // 2, nbatch)
    assert nbatch % nb == 0
    st_spec = pl.BlockSpec((nb, A_HEADS, A_HEAD_DK, A_HEAD_DV), lambda b: (b, 0, 0, 0))
    return pl.pallas_call(
        _gla_sample_kernel,
        grid=(nbatch // nb,),
        in_specs=[
            pl.BlockSpec((nb, seq, n), lambda b: (b, 0, 0)),
            pl.BlockSpec(wg2.shape, lambda b: (0, 0)),
            pl.BlockSpec(bg.shape, lambda b: (0, 0)),
            st_spec,
        ],
        out_specs=[pl.BlockSpec((nb, seq, BRANCH_WIDTH), lambda b: (b, 0, 0)), st_spec],
        out_shape=[
            jax.ShapeDtypeStruct((nbatch, seq, BRANCH_WIDTH), F32),
            jax.ShapeDtypeStruct(state.shape, F32),
        ],
        compiler_params=pltpu.CompilerParams(
            dimension_semantics=("arbitrary",), vmem_limit_bytes=VMEM_LIMIT_BYTES),
        name="gla_sample",
    )(pj, wg2, bg, state)


def _swa_sample_kernel(pj_ref, ck_ref, cv_ref, kn_ref, bd_ref, cos_ref, sin_ref, qn_ref, sink_ref,
                       og_ref, nk_ref, nv_ref):
    nb, seq = pj_ref.shape[0], pj_ref.shape[1]
    kvw = B_KV_WIDTH
    qrows = SUBLANES
    nkeys = 2 * WINDOW
    cos = cos_ref[...]
    sin = sin_ref[...]
    bd = bd_ref[...]
    first_half, lo64 = _lane_masks(qrows)
    lo64_k = lax.broadcasted_iota(jnp.int32, (nkeys, LANES), 1) < B_HEAD_DIM
    qgain = qn_ref[...] * (B_HEAD_DIM ** -0.5)
    rowi = lax.broadcasted_iota(jnp.int32, (qrows, nkeys), 0)
    coli = lax.broadcasted_iota(jnp.int32, (qrows, nkeys), 1)
    mask = jnp.logical_and(coli - rowi >= 0, coli - rowi <= WINDOW)
    zpad = jnp.zeros((qrows - seq, kvw), F32)

    def per_seq(i, carry):
        pj = pj_ref[i]
        kv8 = jnp.concatenate([pj[:, 0:2 * kvw], jnp.zeros((qrows - seq, 2 * kvw), F32)], axis=0)
        kn = _head_norm(kv8[:, :kvw], bd, kn_ref[...])
        k_new = jnp.concatenate(
            [_rope_slab(kn[:, s * LANES:(s + 1) * LANES], cos, sin, first_half) for s in range(kvw // LANES)], axis=1)
        v_new = kv8[:, kvw:]
        ck = ck_ref[i]
        cv = cv_ref[i]
        nk_ref[i, 0:WINDOW - seq, :] = ck[seq:, :]
        nk_ref[i, WINDOW - seq:WINDOW, :] = k_new[0:seq]
        nv_ref[i, 0:WINDOW - seq, :] = cv[seq:, :]
        nv_ref[i, WINDOW - seq:WINDOW, :] = v_new[0:seq]

        zeros_tail = jnp.zeros((WINDOW - qrows, kvw), F32)
        k_all = jnp.concatenate([ck, k_new[0:seq], zpad, zeros_tail], axis=0)
        v_all = jnp.concatenate([cv, v_new[0:seq], zpad, zeros_tail], axis=0)

        q8 = jnp.concatenate([pj[:, 2 * kvw:2 * kvw + BRANCH_WIDTH],
                              jnp.zeros((qrows - seq, BRANCH_WIDTH), F32)], axis=0)
        gate = pj[:, 2 * kvw + BRANCH_WIDTH:2 * kvw + 2 * BRANCH_WIDTH]
        out_slabs = [None] * (BRANCH_WIDTH // LANES)
        q_slabs = []
        for c in range(BRANCH_WIDTH // kvw):
            qc = _head_norm(q8[:, c * kvw:(c + 1) * kvw], bd, qgain)
            for s in range(kvw // LANES):
                q_slabs.append(_rope_slab(qc[:, s * LANES:(s + 1) * LANES], cos, sin, first_half))
        pairs = B_GROUP // 2
        for g in range(B_KV_HEADS):
            s = g // 2
            ktop, kbot = _split_group(k_all[:, s * LANES:(s + 1) * LANES], g, lo64_k)
            vtop, vbot = _split_group(v_all[:, s * LANES:(s + 1) * LANES], g, lo64_k)
            kpad = jnp.concatenate([ktop, kbot], axis=0).astype(BF16)
            vpad = jnp.concatenate([vtop, vbot], axis=0).astype(BF16)
            qg = jnp.concatenate(q_slabs[g * pairs:(g + 1) * pairs], axis=0).astype(BF16)
            out_slabs[g * pairs:(g + 1) * pairs] = _attend_group(
                qg, kpad, vpad, mask, [sink_ref[g * B_GROUP + r] for r in range(B_GROUP)], lo64, qrows)
        o = jnp.concatenate(out_slabs, axis=1)[0:seq]
        og_ref[i] = o * _silu(gate)
        return carry

    lax.fori_loop(0, nb, per_seq, 0)


def _swa_sample(pj, ck, cv, k_norm, bd, cos, sin, q_norm, sinks):
    nbatch, seq, n = pj.shape
    nb = min(SAMPLE_BATCH_BLOCK, nbatch)
    assert nbatch % nb == 0 and ck.shape[1] == WINDOW
    cache_spec = pl.BlockSpec((nb, WINDOW, B_KV_WIDTH), lambda b: (b, 0, 0))
    full = lambda a: pl.BlockSpec(a.shape, lambda b: (0,) * a.ndim)
    return pl.pallas_call(
        _swa_sample_kernel,
        grid=(nbatch // nb,),
        in_specs=[
            pl.BlockSpec((nb, seq, n), lambda b: (b, 0, 0)),
            cache_spec, cache_spec,
            full(k_norm), full(bd), full(cos), full(sin), full(q_norm),
            pl.BlockSpec(memory_space=pltpu.SMEM),
        ],
        out_specs=[pl.BlockSpec((nb, seq, BRANCH_WIDTH), lambda b: (b, 0, 0)), cache_spec, cache_spec],
        out_shape=[
            jax.ShapeDtypeStruct((nbatch, seq, BRANCH_WIDTH), F32),
            jax.ShapeDtypeStruct(ck.shape, F32),
            jax.ShapeDtypeStruct(cv.shape, F32),
        ],
        compiler_params=pltpu.CompilerParams(
            dimension_semantics=("arbitrary",), vmem_limit_bytes=VMEM_LIMIT_BYTES),
        name="swa_sample",
    )(pj, ck, cv, k_norm, bd, cos, sin, q_norm, sinks)


def _rope_tables(pos, rows):
    half = B_HEAD_DIM // 2
    inv_freq = ROPE_THETA ** (-jnp.arange(half, dtype=F32) / half)
    ang = pos.astype(F32)[:, None] * inv_freq[None, :]
    cos, sin = jnp.cos(ang), jnp.sin(ang)
    cos_t = jnp.concatenate([cos, cos, cos, cos], axis=1)
    sin_t = jnp.concatenate([-sin, sin, -sin, sin], axis=1)
    pad = rows - pos.shape[0]
    if pad:
        cos_t = jnp.pad(cos_t, ((0, pad), (0, 0)))
        sin_t = jnp.pad(sin_t, ((0, pad), (0, 0)))
    return cos_t, sin_t


def kernel(x_prompt, x_sample, state_gla, cache_swa_k, cache_swa_v, a_norm, a_w_in, a_w_gate2, a_b_gate,
           a_out_norm, a_w_out, kv_norm, w_k, w_v, k_norm, b_norm, b_w_in, b_q_norm, b_sinks, b_w_out):
    assert a_norm.shape[0] == 1 and b_norm.shape[0] == 1
    bsz_p, seq_p, d = x_prompt.shape
    bsz_s, seq_s, _ = x_sample.shape
    kw, bw = A_KEY_WIDTH, BRANCH_WIDTH

    w_in = a_w_in[0]
    wa_qk = w_in[:, :2 * kw].astype(BF16)
    wa_v = w_in[:, 2 * kw:2 * kw + bw].astype(BF16)
    wa_g = w_in[:, 2 * kw + bw:2 * kw + 2 * bw].astype(BF16)
    wa_low = jnp.pad(w_in[:, 2 * kw + 2 * bw:], ((0, 0), (0, LANES - A_GATE_RANK))).astype(BF16)
    wa_g2 = jnp.pad(a_w_gate2[0], ((0, LANES - A_GATE_RANK), (0, 0))).astype(BF16)
    a_bg = a_b_gate[0][None, :]
    a_n = a_norm[0][None, :]
    a_on = a_out_norm[0][None, :]
    wa_out = a_w_out[0].astype(BF16)
    w_kv = jnp.concatenate([w_k, w_v], axis=1).astype(BF16)
    wb_q = b_w_in[0][:, :bw].astype(BF16)
    wb_g = b_w_in[0][:, bw:].astype(BF16)
    wb_out = b_w_out[0].astype(BF16)
    kvn = kv_norm[None, :]
    bn = b_norm[0][None, :]
    kn_t = jnp.tile(k_norm, B_KV_WIDTH // B_HEAD_DIM)[None, :]
    qn_t = jnp.tile(b_q_norm[0], B_KV_WIDTH // B_HEAD_DIM)[None, :]
    sinks = b_sinks[0]
    grp = jnp.arange(B_KV_WIDTH) // B_HEAD_DIM
    bd = (grp[:, None] == grp[None, :]).astype(BF16)

    cos_p, sin_p = _rope_tables(jnp.arange(seq_p, dtype=jnp.int32), seq_p)
    h1_p, st_p = _gla_prompt(x_prompt, a_n, wa_qk, wa_v, wa_g, wa_low, wa_g2, a_bg, a_on, wa_out)
    y_p, kc_p, vc_p = _swa_prompt(h1_p, kvn, w_kv, kn_t, bd, cos_p, sin_p, bn, wb_q, wb_g, qn_t, sinks, wb_out)

    m = bsz_s * seq_s
    xs = x_sample.reshape(m, d)
    low_pad = PROJ_COL_BLOCK - A_GATE_RANK
    wa_all = jnp.concatenate(
        [wa_qk, wa_v, wa_g, jnp.pad(w_in[:, 2 * kw + 2 * bw:], ((0, 0), (0, low_pad))).astype(BF16)], axis=1)
    pj_a = _norm_proj(xs, a_n, wa_all, "gla_proj_sample")
    o_s, st_s = _gla_sample(pj_a.reshape(bsz_s, seq_s, -1), wa_g2, a_bg, state_gla[0])
    gate_a = pj_a[:, 2 * kw + bw:2 * kw + 2 * bw]
    h1_s = _gla_out(o_s.reshape(m, bw), gate_a, xs, a_on, wa_out)

    pj_kv = _norm_proj(h1_s, kvn, w_kv, "kv_proj_sample")
    pj_qg = _norm_proj(h1_s, bn, jnp.concatenate([wb_q, wb_g], axis=1), "q_proj_sample")
    pj_b = jnp.concatenate([pj_kv, pj_qg], axis=1).reshape(bsz_s, seq_s, -1)
    pos_s = PAST_LEN + jnp.arange(seq_s, dtype=jnp.int32)
    cos_s, sin_s = _rope_tables(pos_s, SUBLANES)
    og_s, nk_s, nv_s = _swa_sample(
        pj_b, cache_swa_k.reshape(bsz_s, WINDOW, B_KV_WIDTH), cache_swa_v.reshape(bsz_s, WINDOW, B_KV_WIDTH),
        kn_t, bd, cos_s, sin_s, qn_t, sinks)
    y_s = _out_proj(og_s.reshape(m, bw), h1_s, wb_out)

    return (y_p, y_s.reshape(bsz_s, seq_s, d),
            st_p[None], st_s[None],
            kc_p.reshape(bsz_p, WINDOW, B_KV_HEADS, B_HEAD_DIM), vc_p.reshape(bsz_p, WINDOW, B_KV_HEADS, B_HEAD_DIM),
            nk_s.reshape(bsz_s, WINDOW, B_KV_HEADS, B_HEAD_DIM), nv_s.reshape(bsz_s, WINDOW, B_KV_HEADS, B_HEAD_DIM))
```

```python
import functools

import jax
import jax.numpy as jnp
from jax import lax
from jax.experimental import pallas as pl
from jax.experimental.pallas import tpu as pltpu

F32 = jnp.float32
BF16 = jnp.bfloat16

A_HEADS = 4
A_HEAD_DK = 128
A_HEAD_DV = 512
A_KEY_WIDTH = A_HEADS * A_HEAD_DK
BRANCH_WIDTH = A_HEADS * A_HEAD_DV
A_GATE_RANK = 16
A_GATE_NORMALIZER = 16.0
B_HEAD_DIM = 64
B_HEADS = BRANCH_WIDTH // B_HEAD_DIM
B_KV_HEADS = 4
B_GROUP = B_HEADS // B_KV_HEADS
B_KV_WIDTH = B_KV_HEADS * B_HEAD_DIM
WINDOW = 128
ROPE_THETA = 10000.0
RMS_EPS = 1e-6
PAST_LEN = 16384

LANES = 128
SUBLANES = 8
VMEM_LIMIT_BYTES = 56 * 1024 * 1024

PROMPT_TILE = 256
GLA_CHUNK = 128
GLA_SAFE_DECAY = 80.0
SAMPLE_BATCH_BLOCK = 8
PROJ_COL_BLOCK = 512
OUT_PROJ_CHUNKS = 4
LOG2E = 1.4426950408889634


def _dot(a, b):
    return jnp.dot(a, b, preferred_element_type=F32)


def _dot_nt(a, b):
    return lax.dot_general(a, b, (((1,), (1,)), ((), ())), preferred_element_type=F32)


def _split2(x):
    hi = x.astype(BF16)
    lo = (x - hi.astype(F32)).astype(BF16)
    return hi, lo


def _rmsnorm(x, g):
    ms = jnp.mean(x * x, axis=-1, keepdims=True)
    return x * lax.rsqrt(ms + RMS_EPS) * g


def _log_sigmoid(x):
    return jnp.minimum(x, 0.0) - jnp.log1p(jnp.exp(-jnp.abs(x)))


def _silu(x):
    return x * (1.0 / (1.0 + jnp.exp(-x)))


def _head_norm(x, bd, gain):
    hi, lo = _split2(x * x)
    ss = _dot(hi, bd) + _dot(lo, bd)
    return x * lax.rsqrt(ss * (1.0 / B_HEAD_DIM) + RMS_EPS) * gain


def _rope_slab(x, cos, sin_signed, first_half):
    xr = jnp.where(first_half, pltpu.roll(x, 96, 1), pltpu.roll(x, 32, 1))
    return x * cos + xr * sin_signed


def _lane_masks(rows):
    lane = lax.broadcasted_iota(jnp.int32, (rows, LANES), 1)
    first_half = (lane % B_HEAD_DIM) < (B_HEAD_DIM // 2)
    lo64 = lane < B_HEAD_DIM
    return first_half, lo64


def _split_group(slab, g, lo64):
    swapped = pltpu.roll(slab, B_HEAD_DIM, 1)
    zero = jnp.zeros_like(slab)
    if g % 2 == 0:
        return jnp.where(lo64, slab, zero), jnp.where(lo64, zero, swapped)
    return jnp.where(lo64, swapped, zero), jnp.where(lo64, zero, slab)


def _attend_group(qg, kpad, vpad, mask, sinks, lo64, rows):
    s = _dot_nt(qg, kpad)
    nk = kpad.shape[0] // 2
    ps, invs = [], []
    for j in range(len(sinks) // 2):
        pj, ij = [], []
        for half in range(2):
            sink = sinks[2 * j + half]
            sh = jnp.where(mask, s[j * rows:(j + 1) * rows, half * nk:(half + 1) * nk], -jnp.inf)
            m = jnp.maximum(jnp.max(sh, axis=-1, keepdims=True), sink)
            p = jnp.exp(sh - m)
            den = jnp.sum(p, axis=-1, keepdims=True) + jnp.exp(sink - m)
            pj.append(p.astype(BF16))
            ij.append(1.0 / den)
        ps.append(jnp.concatenate(pj, axis=1))
        invs.append(jnp.where(lo64, ij[0], ij[1]))
    o = _dot(jnp.concatenate(ps, axis=0), vpad)
    return [o[j * rows:(j + 1) * rows] * invs[j] for j in range(len(invs))]


def _gla_prompt_kernel(x_ref, an_ref, wqk_ref, wv_ref, wg_ref, wlow_ref, wg2_ref, bg_ref, on_ref, wout_ref,
                       h1_ref, st_ref,
                       qk_s, v_s, gate_s, b_s, o_s, oi_s):
    tile = x_ref.shape[0]
    chunk = GLA_CHUNK
    n_chunks = tile // chunk
    dk, dv, kw = A_HEAD_DK, A_HEAD_DV, A_KEY_WIDTH

    @pl.when(pl.program_id(1) == 0)
    def _():
        st_ref[...] = jnp.zeros_like(st_ref)

    x = x_ref[...]
    u = _rmsnorm(x, an_ref[...]).astype(BF16)
    qk_s[...] = _dot(u, wqk_ref[...])
    v_s[...] = _dot(u, wv_ref[...])
    gate_s[...] = _dot(u, wg_ref[...])
    glow = _dot(u, wlow_ref[...]).astype(BF16)
    pre = _dot(glow, wg2_ref[...]) + bg_ref[...]
    loga = _log_sigmoid(pre) * (1.0 / A_GATE_NORMALIZER)

    row = lax.broadcasted_iota(jnp.int32, (chunk, chunk), 0)
    col = lax.broadcasted_iota(jnp.int32, (chunk, chunk), 1)
    lower = row >= col
    tri = jnp.where(lower, 1.0, 0.0).astype(BF16)
    for c in range(n_chunks):
        hi, lo = _split2(loga[c * chunk:(c + 1) * chunk])
        b_s[c * chunk:(c + 1) * chunk, :] = _dot(tri, hi) + _dot(tri, lo)
    safe = jnp.min(b_s[...]) >= -GLA_SAFE_DECAY

    scale = A_HEAD_DK ** -0.5
    for c in range(n_chunks):
        rows = slice(c * chunk, (c + 1) * chunk)
        b = b_s[rows, :]
        eb = jnp.exp(b)
        enb = jnp.exp(-b)
        blast = b[chunk - 1:chunk, :]
        ekl = jnp.exp(blast - b)
        eblast = jnp.exp(blast)
        for h in range(A_HEADS):
            ks = slice(h * dk, (h + 1) * dk)
            vs = slice(h * dv, (h + 1) * dv)
            q = qk_s[rows, ks] * scale
            k = qk_s[rows, kw + h * dk: kw + (h + 1) * dk]
            vh = v_s[rows, vs].astype(BF16)
            state = st_ref[h]
            qt = (q * eb[:, ks]).astype(BF16)
            o_inter = _dot(qt, state.astype(BF16))
            a = _dot_nt(qt, (k * enb[:, ks]).astype(BF16))
            a = jnp.where(lower, a, 0.0).astype(BF16)
            oi_s[rows, vs] = o_inter
            o_s[rows, vs] = o_inter + _dot(a, vh)
            kd_t = jnp.transpose(k * ekl[:, ks]).astype(BF16)
            dec = jnp.transpose(jnp.broadcast_to(eblast[:, ks], (dk, dk)))
            st_ref[h] = state * jnp.concatenate([dec] * (dv // dk), axis=1) + _dot(kd_t, vh)

    @pl.when(jnp.logical_not(safe))
    def _():
        trow = lax.broadcasted_iota(jnp.int32, (chunk, dk), 0)
        sub8 = lax.broadcasted_iota(jnp.int32, (SUBLANES, 1), 0)
        for c in range(n_chunks):
            rows = slice(c * chunk, (c + 1) * chunk)
            for h in range(A_HEADS):
                ks = slice(h * dk, (h + 1) * dk)
                vs = slice(h * dv, (h + 1) * dv)
                q = qk_s[rows, ks] * scale
                b = b_s[rows, ks]

                def body(s, acc, c=c, h=h, q=q, b=b):
                    r8 = pl.multiple_of(c * chunk + (s // SUBLANES) * SUBLANES, SUBLANES)
                    pick = sub8 == s % SUBLANES

                    def row_of(ref, lanes):
                        return jnp.sum(jnp.where(pick, ref[pl.ds(r8, SUBLANES), lanes], 0.0), axis=0, keepdims=True)

                    brow = row_of(b_s, slice(h * dk, (h + 1) * dk))
                    krow = row_of(qk_s, slice(kw + h * dk, kw + (h + 1) * dk))
                    vrow = row_of(v_s, slice(h * dv, (h + 1) * dv))
                    w = jnp.where(trow >= s, jnp.exp(jnp.minimum(b - brow, 0.0)), 0.0)
                    colv = jnp.sum(q * krow * w, axis=-1, keepdims=True)
                    return acc + colv * vrow

                o_s[rows, vs] = oi_s[rows, vs] + lax.fori_loop(0, chunk, body, jnp.zeros((chunk, dv), F32))

    y = x
    for h in range(A_HEADS):
        vs = slice(h * dv, (h + 1) * dv)
        o = _rmsnorm(o_s[:, vs], on_ref[...])
        og = (o * _silu(gate_s[:, vs])).astype(BF16)
        y = y + _dot(og, wout_ref[vs, :])
    h1_ref[...] = y


def _gla_prompt(x, a_norm, wqk, wv, wg, wlow, wg2, bg, out_norm, wout):
    bsz, seq, d = x.shape
    tile = min(PROMPT_TILE, seq)
    assert seq % tile == 0 and tile % GLA_CHUNK == 0
    const = lambda shape: pl.BlockSpec(shape, lambda b, l: (0,) * len(shape), pipeline_mode=pl.Buffered(1))
    return pl.pallas_call(
        _gla_prompt_kernel,
        grid=(bsz, seq // tile),
        in_specs=[
            pl.BlockSpec((None, tile, d), lambda b, l: (b, l, 0)),
            const(a_norm.shape), const(wqk.shape), const(wv.shape), const(wg.shape), const(wlow.shape),
            const(wg2.shape), const(bg.shape), const(out_norm.shape), const(wout.shape),
        ],
        out_specs=[
            pl.BlockSpec((None, tile, d), lambda b, l: (b, l, 0)),
            pl.BlockSpec((None, A_HEADS, A_HEAD_DK, A_HEAD_DV), lambda b, l: (b, 0, 0, 0)),
        ],
        out_shape=[
            jax.ShapeDtypeStruct((bsz, seq, d), F32),
            jax.ShapeDtypeStruct((bsz, A_HEADS, A_HEAD_DK, A_HEAD_DV), F32),
        ],
        scratch_shapes=[
            pltpu.VMEM((tile, 2 * A_KEY_WIDTH), F32),
            pltpu.VMEM((tile, BRANCH_WIDTH), F32),
            pltpu.VMEM((tile, BRANCH_WIDTH), F32),
            pltpu.VMEM((tile, A_KEY_WIDTH), F32),
            pltpu.VMEM((tile, BRANCH_WIDTH), F32),
            pltpu.VMEM((tile, BRANCH_WIDTH), F32),
        ],
        compiler_params=pltpu.CompilerParams(
            dimension_semantics=("arbitrary", "arbitrary"), vmem_limit_bytes=VMEM_LIMIT_BYTES),
        name="gla_prompt",
    )(x, a_norm, wqk, wv, wg, wlow, wg2, bg, out_norm, wout)


def _swa_prompt_kernel(h_ref, kvn_ref, wkv_ref, kn_ref, bd_ref, cos_ref, sin_ref, bn_ref, wq_ref, wg_ref,
                       qn_ref, sink_ref, wout_ref,
                       y_ref, kc_ref, vc_ref,
                       ktop_s, kbot_s, vtop_s, vbot_s, qb_s, ub_s, kv_s, q_s, gsl_s, osl_s, og_s):
    tile = h_ref.shape[0]
    l = pl.program_id(1)
    blk = WINDOW
    kvw = B_KV_WIDTH
    pairs = B_GROUP // 2
    n_slabs = BRANCH_WIDTH // LANES
    n_iter = (tile // blk) * B_KV_HEADS
    gate_cols = wg_ref.shape[2]

    @pl.when(l == 0)
    def _():
        zeros = jnp.zeros((B_KV_HEADS, WINDOW, LANES), BF16)
        ktop_s[:, 0:WINDOW, :] = zeros
        kbot_s[:, 0:WINDOW, :] = zeros
        vtop_s[:, 0:WINDOW, 0:LANES] = zeros
        vbot_s[:, 0:WINDOW, 0:LANES] = zeros
        lane = lax.broadcasted_iota(jnp.int32, (B_KV_HEADS, WINDOW + tile, LANES), 2)
        vtop_s[:, :, LANES:] = jnp.where(lane < B_HEAD_DIM, 1.0, 0.0).astype(BF16)
        vbot_s[:, :, LANES:] = jnp.where(lane < B_HEAD_DIM, 0.0, 1.0).astype(BF16)

    h = h_ref[...]
    cos = cos_ref[...]
    sin = sin_ref[...]
    first_half, lo64 = _lane_masks(tile)
    bd = bd_ref[...]

    kv_s[...] = _dot(_rmsnorm(h, kvn_ref[...]).astype(BF16), wkv_ref[...])
    ub_s[...] = _rmsnorm(h, bn_ref[...]).astype(BF16)

    kn = _head_norm(kv_s[:, :kvw], bd, kn_ref[...])
    v = kv_s[:, kvw:]
    k_slabs = [_rope_slab(kn[:, s * LANES:(s + 1) * LANES], cos, sin, first_half) for s in range(kvw // LANES)]

    @pl.when(l == pl.num_programs(1) - 1)
    def _():
        kc_ref[...] = jnp.concatenate(k_slabs, axis=1)[tile - WINDOW:, :]
        vc_ref[...] = v[tile - WINDOW:, :]

    for g in range(B_KV_HEADS):
        s = g // 2
        top, bot = _split_group(k_slabs[s], g, lo64)
        ktop_s[g, WINDOW:WINDOW + tile, :] = top.astype(BF16)
        kbot_s[g, WINDOW:WINDOW + tile, :] = bot.astype(BF16)
        top, bot = _split_group(v[:, s * LANES:(s + 1) * LANES], g, lo64)
        vtop_s[g, WINDOW:WINDOW + tile, 0:LANES] = top.astype(BF16)
        vbot_s[g, WINDOW:WINDOW + tile, 0:LANES] = bot.astype(BF16)

    qscale = qn_ref[...] * (B_HEAD_DIM ** -0.5 * LOG2E)
    qcos = [cos * qscale[:, s * LANES:(s + 1) * LANES] for s in range(kvw // LANES)]
    qsin = [sin * jnp.where(first_half[0:1], pltpu.roll(qscale[:, s * LANES:(s + 1) * LANES], 96, 1),
                            pltpu.roll(qscale[:, s * LANES:(s + 1) * LANES], 32, 1)) for s in range(kvw // LANES)]

    q_s[...] = _dot(ub_s[...], wq_ref[...])
    for c in range(BRANCH_WIDTH // kvw):
        x = q_s[:, c * kvw:(c + 1) * kvw]
        hi, lo = _split2(x * x)
        rinv = lax.rsqrt((_dot(hi, bd) + _dot(lo, bd)) * (1.0 / B_HEAD_DIM) + RMS_EPS)
        for s in range(kvw // LANES):
            xs = x[:, s * LANES:(s + 1) * LANES]
            xr = jnp.where(first_half, pltpu.roll(xs, 96, 1), pltpu.roll(xs, 32, 1))
            qr = ((xs * qcos[s] + xr * qsin[s]) * rinv[:, s * LANES:(s + 1) * LANES]).astype(BF16)
            for i in range(tile // blk):
                qb_s[i, c * (kvw // LANES) + s] = qr[i * blk:(i + 1) * blk]

    rowi = lax.broadcasted_iota(jnp.int32, (blk, 2 * blk), 0)
    coli = lax.broadcasted_iota(jnp.int32, (blk, 2 * blk), 1)
    band = jnp.logical_and(coli - rowi >= 0, coli - rowi <= WINDOW)
    _, lo64_b = _lane_masks(blk)

    def attn_body(idx, carry):
        gc = _dot(ub_s[...], wg_ref[idx])
        for s in range(gate_cols // LANES):
            gsl_s[idx * (gate_cols // LANES) + s] = gc[:, s * LANES:(s + 1) * LANES]

        i = idx // B_KV_HEADS
        g = idx % B_KV_HEADS
        r0 = pl.multiple_of(i * blk, blk)
        first_col = jnp.where(l * tile + r0 > 0, 0, WINDOW)
        mask = jnp.logical_and(band, coli >= first_col)
        kpad = jnp.concatenate([ktop_s[g, pl.ds(r0, 2 * blk), :], kbot_s[g, pl.ds(r0, 2 * blk), :]], axis=0)
        vpad = jnp.concatenate([vtop_s[g, pl.ds(r0, 2 * blk), :], vbot_s[g, pl.ds(r0, 2 * blk), :]], axis=0)
        qg = qb_s[i, pl.ds(g * pairs, pairs)].reshape(pairs * blk, LANES)
        s = _dot_nt(qg, kpad)
        ps, esinks = [], []
        for j in range(pairs):
            pj, ej = [], []
            for half in range(2):
                sink = sink_ref[g * B_GROUP + 2 * j + half] * LOG2E
                sh = jnp.where(mask, s[j * blk:(j + 1) * blk, half * 2 * blk:(half + 1) * 2 * blk], -jnp.inf)
                m = jnp.maximum(jnp.max(sh, axis=-1, keepdims=True), sink)
                pj.append(jnp.exp2(sh - m).astype(BF16))
                ej.append(jnp.exp2(sink - m))
            ps.append(jnp.concatenate(pj, axis=1))
            esinks.append(jnp.where(lo64_b, ej[0], ej[1]))
        o = _dot(jnp.concatenate(ps, axis=0), vpad)
        for j in range(pairs):
            oj = o[j * blk:(j + 1) * blk]
            osl_s[g * pairs + j, pl.ds(r0, blk), :] = oj[:, :LANES] / (oj[:, LANES:] + esinks[j])
        return carry

    lax.fori_loop(0, n_iter, attn_body, 0, unroll=4)

    for buf in (ktop_s, kbot_s):
        buf[:, 0:WINDOW, :] = buf[:, tile:tile + WINDOW, :]
    for buf in (vtop_s, vbot_s):
        buf[:, 0:WINDOW, 0:LANES] = buf[:, tile:tile + WINDOW, 0:LANES]

    y = h
    per = n_slabs // OUT_PROJ_CHUNKS
    for c in range(OUT_PROJ_CHUNKS):
        for sl in range(c * per, (c + 1) * per):
            og_s[:, sl * LANES:(sl + 1) * LANES] = (osl_s[sl] * _silu(gsl_s[sl])).astype(BF16)
        cols = slice(c * per * LANES, (c + 1) * per * LANES)
        y = y + _dot(og_s[:, cols], wout_ref[cols, :])
    y_ref[...] = y


def _swa_prompt(h, kv_norm, wkv, k_norm, bd, cos, sin, b_norm, wq, wg, q_norm, sinks, wout):
    bsz, seq, d = h.shape
    tile = min(PROMPT_TILE, seq)
    assert seq % tile == 0 and tile % WINDOW == 0 and seq >= WINDOW
    n_iter = (tile // WINDOW) * B_KV_HEADS
    gate_cols = BRANCH_WIDTH // n_iter
    assert gate_cols % LANES == 0
    wg = wg.reshape(d, n_iter, gate_cols).transpose(1, 0, 2)
    const = lambda shape: pl.BlockSpec(shape, lambda b, l: (0,) * len(shape), pipeline_mode=pl.Buffered(1))
    k_scratch = pltpu.VMEM((B_KV_HEADS, WINDOW + tile, LANES), BF16)
    v_scratch = pltpu.VMEM((B_KV_HEADS, WINDOW + tile, 2 * LANES), BF16)
    n_slabs = BRANCH_WIDTH // LANES
    return pl.pallas_call(
        _swa_prompt_kernel,
        grid=(bsz, seq // tile),
        in_specs=[
            pl.BlockSpec((None, tile, d), lambda b, l: (b, l, 0)),
            const(kv_norm.shape), const(wkv.shape), const(k_norm.shape), const(bd.shape),
            pl.BlockSpec((tile, LANES), lambda b, l: (l, 0)),
            pl.BlockSpec((tile, LANES), lambda b, l: (l, 0)),
            const(b_norm.shape), const(wq.shape), const(wg.shape), const(q_norm.shape),
            pl.BlockSpec(memory_space=pltpu.SMEM),
            const(wout.shape),
        ],
        out_specs=[
            pl.BlockSpec((None, tile, d), lambda b, l: (b, l, 0)),
            pl.BlockSpec((None, WINDOW, B_KV_WIDTH), lambda b, l: (b, 0, 0)),
            pl.BlockSpec((None, WINDOW, B_KV_WIDTH), lambda b, l: (b, 0, 0)),
        ],
        out_shape=[
            jax.ShapeDtypeStruct((bsz, seq, d), F32),
            jax.ShapeDtypeStruct((bsz, WINDOW, B_KV_WIDTH), F32),
            jax.ShapeDtypeStruct((bsz, WINDOW, B_KV_WIDTH), F32),
        ],
        scratch_shapes=[
            k_scratch, k_scratch, v_scratch, v_scratch,
            pltpu.VMEM((tile // WINDOW, n_slabs, WINDOW, LANES), BF16),
            pltpu.VMEM((tile, d), BF16),
            pltpu.VMEM((tile, 2 * B_KV_WIDTH), F32),
            pltpu.VMEM((tile, BRANCH_WIDTH), F32),
            pltpu.VMEM((n_slabs, tile, LANES), F32),
            pltpu.VMEM((n_slabs, tile, LANES), F32),
            pltpu.VMEM((tile, BRANCH_WIDTH), BF16),
        ],
        compiler_params=pltpu.CompilerParams(
            dimension_semantics=("arbitrary", "arbitrary"), vmem_limit_bytes=VMEM_LIMIT_BYTES),
        name="swa_prompt",
    )(h, kv_norm, wkv, k_norm, bd, cos, sin, b_norm, wq, wg, q_norm, sinks, wout)


def _norm_proj_kernel(x_ref, g_ref, w_ref, o_ref):
    o_ref[...] = _dot(_rmsnorm(x_ref[...], g_ref[...]).astype(BF16), w_ref[...])


def _norm_proj(x, g, w, name):
    m, d = x.shape
    n = w.shape[1]
    assert n % PROJ_COL_BLOCK == 0
    return pl.pallas_call(
        _norm_proj_kernel,
        grid=(n // PROJ_COL_BLOCK,),
        in_specs=[
            pl.BlockSpec((m, d), lambda j: (0, 0)),
            pl.BlockSpec((1, d), lambda j: (0, 0)),
            pl.BlockSpec((d, PROJ_COL_BLOCK), lambda j: (0, j)),
        ],
        out_specs=pl.BlockSpec((m, PROJ_COL_BLOCK), lambda j: (0, j)),
        out_shape=jax.ShapeDtypeStruct((m, n), F32),
        compiler_params=pltpu.CompilerParams(dimension_semantics=("arbitrary",)),
        name=name,
    )(x, g, w)


def _gla_out_kernel(o_ref, gate_ref, x_ref, on_ref, w_ref, y_ref, og_s):
    for h in range(A_HEADS):
        vs = slice(h * A_HEAD_DV, (h + 1) * A_HEAD_DV)
        o = _rmsnorm(o_ref[:, vs], on_ref[...])
        og_s[:, vs] = (o * _silu(gate_ref[:, vs])).astype(BF16)
    y_ref[...] = _dot(og_s[...], w_ref[...]) + x_ref[...]


def _gla_out(o, gate, x, out_norm, wout):
    m = o.shape[0]
    return pl.pallas_call(
        _gla_out_kernel,
        out_shape=jax.ShapeDtypeStruct(x.shape, F32),
        scratch_shapes=[pltpu.VMEM((m, BRANCH_WIDTH), BF16)],
        compiler_params=pltpu.CompilerParams(vmem_limit_bytes=VMEM_LIMIT_BYTES),
        name="gla_out_sample",
    )(o, gate, x, out_norm, wout)


def _out_proj_kernel(og_ref, x_ref, w_ref, y_ref):
    y_ref[...] = _dot(og_ref[...].astype(BF16), w_ref[...]) + x_ref[...]


def _out_proj(og, x, wout):
    return pl.pallas_call(
        _out_proj_kernel,
        out_shape=jax.ShapeDtypeStruct(x.shape, F32),
        compiler_params=pltpu.CompilerParams(vmem_limit_bytes=VMEM_LIMIT_BYTES),
        name="swa_out_sample",
    )(og, x, wout)


def _gla_sample_kernel(pj_ref, wg2_ref, bg_ref, st_ref, o_ref, nst_ref):
    nb, seq = pj_ref.shape[0], pj_ref.shape[1]
    dk, dv, kw = A_HEAD_DK, A_HEAD_DV, A_KEY_WIDTH
    scale = A_HEAD_DK ** -0.5
    pad_rows = SUBLANES - seq

    def per_seq(i, carry):
        pj = pj_ref[i]
        q = pj[:, 0:kw] * scale
        k = pj[:, kw:2 * kw]
        v = pj[:, 2 * kw:2 * kw + BRANCH_WIDTH]
        glow = pj[:, 2 * kw + 2 * BRANCH_WIDTH:2 * kw + 2 * BRANCH_WIDTH + LANES]
        glow8 = jnp.concatenate([glow, jnp.zeros((pad_rows, LANES), F32)], axis=0).astype(BF16)
        pre = _dot(glow8, wg2_ref[...])[0:seq] + bg_ref[...]
        loga = _log_sigmoid(pre) * (1.0 / A_GATE_NORMALIZER)
        bs = [loga[0:1]]
        for t in range(1, seq):
            bs.append(bs[-1] + loga[t:t + 1])
        bmat = jnp.concatenate(bs, axis=0)
        blast = bs[-1]
        qt = q * jnp.exp(bmat)
        kd = k * jnp.exp(blast - bmat)
        eblast = jnp.exp(blast)

        outs = [[jnp.zeros((1, dv), F32) for _ in range(A_HEADS)] for _ in range(seq)]
        for t in range(seq):
            for s in range(t + 1):
                w = q[t:t + 1] * k[s:s + 1] * jnp.exp(bs[t] - bs[s])
                for h in range(A_HEADS):
                    a = jnp.sum(w[:, h * dk:(h + 1) * dk], axis=-1, keepdims=True)
                    outs[t][h] = outs[t][h] + a * v[s:s + 1, h * dv:(h + 1) * dv]
        intra = jnp.concatenate([jnp.concatenate(outs[t], axis=1) for t in range(seq)], axis=0)

        qt8 = jnp.concatenate([qt, jnp.zeros((pad_rows, kw), F32)], axis=0).astype(BF16)
        inter = []
        for h in range(A_HEADS):
            ks = slice(h * dk, (h + 1) * dk)
            state = st_ref[i, h]
            inter.append(_dot(qt8[:, ks], state.astype(BF16))[0:seq])
            m = jnp.concatenate([kd[:, ks], eblast[:, ks], jnp.zeros((dk - seq - 1, dk), F32)], axis=0)
            mt = jnp.transpose(m)
            vpad = jnp.concatenate([v[:, h * dv:(h + 1) * dv], jnp.zeros((dk - seq, dv), F32)], axis=0)
            nst_ref[i, h] = state * mt[:, seq:seq + 1] + _dot(mt.astype(BF16), vpad.astype(BF16))
        o_ref[i] = intra + jnp.concatenate(inter, axis=1)
        return carry

    lax.fori_loop(0, nb, per_seq, 0)


def _gla_sample(pj, wg2, bg, state):
    nbatch, seq, n = pj.shape
    nb = min(SAMPLE_BATCH_BLOCK // 2, nbatch)
    assert nbatch % nb == 0
    st_spec = pl.BlockSpec((nb, A_HEADS, A_HEAD_DK, A_HEAD_DV), lambda b: (b, 0, 0, 0))
    return pl.pallas_call(
        _gla_sample_kernel,
        grid=(nbatch // nb,),
        in_specs=[
            pl.BlockSpec((nb, seq, n), lambda b: (b, 0, 0)),
            pl.BlockSpec(wg2.shape, lambda b: (0, 0)),
            pl.BlockSpec(bg.shape, lambda b: (0, 0)),
            st_spec,
        ],
        out_specs=[pl.BlockSpec((nb, seq, BRANCH_WIDTH), lambda b: (b, 0, 0)), st_spec],
        out_shape=[
            jax.ShapeDtypeStruct((nbatch, seq, BRANCH_WIDTH), F32),
            jax.ShapeDtypeStruct(state.shape, F32),
        ],
        compiler_params=pltpu.CompilerParams(
            dimension_semantics=("arbitrary",), vmem_limit_bytes=VMEM_LIMIT_BYTES),
        name="gla_sample",
    )(pj, wg2, bg, state)


def _swa_sample_kernel(pj_ref, ck_ref, cv_ref, kn_ref, bd_ref, cos_ref, sin_ref, qn_ref, sink_ref,
                       og_ref, nk_ref, nv_ref):
    nb, seq = pj_ref.shape[0], pj_ref.shape[1]
    kvw = B_KV_WIDTH
    qrows = SUBLANES
    nkeys = 2 * WINDOW
    cos = cos_ref[...]
    sin = sin_ref[...]
    bd = bd_ref[...]
    first_half, lo64 = _lane_masks(qrows)
    lo64_k = lax.broadcasted_iota(jnp.int32, (nkeys, LANES), 1) < B_HEAD_DIM
    qgain = qn_ref[...] * (B_HEAD_DIM ** -0.5)
    rowi = lax.broadcasted_iota(jnp.int32, (qrows, nkeys), 0)
    coli = lax.broadcasted_iota(jnp.int32, (qrows, nkeys), 1)
    mask = jnp.logical_and(coli - rowi >= 0, coli - rowi <= WINDOW)
    zpad = jnp.zeros((qrows - seq, kvw), F32)

    def per_seq(i, carry):
        pj = pj_ref[i]
        kv8 = jnp.concatenate([pj[:, 0:2 * kvw], jnp.zeros((qrows - seq, 2 * kvw), F32)], axis=0)
        kn = _head_norm(kv8[:, :kvw], bd, kn_ref[...])
        k_new = jnp.concatenate(
            [_rope_slab(kn[:, s * LANES:(s + 1) * LANES], cos, sin, first_half) for s in range(kvw // LANES)], axis=1)
        v_new = kv8[:, kvw:]
        ck = ck_ref[i]
        cv = cv_ref[i]
        nk_ref[i, 0:WINDOW - seq, :] = ck[seq:, :]
        nk_ref[i, WINDOW - seq:WINDOW, :] = k_new[0:seq]
        nv_ref[i, 0:WINDOW - seq, :] = cv[seq:, :]
        nv_ref[i, WINDOW - seq:WINDOW, :] = v_new[0:seq]

        zeros_tail = jnp.zeros((WINDOW - qrows, kvw), F32)
        k_all = jnp.concatenate([ck, k_new[0:seq], zpad, zeros_tail], axis=0)
        v_all = jnp.concatenate([cv, v_new[0:seq], zpad, zeros_tail], axis=0)

        q8 = jnp.concatenate([pj[:, 2 * kvw:2 * kvw + BRANCH_WIDTH],
                              jnp.zeros((qrows - seq, BRANCH_WIDTH), F32)], axis=0)
        gate = pj[:, 2 * kvw + BRANCH_WIDTH:2 * kvw + 2 * BRANCH_WIDTH]
        out_slabs = [None] * (BRANCH_WIDTH // LANES)
        q_slabs = []
        for c in range(BRANCH_WIDTH // kvw):
            qc = _head_norm(q8[:, c * kvw:(c + 1) * kvw], bd, qgain)
            for s in range(kvw // LANES):
                q_slabs.append(_rope_slab(qc[:, s * LANES:(s + 1) * LANES], cos, sin, first_half))
        pairs = B_GROUP // 2
        for g in range(B_KV_HEADS):
            s = g // 2
            ktop, kbot = _split_group(k_all[:, s * LANES:(s + 1) * LANES], g, lo64_k)
            vtop, vbot = _split_group(v_all[:, s * LANES:(s + 1) * LANES], g, lo64_k)
            kpad = jnp.concatenate([ktop, kbot], axis=0).astype(BF16)
            vpad = jnp.concatenate([vtop, vbot], axis=0).astype(BF16)
            qg = jnp.concatenate(q_slabs[g * pairs:(g + 1) * pairs], axis=0).astype(BF16)
            out_slabs[g * pairs:(g + 1) * pairs] = _attend_group(
                qg, kpad, vpad, mask, [sink_ref[g * B_GROUP + r] for r in range(B_GROUP)], lo64, qrows)
        o = jnp.concatenate(out_slabs, axis=1)[0:seq]
        og_ref[i] = o * _silu(gate)
        return carry

    lax.fori_loop(0, nb, per_seq, 0)


def _swa_sample(pj, ck, cv, k_norm, bd, cos, sin, q_norm, sinks):
    nbatch, seq, n = pj.shape
    nb = min(SAMPLE_BATCH_BLOCK, nbatch)
    assert nbatch % nb == 0 and ck.shape[1] == WINDOW
    cache_spec = pl.BlockSpec((nb, WINDOW, B_KV_WIDTH), lambda b: (b, 0, 0))
    full = lambda a: pl.BlockSpec(a.shape, lambda b: (0,) * a.ndim)
    return pl.pallas_call(
        _swa_sample_kernel,
        grid=(nbatch // nb,),
        in_specs=[
            pl.BlockSpec((nb, seq, n), lambda b: (b, 0, 0)),
            cache_spec, cache_spec,
            full(k_norm), full(bd), full(cos), full(sin), full(q_norm),
            pl.BlockSpec(memory_space=pltpu.SMEM),
        ],
        out_specs=[pl.BlockSpec((nb, seq, BRANCH_WIDTH), lambda b: (b, 0, 0)), cache_spec, cache_spec],
        out_shape=[
            jax.ShapeDtypeStruct((nbatch, seq, BRANCH_WIDTH), F32),
            jax.ShapeDtypeStruct(ck.shape, F32),
            jax.ShapeDtypeStruct(cv.shape, F32),
        ],
        compiler_params=pltpu.CompilerParams(
            dimension_semantics=("arbitrary",), vmem_limit_bytes=VMEM_LIMIT_BYTES),
        name="swa_sample",
    )(pj, ck, cv, k_norm, bd, cos, sin, q_norm, sinks)


def _rope_tables(pos, rows):
    half = B_HEAD_DIM // 2
    inv_freq = ROPE_THETA ** (-jnp.arange(half, dtype=F32) / half)
    ang = pos.astype(F32)[:, None] * inv_freq[None, :]
    cos, sin = jnp.cos(ang), jnp.sin(ang)
    cos_t = jnp.concatenate([cos, cos, cos, cos], axis=1)
    sin_t = jnp.concatenate([-sin, sin, -sin, sin], axis=1)
    pad = rows - pos.shape[0]
    if pad:
        cos_t = jnp.pad(cos_t, ((0, pad), (0, 0)))
        sin_t = jnp.pad(sin_t, ((0, pad), (0, 0)))
    return cos_t, sin_t


def kernel(x_prompt, x_sample, state_gla, cache_swa_k, cache_swa_v, a_norm, a_w_in, a_w_gate2, a_b_gate,
           a_out_norm, a_w_out, kv_norm, w_k, w_v, k_norm, b_norm, b_w_in, b_q_norm, b_sinks, b_w_out):
    assert a_norm.shape[0] == 1 and b_norm.shape[0] == 1
    bsz_p, seq_p, d = x_prompt.shape
    bsz_s, seq_s, _ = x_sample.shape
    kw, bw = A_KEY_WIDTH, BRANCH_WIDTH

    w_in = a_w_in[0]
    wa_qk = w_in[:, :2 * kw].astype(BF16)
    wa_v = w_in[:, 2 * kw:2 * kw + bw].astype(BF16)
    wa_g = w_in[:, 2 * kw + bw:2 * kw + 2 * bw].astype(BF16)
    wa_low = jnp.pad(w_in[:, 2 * kw + 2 * bw:], ((0, 0), (0, LANES - A_GATE_RANK))).astype(BF16)
    wa_g2 = jnp.pad(a_w_gate2[0], ((0, LANES - A_GATE_RANK), (0, 0))).astype(BF16)
    a_bg = a_b_gate[0][None, :]
    a_n = a_norm[0][None, :]
    a_on = a_out_norm[0][None, :]
    wa_out = a_w_out[0].astype(BF16)
    w_kv = jnp.concatenate([w_k, w_v], axis=1).astype(BF16)
    wb_q = b_w_in[0][:, :bw].astype(BF16)
    wb_g = b_w_in[0][:, bw:].astype(BF16)
    wb_out = b_w_out[0].astype(BF16)
    kvn = kv_norm[None, :]
    bn = b_norm[0][None, :]
    kn_t = jnp.tile(k_norm, B_KV_WIDTH // B_HEAD_DIM)[None, :]
    qn_t = jnp.tile(b_q_norm[0], B_KV_WIDTH // B_HEAD_DIM)[None, :]
    sinks = b_sinks[0]
    grp = jnp.arange(B_KV_WIDTH) // B_HEAD_DIM
    bd = (grp[:, None] == grp[None, :]).astype(BF16)

    cos_p, sin_p = _rope_tables(jnp.arange(seq_p, dtype=jnp.int32), seq_p)
    h1_p, st_p = _gla_prompt(x_prompt, a_n, wa_qk, wa_v, wa_g, wa_low, wa_g2, a_bg, a_on, wa_out)
    y_p, kc_p, vc_p = _swa_prompt(h1_p, kvn, w_kv, kn_t, bd, cos_p, sin_p, bn, wb_q, wb_g, qn_t, sinks, wb_out)

    m = bsz_s * seq_s
    xs = x_sample.reshape(m, d)
    low_pad = PROJ_COL_BLOCK - A_GATE_RANK
    wa_all = jnp.concatenate(
        [wa_qk, wa_v, wa_g, jnp.pad(w_in[:, 2 * kw + 2 * bw:], ((0, 0), (0, low_pad))).astype(BF16)], axis=1)
    pj_a = _norm_proj(xs, a_n, wa_all, "gla_proj_sample")
    o_s, st_s = _gla_sample(pj_a.reshape(bsz_s, seq_s, -1), wa_g2, a_bg, state_gla[0])
    gate_a = pj_a[:, 2 * kw + bw:2 * kw + 2 * bw]
    h1_s = _gla_out(o_s.reshape(m, bw), gate_a, xs, a_on, wa_out)

    pj_kv = _norm_proj(h1_s, kvn, w_kv, "kv_proj_sample")
    pj_qg = _norm_proj(h1_s, bn, jnp.concatenate([wb_q, wb_g], axis=1), "q_proj_sample")
    pj_b = jnp.concatenate([pj_kv, pj_qg], axis=1).reshape(bsz_s, seq_s, -1)
    pos_s = PAST_LEN + jnp.arange(seq_s, dtype=jnp.int32)
    cos_s, sin_s = _rope_tables(pos_s, SUBLANES)
    og_s, nk_s, nv_s = _swa_sample(
        pj_b, cache_swa_k.reshape(bsz_s, WINDOW, B_KV_WIDTH), cache_swa_v.reshape(bsz_s, WINDOW, B_KV_WIDTH),
        kn_t, bd, cos_s, sin_s, qn_t, sinks)
    y_s = _out_proj(og_s.reshape(m, bw), h1_s, wb_out)

    return (y_p, y_s.reshape(bsz_s, seq_s, d),
            st_p[None], st_s[None],
            kc_p.reshape(bsz_p, WINDOW, B_KV_HEADS, B_HEAD_DIM), vc_p.reshape(bsz_p, WINDOW, B_KV_HEADS, B_HEAD_DIM),
            nk_s.reshape(bsz_s, WINDOW, B_KV_HEADS, B_HEAD_DIM), nv_s.reshape(bsz_s, WINDOW, B_KV_HEADS, B_HEAD_DIM))
```

```python
import jax
import jax.numpy as jnp
from jax import lax
from jax.experimental import pallas as pl
from jax.experimental.pallas import tpu as pltpu

F32 = jnp.float32
BF16 = jnp.bfloat16

A_HEADS = 4
A_HEAD_DK = 128
A_HEAD_DV = 512
A_KEY_WIDTH = A_HEADS * A_HEAD_DK
BRANCH_WIDTH = A_HEADS * A_HEAD_DV
A_GATE_RANK = 16
A_GATE_NORMALIZER = 16.0
B_HEAD_DIM = 64
B_HEADS = BRANCH_WIDTH // B_HEAD_DIM
B_KV_HEADS = 4
B_GROUP = B_HEADS // B_KV_HEADS
B_KV_WIDTH = B_KV_HEADS * B_HEAD_DIM
WINDOW = 128
ROPE_THETA = 10000.0
RMS_EPS = 1e-6
PAST_LEN = 16384

LANES = 128
SUBLANES = 8
VMEM_LIMIT_BYTES = 56 * 1024 * 1024

PROMPT_TILE = 256
GLA_CHUNK = 128
GLA_SAFE_DECAY = 80.0
SAMPLE_BATCH_BLOCK = 8
PROJ_COL_BLOCK = 512
OUT_PROJ_CHUNKS = 4
LOG2E = 1.4426950408889634


def _dot(a, b):
    return jnp.dot(a, b, preferred_element_type=F32)


def _dot_nt(a, b):
    return lax.dot_general(a, b, (((1,), (1,)), ((), ())), preferred_element_type=F32)


def _split2(x):
    hi = x.astype(BF16)
    lo = (x - hi.astype(F32)).astype(BF16)
    return hi, lo


def _rmsnorm(x, g):
    ms = jnp.mean(x * x, axis=-1, keepdims=True)
    return x * lax.rsqrt(ms + RMS_EPS) * g


def _log_sigmoid(x):
    return jnp.minimum(x, 0.0) - jnp.log1p(jnp.exp(-jnp.abs(x)))


def _silu(x):
    return x * (1.0 / (1.0 + jnp.exp(-x)))


def _head_norm(x, bd, gain):
    hi, lo = _split2(x * x)
    ss = _dot(hi, bd) + _dot(lo, bd)
    return x * lax.rsqrt(ss * (1.0 / B_HEAD_DIM) + RMS_EPS) * gain


def _rope_slab(x, cos, sin_signed, first_half):
    xr = jnp.where(first_half, pltpu.roll(x, 96, 1), pltpu.roll(x, 32, 1))
    return x * cos + xr * sin_signed


def _lane_masks(rows):
    lane = lax.broadcasted_iota(jnp.int32, (rows, LANES), 1)
    first_half = (lane % B_HEAD_DIM) < (B_HEAD_DIM // 2)
    lo64 = lane < B_HEAD_DIM
    return first_half, lo64


def _split_group(slab, g, lo64):
    swapped = pltpu.roll(slab, B_HEAD_DIM, 1)
    zero = jnp.zeros_like(slab)
    if g % 2 == 0:
        return jnp.where(lo64, slab, zero), jnp.where(lo64, zero, swapped)
    return jnp.where(lo64, swapped, zero), jnp.where(lo64, zero, slab)


def _gla_prompt_kernel(x_ref, an_ref, wqk_ref, wv_ref, wg_ref, wlow_ref, wg2_ref, bg_ref, on_ref, wout_ref,
                       h1_ref, st_ref,
                       qk_s, v_s, gate_s, b_s, o_s, oi_s):
    tile = x_ref.shape[0]
    chunk = GLA_CHUNK
    n_chunks = tile // chunk
    dk, dv, kw = A_HEAD_DK, A_HEAD_DV, A_KEY_WIDTH

    @pl.when(pl.program_id(1) == 0)
    def _():
        st_ref[...] = jnp.zeros_like(st_ref)

    x = x_ref[...]
    u = _rmsnorm(x, an_ref[...]).astype(BF16)
    qk_s[...] = _dot(u, wqk_ref[...])
    v_s[...] = _dot(u, wv_ref[...])
    gate_s[...] = _dot(u, wg_ref[...])
    glow = _dot(u, wlow_ref[...]).astype(BF16)
    pre = _dot(glow, wg2_ref[...]) + bg_ref[...]
    loga = _log_sigmoid(pre) * (1.0 / A_GATE_NORMALIZER)

    row = lax.broadcasted_iota(jnp.int32, (chunk, chunk), 0)
    col = lax.broadcasted_iota(jnp.int32, (chunk, chunk), 1)
    lower = row >= col
    tri = jnp.where(lower, 1.0, 0.0).astype(BF16)
    for c in range(n_chunks):
        hi, lo = _split2(loga[c * chunk:(c + 1) * chunk])
        b_s[c * chunk:(c + 1) * chunk, :] = _dot(tri, hi) + _dot(tri, lo)
    safe = jnp.min(b_s[...]) >= -GLA_SAFE_DECAY

    scale = A_HEAD_DK ** -0.5
    for c in range(n_chunks):
        rows = slice(c * chunk, (c + 1) * chunk)
        b = b_s[rows, :]
        eb = jnp.exp(b)
        enb = jnp.exp(-b)
        blast = b[chunk - 1:chunk, :]
        ekl = jnp.exp(blast - b)
        eblast = jnp.exp(blast)
        for h in range(A_HEADS):
            ks = slice(h * dk, (h + 1) * dk)
            vs = slice(h * dv, (h + 1) * dv)
            q = qk_s[rows, ks] * scale
            k = qk_s[rows, kw + h * dk: kw + (h + 1) * dk]
            vh = v_s[rows, vs].astype(BF16)
            state = st_ref[h]
            qt = (q * eb[:, ks]).astype(BF16)
            o_inter = _dot(qt, state.astype(BF16))
            a = _dot_nt(qt, (k * enb[:, ks]).astype(BF16))
            a = jnp.where(lower, a, 0.0).astype(BF16)
            oi_s[rows, vs] = o_inter
            o_s[rows, vs] = o_inter + _dot(a, vh)
            kd_t = jnp.transpose(k * ekl[:, ks]).astype(BF16)
            dec = jnp.transpose(jnp.broadcast_to(eblast[:, ks], (dk, dk)))
            st_ref[h] = state * jnp.concatenate([dec] * (dv // dk), axis=1) + _dot(kd_t, vh)

    @pl.when(jnp.logical_not(safe))
    def _():
        trow = lax.broadcasted_iota(jnp.int32, (chunk, dk), 0)
        sub8 = lax.broadcasted_iota(jnp.int32, (SUBLANES, 1), 0)
        for c in range(n_chunks):
            rows = slice(c * chunk, (c + 1) * chunk)
            for h in range(A_HEADS):
                ks = slice(h * dk, (h + 1) * dk)
                vs = slice(h * dv, (h + 1) * dv)
                q = qk_s[rows, ks] * scale
                b = b_s[rows, ks]

                def body(s, acc, c=c, h=h, q=q, b=b):
                    r8 = pl.multiple_of(c * chunk + (s // SUBLANES) * SUBLANES, SUBLANES)
                    pick = sub8 == s % SUBLANES

                    def row_of(ref, lanes):
                        return jnp.sum(jnp.where(pick, ref[pl.ds(r8, SUBLANES), lanes], 0.0), axis=0, keepdims=True)

                    brow = row_of(b_s, slice(h * dk, (h + 1) * dk))
                    krow = row_of(qk_s, slice(kw + h * dk, kw + (h + 1) * dk))
                    vrow = row_of(v_s, slice(h * dv, (h + 1) * dv))
                    w = jnp.where(trow >= s, jnp.exp(jnp.minimum(b - brow, 0.0)), 0.0)
                    colv = jnp.sum(q * krow * w, axis=-1, keepdims=True)
                    return acc + colv * vrow

                o_s[rows, vs] = oi_s[rows, vs] + lax.fori_loop(0, chunk, body, jnp.zeros((chunk, dv), F32))

    y = x
    for h in range(A_HEADS):
        vs = slice(h * dv, (h + 1) * dv)
        o = _rmsnorm(o_s[:, vs], on_ref[...])
        og = (o * _silu(gate_s[:, vs])).astype(BF16)
        y = y + _dot(og, wout_ref[vs, :])
    h1_ref[...] = y


def _gla_prompt(x, a_norm, wqk, wv, wg, wlow, wg2, bg, out_norm, wout):
    bsz, seq, d = x.shape
    tile = min(PROMPT_TILE, seq)
    assert seq % tile == 0 and tile % GLA_CHUNK == 0
    const = lambda shape: pl.BlockSpec(shape, lambda b, l: (0,) * len(shape), pipeline_mode=pl.Buffered(1))
    return pl.pallas_call(
        _gla_prompt_kernel,
        grid=(bsz, seq // tile),
        in_specs=[
            pl.BlockSpec((None, tile, d), lambda b, l: (b, l, 0)),
            const(a_norm.shape), const(wqk.shape), const(wv.shape), const(wg.shape), const(wlow.shape),
            const(wg2.shape), const(bg.shape), const(out_norm.shape), const(wout.shape),
        ],
        out_specs=[
            pl.BlockSpec((None, tile, d), lambda b, l: (b, l, 0)),
            pl.BlockSpec((None, A_HEADS, A_HEAD_DK, A_HEAD_DV), lambda b, l: (b, 0, 0, 0)),
        ],
        out_shape=[
            jax.ShapeDtypeStruct((bsz, seq, d), F32),
            jax.ShapeDtypeStruct((bsz, A_HEADS, A_HEAD_DK, A_HEAD_DV), F32),
        ],
        scratch_shapes=[
            pltpu.VMEM((tile, 2 * A_KEY_WIDTH), F32),
            pltpu.VMEM((tile, BRANCH_WIDTH), F32),
            pltpu.VMEM((tile, BRANCH_WIDTH), F32),
            pltpu.VMEM((tile, A_KEY_WIDTH), F32),
            pltpu.VMEM((tile, BRANCH_WIDTH), F32),
            pltpu.VMEM((tile, BRANCH_WIDTH), F32),
        ],
        compiler_params=pltpu.CompilerParams(
            dimension_semantics=("arbitrary", "arbitrary"), vmem_limit_bytes=VMEM_LIMIT_BYTES),
        name="gla_prompt",
    )(x, a_norm, wqk, wv, wg, wlow, wg2, bg, out_norm, wout)


def _swa_prompt_kernel(h_ref, kvn_ref, wkv_ref, kn_ref, bd_ref, cos_ref, sin_ref, bn_ref, wq_ref, wg_ref,
                       qn_ref, sink_ref, wout_ref,
                       y_ref, kc_ref, vc_ref,
                       ktop_s, kbot_s, vtop_s, vbot_s, qb_s, ub_s, kv_s, q_s, gsl_s, osl_s, og_s):
    tile = h_ref.shape[0]
    l = pl.program_id(1)
    blk = WINDOW
    kvw = B_KV_WIDTH
    pairs = B_GROUP // 2
    n_slabs = BRANCH_WIDTH // LANES
    n_iter = (tile // blk) * B_KV_HEADS
    gate_cols = wg_ref.shape[2]

    @pl.when(l == 0)
    def _():
        zeros = jnp.zeros((B_KV_HEADS, WINDOW, LANES), BF16)
        ktop_s[:, 0:WINDOW, :] = zeros
        kbot_s[:, 0:WINDOW, :] = zeros
        vtop_s[:, 0:WINDOW, 0:LANES] = zeros
        vbot_s[:, 0:WINDOW, 0:LANES] = zeros
        lane = lax.broadcasted_iota(jnp.int32, (B_KV_HEADS, WINDOW + tile, LANES), 2)
        vtop_s[:, :, LANES:] = jnp.where(lane < B_HEAD_DIM, 1.0, 0.0).astype(BF16)
        vbot_s[:, :, LANES:] = jnp.where(lane < B_HEAD_DIM, 0.0, 1.0).astype(BF16)

    h = h_ref[...]
    cos = cos_ref[...]
    sin = sin_ref[...]
    first_half, lo64 = _lane_masks(tile)
    bd = bd_ref[...]

    kv_s[...] = _dot(_rmsnorm(h, kvn_ref[...]).astype(BF16), wkv_ref[...])
    ub_s[...] = _rmsnorm(h, bn_ref[...]).astype(BF16)

    kn = _head_norm(kv_s[:, :kvw], bd, kn_ref[...])
    v = kv_s[:, kvw:]
    k_slabs = [_rope_slab(kn[:, s * LANES:(s + 1) * LANES], cos, sin, first_half) for s in range(kvw // LANES)]

    @pl.when(l == pl.num_programs(1) - 1)
    def _():
        kc_ref[...] = jnp.concatenate(k_slabs, axis=1)[tile - WINDOW:, :]
        vc_ref[...] = v[tile - WINDOW:, :]

    for g in range(B_KV_HEADS):
        s = g // 2
        top, bot = _split_group(k_slabs[s], g, lo64)
        ktop_s[g, WINDOW:WINDOW + tile, :] = top.astype(BF16)
        kbot_s[g, WINDOW:WINDOW + tile, :] = bot.astype(BF16)
        top, bot = _split_group(v[:, s * LANES:(s + 1) * LANES], g, lo64)
        vtop_s[g, WINDOW:WINDOW + tile, 0:LANES] = top.astype(BF16)
        vbot_s[g, WINDOW:WINDOW + tile, 0:LANES] = bot.astype(BF16)

    qscale = qn_ref[...] * (B_HEAD_DIM ** -0.5 * LOG2E)
    qcos = [cos * qscale[:, s * LANES:(s + 1) * LANES] for s in range(kvw // LANES)]
    qsin = [sin * jnp.where(first_half[0:1], pltpu.roll(qscale[:, s * LANES:(s + 1) * LANES], 96, 1),
                            pltpu.roll(qscale[:, s * LANES:(s + 1) * LANES], 32, 1)) for s in range(kvw // LANES)]

    q_s[...] = _dot(ub_s[...], wq_ref[...])
    for c in range(BRANCH_WIDTH // kvw):
        x = q_s[:, c * kvw:(c + 1) * kvw]
        hi, lo = _split2(x * x)
        rinv = lax.rsqrt((_dot(hi, bd) + _dot(lo, bd)) * (1.0 / B_HEAD_DIM) + RMS_EPS)
        for s in range(kvw // LANES):
            xs = x[:, s * LANES:(s + 1) * LANES]
            xr = jnp.where(first_half, pltpu.roll(xs, 96, 1), pltpu.roll(xs, 32, 1))
            qr = ((xs * qcos[s] + xr * qsin[s]) * rinv[:, s * LANES:(s + 1) * LANES]).astype(BF16)
            for i in range(tile // blk):
                qb_s[i, c * (kvw // LANES) + s] = qr[i * blk:(i + 1) * blk]

    rowi = lax.broadcasted_iota(jnp.int32, (blk, 2 * blk), 0)
    coli = lax.broadcasted_iota(jnp.int32, (blk, 2 * blk), 1)
    band = jnp.logical_and(coli - rowi >= 0, coli - rowi <= WINDOW)
    _, lo64_b = _lane_masks(blk)

    def attn_body(idx, carry):
        gc = _dot(ub_s[...], wg_ref[idx])
        for s in range(gate_cols // LANES):
            gsl_s[idx * (gate_cols // LANES) + s] = gc[:, s * LANES:(s + 1) * LANES]

        i = idx // B_KV_HEADS
        g = idx % B_KV_HEADS
        r0 = pl.multiple_of(i * blk, blk)
        first_col = jnp.where(l * tile + r0 > 0, 0, WINDOW)
        mask = jnp.logical_and(band, coli >= first_col)
        kpad = jnp.concatenate([ktop_s[g, pl.ds(r0, 2 * blk), :], kbot_s[g, pl.ds(r0, 2 * blk), :]], axis=0)
        vpad = jnp.concatenate([vtop_s[g, pl.ds(r0, 2 * blk), :], vbot_s[g, pl.ds(r0, 2 * blk), :]], axis=0)
        qg = qb_s[i, pl.ds(g * pairs, pairs)].reshape(pairs * blk, LANES)
        s = _dot_nt(qg, kpad)
        ps, esinks = [], []
        for j in range(pairs):
            pj, ej = [], []
            for half in range(2):
                sink = sink_ref[g * B_GROUP + 2 * j + half] * LOG2E
                sh = jnp.where(mask, s[j * blk:(j + 1) * blk, half * 2 * blk:(half + 1) * 2 * blk], -jnp.inf)
                m = jnp.maximum(jnp.max(sh, axis=-1, keepdims=True), sink)
                pj.append(jnp.exp2(sh - m).astype(BF16))
                ej.append(jnp.exp2(sink - m))
            ps.append(jnp.concatenate(pj, axis=1))
            esinks.append(jnp.where(lo64_b, ej[0], ej[1]))
        o = _dot(jnp.concatenate(ps, axis=0), vpad)
        for j in range(pairs):
            oj = o[j * blk:(j + 1) * blk]
            osl_s[g * pairs + j, pl.ds(r0, blk), :] = oj[:, :LANES] / (oj[:, LANES:] + esinks[j])
        return carry

    lax.fori_loop(0, n_iter, attn_body, 0, unroll=4)

    for buf in (ktop_s, kbot_s):
        buf[:, 0:WINDOW, :] = buf[:, tile:tile + WINDOW, :]
    for buf in (vtop_s, vbot_s):
        buf[:, 0:WINDOW, 0:LANES] = buf[:, tile:tile + WINDOW, 0:LANES]

    y = h
    per = n_slabs // OUT_PROJ_CHUNKS
    for c in range(OUT_PROJ_CHUNKS):
        for sl in range(c * per, (c + 1) * per):
            og_s[:, sl * LANES:(sl + 1) * LANES] = (osl_s[sl] * _silu(gsl_s[sl])).astype(BF16)
        cols = slice(c * per * LANES, (c + 1) * per * LANES)
        y = y + _dot(og_s[:, cols], wout_ref[cols, :])
    y_ref[...] = y


def _swa_prompt(h, kv_norm, wkv, k_norm, bd, cos, sin, b_norm, wq, wg, q_norm, sinks, wout):
    bsz, seq, d = h.shape
    tile = min(PROMPT_TILE, seq)
    assert seq % tile == 0 and tile % WINDOW == 0 and seq >= WINDOW
    n_iter = (tile // WINDOW) * B_KV_HEADS
    gate_cols = BRANCH_WIDTH // n_iter
    assert gate_cols % LANES == 0
    wg = wg.reshape(d, n_iter, gate_cols).transpose(1, 0, 2)
    const = lambda shape: pl.BlockSpec(shape, lambda b, l: (0,) * len(shape), pipeline_mode=pl.Buffered(1))
    k_scratch = pltpu.VMEM((B_KV_HEADS, WINDOW + tile, LANES), BF16)
    v_scratch = pltpu.VMEM((B_KV_HEADS, WINDOW + tile, 2 * LANES), BF16)
    n_slabs = BRANCH_WIDTH // LANES
    return pl.pallas_call(
        _swa_prompt_kernel,
        grid=(bsz, seq // tile),
        in_specs=[
            pl.BlockSpec((None, tile, d), lambda b, l: (b, l, 0)),
            const(kv_norm.shape), const(wkv.shape), const(k_norm.shape), const(bd.shape),
            pl.BlockSpec((tile, LANES), lambda b, l: (l, 0)),
            pl.BlockSpec((tile, LANES), lambda b, l: (l, 0)),
            const(b_norm.shape), const(wq.shape), const(wg.shape), const(q_norm.shape),
            pl.BlockSpec(memory_space=pltpu.SMEM),
            const(wout.shape),
        ],
        out_specs=[
            pl.BlockSpec((None, tile, d), lambda b, l: (b, l, 0)),
            pl.BlockSpec((None, WINDOW, B_KV_WIDTH), lambda b, l: (b, 0, 0)),
            pl.BlockSpec((None, WINDOW, B_KV_WIDTH), lambda b, l: (b, 0, 0)),
        ],
        out_shape=[
            jax.ShapeDtypeStruct((bsz, seq, d), F32),
            jax.ShapeDtypeStruct((bsz, WINDOW, B_KV_WIDTH), F32),
            jax.ShapeDtypeStruct((bsz, WINDOW, B_KV_WIDTH), F32),
        ],
        scratch_shapes=[
            k_scratch, k_scratch, v_scratch, v_scratch,
            pltpu.VMEM((tile // WINDOW, n_slabs, WINDOW, LANES), BF16),
            pltpu.VMEM((tile, d), BF16),
            pltpu.VMEM((tile, 2 * B_KV_WIDTH), F32),
            pltpu.VMEM((tile, BRANCH_WIDTH), F32),
            pltpu.VMEM((n_slabs, tile, LANES), F32),
            pltpu.VMEM((n_slabs, tile, LANES), F32),
            pltpu.VMEM((tile, BRANCH_WIDTH), BF16),
        ],
        compiler_params=pltpu.CompilerParams(
            dimension_semantics=("arbitrary", "arbitrary"), vmem_limit_bytes=VMEM_LIMIT_BYTES),
        name="swa_prompt",
    )(h, kv_norm, wkv, k_norm, bd, cos, sin, b_norm, wq, wg, q_norm, sinks, wout)


def _norm_proj_kernel(x_ref, g_ref, w_ref, o_ref):
    o_ref[...] = _dot(_rmsnorm(x_ref[...], g_ref[...]).astype(BF16), w_ref[...])


def _norm_proj(x, g, w, name):
    m, d = x.shape
    n = w.shape[1]
    assert n % PROJ_COL_BLOCK == 0
    return pl.pallas_call(
        _norm_proj_kernel,
        grid=(n // PROJ_COL_BLOCK,),
        in_specs=[
            pl.BlockSpec((m, d), lambda j: (0, 0)),
            pl.BlockSpec((1, d), lambda j: (0, 0)),
            pl.BlockSpec((d, PROJ_COL_BLOCK), lambda j: (0, j)),
        ],
        out_specs=pl.BlockSpec((m, PROJ_COL_BLOCK), lambda j: (0, j)),
        out_shape=jax.ShapeDtypeStruct((m, n), F32),
        compiler_params=pltpu.CompilerParams(dimension_semantics=("arbitrary",)),
        name=name,
    )(x, g, w)


def _gla_out_kernel(o_ref, gate_ref, x_ref, on_ref, w_ref, y_ref, og_s):
    for h in range(A_HEADS):
        vs = slice(h * A_HEAD_DV, (h + 1) * A_HEAD_DV)
        o = _rmsnorm(o_ref[:, vs], on_ref[...])
        og_s[:, vs] = (o * _silu(gate_ref[:, vs])).astype(BF16)
    y_ref[...] = _dot(og_s[...], w_ref[...]) + x_ref[...]


def _gla_out(o, gate, x, out_norm, wout):
    m = o.shape[0]
    return pl.pallas_call(
        _gla_out_kernel,
        out_shape=jax.ShapeDtypeStruct(x.shape, F32),
        scratch_shapes=[pltpu.VMEM((m, BRANCH_WIDTH), BF16)],
        compiler_params=pltpu.CompilerParams(vmem_limit_bytes=VMEM_LIMIT_BYTES),
        name="gla_out_sample",
    )(o, gate, x, out_norm, wout)


def _out_proj_kernel(og_ref, x_ref, w_ref, y_ref):
    y_ref[...] = _dot(og_ref[...].astype(BF16), w_ref[...]) + x_ref[...]


def _out_proj(og, x, wout):
    return pl.pallas_call(
        _out_proj_kernel,
        out_shape=jax.ShapeDtypeStruct(x.shape, F32),
        compiler_params=pltpu.CompilerParams(vmem_limit_bytes=VMEM_LIMIT_BYTES),
        name="swa_out_sample",
    )(og, x, wout)


def _gla_sample_kernel(pj_ref, wg2_ref, bg_ref, st_ref, o_ref, nst_ref):
    nb, seq = pj_ref.shape[0], pj_ref.shape[1]
    dk, dv, kw = A_HEAD_DK, A_HEAD_DV, A_KEY_WIDTH
    scale = A_HEAD_DK ** -0.5
    pad_rows = SUBLANES - seq

    def per_seq(i, carry):
        pj = pj_ref[i]
        q = pj[:, 0:kw] * scale
        k = pj[:, kw:2 * kw]
        v = pj[:, 2 * kw:2 * kw + BRANCH_WIDTH]
        glow = pj[:, 2 * kw + 2 * BRANCH_WIDTH:2 * kw + 2 * BRANCH_WIDTH + LANES]
        glow8 = jnp.concatenate([glow, jnp.zeros((pad_rows, LANES), F32)], axis=0).astype(BF16)
        pre = _dot(glow8, wg2_ref[...])[0:seq] + bg_ref[...]
        loga = _log_sigmoid(pre) * (1.0 / A_GATE_NORMALIZER)
        bs = [loga[0:1]]
        for t in range(1, seq):
            bs.append(bs[-1] + loga[t:t + 1])
        bmat = jnp.concatenate(bs, axis=0)
        blast = bs[-1]
        qt = q * jnp.exp(bmat)
        kd = k * jnp.exp(blast - bmat)
        eblast = jnp.exp(blast)

        outs = [[jnp.zeros((1, dv), F32) for _ in range(A_HEADS)] for _ in range(seq)]
        for t in range(seq):
            for s in range(t + 1):
                w = q[t:t + 1] * k[s:s + 1] * jnp.exp(bs[t] - bs[s])
                for h in range(A_HEADS):
                    a = jnp.sum(w[:, h * dk:(h + 1) * dk], axis=-1, keepdims=True)
                    outs[t][h] = outs[t][h] + a * v[s:s + 1, h * dv:(h + 1) * dv]
        intra = jnp.concatenate([jnp.concatenate(outs[t], axis=1) for t in range(seq)], axis=0)

        qt8 = jnp.concatenate([qt, jnp.zeros((pad_rows, kw), F32)], axis=0).astype(BF16)
        inter = []
        for h in range(A_HEADS):
            ks = slice(h * dk, (h + 1) * dk)
            state = st_ref[i, h]
            inter.append(_dot(qt8[:, ks], state.astype(BF16))[0:seq])
            m = jnp.concatenate([kd[:, ks], eblast[:, ks], jnp.zeros((dk - seq - 1, dk), F32)], axis=0)
            mt = jnp.transpose(m)
            vpad = jnp.concatenate([v[:, h * dv:(h + 1) * dv], jnp.zeros((dk - seq, dv), F32)], axis=0)
            nst_ref[i, h] = state * mt[:, seq:seq + 1] + _dot(mt.astype(BF16), vpad.astype(BF16))
        o_ref[i] = intra + jnp.concatenate(inter, axis=1)
        return carry

    lax.fori_loop(0, nb, per_seq, 0)


def _gla_sample(pj, wg2, bg, state):
    nbatch, seq, n = pj.shape
    nb = min(SAMPLE_BATCH_BLOCK // 2, nbatch)
    assert nbatch % nb == 0
    st_spec = pl.BlockSpec((nb, A_HEADS, A_HEAD_DK, A_HEAD_DV), lambda b: (b, 0, 0, 0))
    return pl.pallas_call(
        _gla_sample_kernel,
        grid=(nbatch // nb,),
        in_specs=[
            pl.BlockSpec((nb, seq, n), lambda b: (b, 0, 0)),
            pl.BlockSpec(wg2.shape, lambda b: (0, 0)),
            pl.BlockSpec(bg.shape, lambda b: (0, 0)),
            st_spec,
        ],
        out_specs=[pl.BlockSpec((nb, seq, BRANCH_WIDTH), lambda b: (b, 0, 0)), st_spec],
        out_shape=[
            jax.ShapeDtypeStruct((nbatch, seq, BRANCH_WIDTH), F32),
            jax.ShapeDtypeStruct(state.shape, F32),
        ],
        compiler_params=pltpu.CompilerParams(
            dimension_semantics=("arbitrary",), vmem_limit_bytes=VMEM_LIMIT_BYTES),
        name="gla_sample",
    )(pj, wg2, bg, state)


def _swa_pre_sample_kernel(h_ref, kvn_ref, wkv_ref, kn_ref, bd_ref, cos_ref, sin_ref, bn_ref, wq_ref, wg_ref, qn_ref,
                           k_ref, v_ref, q_ref, gate_ref):
    m = h_ref.shape[0]
    kvw = B_KV_WIDTH
    h = h_ref[...]
    cos = cos_ref[...]
    sin = sin_ref[...]
    first_half, _ = _lane_masks(m)
    bd = bd_ref[...]

    kv = _dot(_rmsnorm(h, kvn_ref[...]).astype(BF16), wkv_ref[...])
    kn = _head_norm(kv[:, :kvw], bd, kn_ref[...])
    for s in range(kvw // LANES):
        k_ref[:, s * LANES:(s + 1) * LANES] = _rope_slab(kn[:, s * LANES:(s + 1) * LANES], cos, sin, first_half)
    v_ref[...] = kv[:, kvw:]

    ub = _rmsnorm(h, bn_ref[...]).astype(BF16)
    gate_ref[...] = _dot(ub, wg_ref[...])
    qgain = qn_ref[...] * (B_HEAD_DIM ** -0.5 * LOG2E)
    for c in range(BRANCH_WIDTH // kvw):
        qc = _head_norm(_dot(ub, wq_ref[:, c * kvw:(c + 1) * kvw]), bd, qgain)
        for s in range(kvw // LANES):
            lanes = slice(c * kvw + s * LANES, c * kvw + (s + 1) * LANES)
            q_ref[:, lanes] = _rope_slab(qc[:, s * LANES:(s + 1) * LANES], cos, sin, first_half)


def _swa_pre_sample(h, kv_norm, wkv, k_norm, bd, cos, sin, b_norm, wq, wg, q_norm):
    m = h.shape[0]
    return pl.pallas_call(
        _swa_pre_sample_kernel,
        out_shape=[
            jax.ShapeDtypeStruct((m, B_KV_WIDTH), F32),
            jax.ShapeDtypeStruct((m, B_KV_WIDTH), F32),
            jax.ShapeDtypeStruct((m, BRANCH_WIDTH), F32),
            jax.ShapeDtypeStruct((m, BRANCH_WIDTH), F32),
        ],
        compiler_params=pltpu.CompilerParams(vmem_limit_bytes=VMEM_LIMIT_BYTES),
        name="swa_pre_sample",
    )(h, kv_norm, wkv, k_norm, bd, cos, sin, b_norm, wq, wg, q_norm)


def _swa_sample_kernel(q_ref, gate_ref, kn_ref, vn_ref, ck_ref, cv_ref, sink_ref, og_ref, nk_ref, nv_ref):
    nb, seq = q_ref.shape[0], q_ref.shape[1]
    kvw = B_KV_WIDTH
    trows = SUBLANES
    nrows = B_GROUP * B_KV_HEADS * trows
    rowi = lax.broadcasted_iota(jnp.int32, (nrows, 2 * WINDOW), 0)
    coli = lax.broadcasted_iota(jnp.int32, (nrows, 2 * WINDOW), 1)
    dpos = coli - rowi % trows
    mask = jnp.logical_and(dpos >= 0, dpos <= WINDOW)
    lane8 = lax.broadcasted_iota(jnp.int32, (trows, kvw), 1) // B_HEAD_DIM
    grow = (lax.broadcasted_iota(jnp.int32, (nrows, kvw), 0) // trows) % B_KV_HEADS
    gkeep = grow == lax.broadcasted_iota(jnp.int32, (nrows, kvw), 1) // B_HEAD_DIM
    sink = sink_ref[...][:, 0:1] * LOG2E
    zq = jnp.zeros((trows - seq, BRANCH_WIDTH), F32)
    zk = jnp.zeros((WINDOW - seq, kvw), F32)

    def per_seq(i, carry):
        ck = ck_ref[i]
        cv = cv_ref[i]
        k_new = kn_ref[i]
        v_new = vn_ref[i]
        nk_ref[i, 0:WINDOW - seq, :] = ck[seq:, :]
        nk_ref[i, WINDOW - seq:WINDOW, :] = k_new
        nv_ref[i, 0:WINDOW - seq, :] = cv[seq:, :]
        nv_ref[i, WINDOW - seq:WINDOW, :] = v_new

        q8 = jnp.concatenate([q_ref[i], zq], axis=0)
        pieces = []
        for r in range(B_GROUP):
            slab = q8[:, r * kvw:(r + 1) * kvw]
            for g in range(B_KV_HEADS):
                pieces.append(jnp.where(lane8 == g, slab, 0.0))
        qrows = jnp.concatenate(pieces, axis=0).astype(BF16)
        kpad = jnp.concatenate([k_new, zk], axis=0).astype(BF16)
        vpad = jnp.concatenate([v_new, zk], axis=0).astype(BF16)
        s = jnp.concatenate([_dot_nt(qrows, ck.astype(BF16)), _dot_nt(qrows, kpad)], axis=1)
        s = jnp.where(mask, s, -jnp.inf)
        m = jnp.maximum(jnp.max(s, axis=-1, keepdims=True), sink)
        p = jnp.exp2(s - m)
        den = jnp.sum(p, axis=-1, keepdims=True) + jnp.exp2(sink - m)
        pb = p.astype(BF16)
        o = _dot(pb[:, :WINDOW], cv.astype(BF16)) + _dot(pb[:, WINDOW:], vpad)
        o = jnp.where(gkeep, o / den, 0.0)
        outs = []
        for r in range(B_GROUP):
            acc = o[r * B_KV_HEADS * trows:r * B_KV_HEADS * trows + trows]
            for g in range(1, B_KV_HEADS):
                acc = acc + o[(r * B_KV_HEADS + g) * trows:(r * B_KV_HEADS + g + 1) * trows]
            outs.append(acc)
        og = jnp.concatenate(outs, axis=1)[0:seq]
        og_ref[i] = og * _silu(gate_ref[i])
        return carry

    lax.fori_loop(0, nb, per_seq, 0, unroll=4)


def _swa_sample(q, gate, k_new, v_new, ck, cv, sink_rows):
    nbatch, seq, _ = q.shape
    nb = min(SAMPLE_BATCH_BLOCK, nbatch)
    assert nbatch % nb == 0 and ck.shape[1] == WINDOW
    cache_spec = pl.BlockSpec((nb, WINDOW, B_KV_WIDTH), lambda b: (b, 0, 0))
    wide_spec = pl.BlockSpec((nb, seq, BRANCH_WIDTH), lambda b: (b, 0, 0))
    new_spec = pl.BlockSpec((nb, seq, B_KV_WIDTH), lambda b: (b, 0, 0))
    return pl.pallas_call(
        _swa_sample_kernel,
        grid=(nbatch // nb,),
        in_specs=[wide_spec, wide_spec, new_spec, new_spec, cache_spec, cache_spec,
                  pl.BlockSpec(sink_rows.shape, lambda b: (0, 0))],
        out_specs=[wide_spec, cache_spec, cache_spec],
        out_shape=[
            jax.ShapeDtypeStruct((nbatch, seq, BRANCH_WIDTH), F32),
            jax.ShapeDtypeStruct(ck.shape, F32),
            jax.ShapeDtypeStruct(cv.shape, F32),
        ],
        compiler_params=pltpu.CompilerParams(
            dimension_semantics=("arbitrary",), vmem_limit_bytes=VMEM_LIMIT_BYTES),
        name="swa_sample",
    )(q, gate, k_new, v_new, ck, cv, sink_rows)


def _rope_tables(pos, rows):
    half = B_HEAD_DIM // 2
    inv_freq = ROPE_THETA ** (-jnp.arange(half, dtype=F32) / half)
    ang = pos.astype(F32)[:, None] * inv_freq[None, :]
    cos, sin = jnp.cos(ang), jnp.sin(ang)
    cos_t = jnp.concatenate([cos, cos, cos, cos], axis=1)
    sin_t = jnp.concatenate([-sin, sin, -sin, sin], axis=1)
    pad = rows - pos.shape[0]
    if pad:
        cos_t = jnp.pad(cos_t, ((0, pad), (0, 0)))
        sin_t = jnp.pad(sin_t, ((0, pad), (0, 0)))
    return cos_t, sin_t


def kernel(x_prompt, x_sample, state_gla, cache_swa_k, cache_swa_v, a_norm, a_w_in, a_w_gate2, a_b_gate,
           a_out_norm, a_w_out, kv_norm, w_k, w_v, k_norm, b_norm, b_w_in, b_q_norm, b_sinks, b_w_out):
    assert a_norm.shape[0] == 1 and b_norm.shape[0] == 1
    bsz_p, seq_p, d = x_prompt.shape
    bsz_s, seq_s, _ = x_sample.shape
    kw, bw = A_KEY_WIDTH, BRANCH_WIDTH

    w_in = a_w_in[0]
    wa_qk = w_in[:, :2 * kw].astype(BF16)
    wa_v = w_in[:, 2 * kw:2 * kw + bw].astype(BF16)
    wa_g = w_in[:, 2 * kw + bw:2 * kw + 2 * bw].astype(BF16)
    wa_low = jnp.pad(w_in[:, 2 * kw + 2 * bw:], ((0, 0), (0, LANES - A_GATE_RANK))).astype(BF16)
    wa_g2 = jnp.pad(a_w_gate2[0], ((0, LANES - A_GATE_RANK), (0, 0))).astype(BF16)
    a_bg = a_b_gate[0][None, :]
    a_n = a_norm[0][None, :]
    a_on = a_out_norm[0][None, :]
    wa_out = a_w_out[0].astype(BF16)
    w_kv = jnp.concatenate([w_k, w_v], axis=1).astype(BF16)
    wb_q = b_w_in[0][:, :bw].astype(BF16)
    wb_g = b_w_in[0][:, bw:].astype(BF16)
    wb_out = b_w_out[0].astype(BF16)
    kvn = kv_norm[None, :]
    bn = b_norm[0][None, :]
    kn_t = jnp.tile(k_norm, B_KV_WIDTH // B_HEAD_DIM)[None, :]
    qn_t = jnp.tile(b_q_norm[0], B_KV_WIDTH // B_HEAD_DIM)[None, :]
    sinks = b_sinks[0]
    grp = jnp.arange(B_KV_WIDTH) // B_HEAD_DIM
    bd = (grp[:, None] == grp[None, :]).astype(BF16)

    cos_p, sin_p = _rope_tables(jnp.arange(seq_p, dtype=jnp.int32), seq_p)
    h1_p, st_p = _gla_prompt(x_prompt, a_n, wa_qk, wa_v, wa_g, wa_low, wa_g2, a_bg, a_on, wa_out)
    y_p, kc_p, vc_p = _swa_prompt(h1_p, kvn, w_kv, kn_t, bd, cos_p, sin_p, bn, wb_q, wb_g, qn_t, sinks, wb_out)

    m = bsz_s * seq_s
    xs = x_sample.reshape(m, d)
    low_pad = PROJ_COL_BLOCK - A_GATE_RANK
    wa_all = jnp.concatenate(
        [wa_qk, wa_v, wa_g, jnp.pad(w_in[:, 2 * kw + 2 * bw:], ((0, 0), (0, low_pad))).astype(BF16)], axis=1)
    pj_a = _norm_proj(xs, a_n, wa_all, "gla_proj_sample")
    o_s, st_s = _gla_sample(pj_a.reshape(bsz_s, seq_s, -1), wa_g2, a_bg, state_gla[0])
    gate_a = pj_a[:, 2 * kw + bw:2 * kw + 2 * bw]
    h1_s = _gla_out(o_s.reshape(m, bw), gate_a, xs, a_on, wa_out)

    def cols_rg(w):
        return w.reshape(d, B_KV_HEADS, B_GROUP, B_HEAD_DIM).transpose(0, 2, 1, 3).reshape(d, bw)

    wb_out_rg = wb_out.reshape(B_KV_HEADS, B_GROUP, B_HEAD_DIM, d).transpose(1, 0, 2, 3).reshape(bw, d)
    sink_rows = jnp.broadcast_to(
        sinks.reshape(B_KV_HEADS, B_GROUP).T[:, :, None, None], (B_GROUP, B_KV_HEADS, SUBLANES, LANES)
    ).reshape(B_GROUP * B_KV_HEADS * SUBLANES, LANES)
    pos_s = PAST_LEN + jnp.arange(seq_s, dtype=jnp.int32)
    cos_s, sin_s = _rope_tables(pos_s, seq_s)
    cos_s, sin_s = jnp.tile(cos_s, (bsz_s, 1)), jnp.tile(sin_s, (bsz_s, 1))
    k_new, v_new, q_s, gate_b = _swa_pre_sample(
        h1_s, kvn, w_kv, kn_t, bd, cos_s, sin_s, bn, cols_rg(wb_q), cols_rg(wb_g), qn_t)
    og_s, nk_s, nv_s = _swa_sample(
        q_s.reshape(bsz_s, seq_s, bw), gate_b.reshape(bsz_s, seq_s, bw),
        k_new.reshape(bsz_s, seq_s, B_KV_WIDTH), v_new.reshape(bsz_s, seq_s, B_KV_WIDTH),
        cache_swa_k.reshape(bsz_s, WINDOW, B_KV_WIDTH), cache_swa_v.reshape(bsz_s, WINDOW, B_KV_WIDTH), sink_rows)
    y_s = _out_proj(og_s.reshape(m, bw), h1_s, wb_out_rg)

    return (y_p, y_s.reshape(bsz_s, seq_s, d),
            st_p[None], st_s[None],
            kc_p.reshape(bsz_p, WINDOW, B_KV_HEADS, B_HEAD_DIM), vc_p.reshape(bsz_p, WINDOW, B_KV_HEADS, B_HEAD_DIM),
            nk_s.reshape(bsz_s, WINDOW, B_KV_HEADS, B_HEAD_DIM), nv_s.reshape(bsz_s, WINDOW, B_KV_HEADS, B_HEAD_DIM))
```

```python
import jax
import jax.numpy as jnp
import numpy as np
from jax import lax
from jax.experimental import pallas as pl
from jax.experimental.pallas import tpu as pltpu

F32 = jnp.float32
BF16 = jnp.bfloat16

A_HEADS = 4
A_HEAD_DK = 128
A_HEAD_DV = 512
A_KEY_WIDTH = A_HEADS * A_HEAD_DK
BRANCH_WIDTH = A_HEADS * A_HEAD_DV
A_GATE_RANK = 16
A_GATE_NORMALIZER = 16.0
B_HEAD_DIM = 64
B_HEADS = BRANCH_WIDTH // B_HEAD_DIM
B_KV_HEADS = 4
B_GROUP = B_HEADS // B_KV_HEADS
B_KV_WIDTH = B_KV_HEADS * B_HEAD_DIM
WINDOW = 128
ROPE_THETA = 10000.0
RMS_EPS = 1e-6
PAST_LEN = 16384

LANES = 128
SUBLANES = 8
VMEM_LIMIT_BYTES = 56 * 1024 * 1024

PROMPT_TILE = 256
GLA_CHUNK = 256
GLA_SAFE_DECAY = 80.0
SAMPLE_BATCH_BLOCK = 8
OUT_PROJ_CHUNKS = 4
LOG2E = 1.4426950408889634


def _dot(a, b):
    return jnp.dot(a, b, preferred_element_type=F32)


def _dot_nt(a, b):
    return lax.dot_general(a, b, (((1,), (1,)), ((), ())), preferred_element_type=F32)


def _split2(x):
    hi = x.astype(BF16)
    lo = (x - hi.astype(F32)).astype(BF16)
    return hi, lo


def _rmsnorm(x, g):
    ms = jnp.mean(x * x, axis=-1, keepdims=True)
    return x * lax.rsqrt(ms + RMS_EPS) * g


def _log_sigmoid(x):
    return jnp.minimum(x, 0.0) - jnp.log1p(jnp.exp(-jnp.abs(x)))


def _silu(x):
    return x * (1.0 / (1.0 + jnp.exp(-x)))


def _head_norm(x, bd, gain):
    hi, lo = _split2(x * x)
    ss = _dot(hi, bd) + _dot(lo, bd)
    return x * lax.rsqrt(ss * (1.0 / B_HEAD_DIM) + RMS_EPS) * gain


def _rope_slab(x, cos, sin_signed, first_half):
    xr = jnp.where(first_half, pltpu.roll(x, 96, 1), pltpu.roll(x, 32, 1))
    return x * cos + xr * sin_signed


def _lane_masks(rows):
    lane = lax.broadcasted_iota(jnp.int32, (rows, LANES), 1)
    first_half = (lane % B_HEAD_DIM) < (B_HEAD_DIM // 2)
    lo64 = lane < B_HEAD_DIM
    return first_half, lo64


def _split_group(slab, g, lo64):
    swapped = pltpu.roll(slab, B_HEAD_DIM, 1)
    zero = jnp.zeros_like(slab)
    if g % 2 == 0:
        return jnp.where(lo64, slab, zero), jnp.where(lo64, zero, swapped)
    return jnp.where(lo64, swapped, zero), jnp.where(lo64, zero, slab)


def _regroup_slab(a, b, odd, lo64):
    if odd:
        return jnp.where(lo64, pltpu.roll(a, B_HEAD_DIM, 1), b)
    return jnp.where(lo64, a, pltpu.roll(b, B_HEAD_DIM, 1))


def _gla_prompt_kernel(x_ref, an_ref, wqk_ref, wv_ref, wg_ref, wlow_ref, wg2_ref, bg_ref, on_ref, wout_ref,
                       h1_ref, st_ref,
                       qk_s, v_s, gate_s, b_s, o_s, oi_s):
    tile = x_ref.shape[0]
    chunk = GLA_CHUNK
    n_chunks = tile // chunk
    dk, dv, kw = A_HEAD_DK, A_HEAD_DV, A_KEY_WIDTH

    @pl.when(pl.program_id(1) == 0)
    def _():
        st_ref[...] = jnp.zeros_like(st_ref)

    x = x_ref[...]
    u = _rmsnorm(x, an_ref[...]).astype(BF16)
    qk_s[...] = _dot(u, wqk_ref[...])
    v_s[...] = _dot(u, wv_ref[...])
    gate_s[...] = _dot(u, wg_ref[...])
    glow = _dot(u, wlow_ref[...]).astype(BF16)
    pre = _dot(glow, wg2_ref[...]) + bg_ref[...]
    loga = _log_sigmoid(pre) * (1.0 / A_GATE_NORMALIZER)

    row = lax.broadcasted_iota(jnp.int32, (chunk, chunk), 0)
    col = lax.broadcasted_iota(jnp.int32, (chunk, chunk), 1)
    lower = row >= col
    tri = jnp.where(lower, 1.0, 0.0).astype(BF16)
    for c in range(n_chunks):
        hi, lo = _split2(loga[c * chunk:(c + 1) * chunk])
        b_s[c * chunk:(c + 1) * chunk, :] = _dot(tri, hi) + _dot(tri, lo)
    safe = jnp.min(b_s[...]) >= -GLA_SAFE_DECAY

    scale = A_HEAD_DK ** -0.5
    for c in range(n_chunks):
        rows = slice(c * chunk, (c + 1) * chunk)
        b = b_s[rows, :]
        eb = jnp.exp(b)
        enb = jnp.exp(-b)
        blast = b[chunk - 1:chunk, :]
        ekl = jnp.exp(blast - b)
        eblast = jnp.exp(blast)
        for h in range(A_HEADS):
            ks = slice(h * dk, (h + 1) * dk)
            vs = slice(h * dv, (h + 1) * dv)
            q = qk_s[rows, ks] * scale
            k = qk_s[rows, kw + h * dk: kw + (h + 1) * dk]
            vh = v_s[rows, vs].astype(BF16)
            state = st_ref[h]
            qt = (q * eb[:, ks]).astype(BF16)
            o_inter = _dot(qt, state.astype(BF16))
            a = _dot_nt(qt, (k * enb[:, ks]).astype(BF16))
            a = jnp.where(lower, a, 0.0).astype(BF16)
            oi_s[rows, vs] = o_inter
            o_s[rows, vs] = o_inter + _dot(a, vh)
            kd_t = jnp.transpose(k * ekl[:, ks]).astype(BF16)
            dec = jnp.transpose(jnp.broadcast_to(eblast[:, ks], (dk, dk)))
            st_ref[h] = state * jnp.concatenate([dec] * (dv // dk), axis=1) + _dot(kd_t, vh)

    @pl.when(jnp.logical_not(safe))
    def _():
        trow = lax.broadcasted_iota(jnp.int32, (chunk, dk), 0)
        sub8 = lax.broadcasted_iota(jnp.int32, (SUBLANES, 1), 0)
        for c in range(n_chunks):
            rows = slice(c * chunk, (c + 1) * chunk)
            for h in range(A_HEADS):
                ks = slice(h * dk, (h + 1) * dk)
                vs = slice(h * dv, (h + 1) * dv)
                q = qk_s[rows, ks] * scale
                b = b_s[rows, ks]

                def body(s, acc, c=c, h=h, q=q, b=b):
                    r8 = pl.multiple_of(c * chunk + (s // SUBLANES) * SUBLANES, SUBLANES)
                    pick = sub8 == s % SUBLANES

                    def row_of(ref, lanes):
                        return jnp.sum(jnp.where(pick, ref[pl.ds(r8, SUBLANES), lanes], 0.0), axis=0, keepdims=True)

                    brow = row_of(b_s, slice(h * dk, (h + 1) * dk))
                    krow = row_of(qk_s, slice(kw + h * dk, kw + (h + 1) * dk))
                    vrow = row_of(v_s, slice(h * dv, (h + 1) * dv))
                    w = jnp.where(trow >= s, jnp.exp(jnp.minimum(b - brow, 0.0)), 0.0)
                    colv = jnp.sum(q * krow * w, axis=-1, keepdims=True)
                    return acc + colv * vrow

                o_s[rows, vs] = oi_s[rows, vs] + lax.fori_loop(0, chunk, body, jnp.zeros((chunk, dv), F32))

    y = x
    for h in range(A_HEADS):
        vs = slice(h * dv, (h + 1) * dv)
        o = _rmsnorm(o_s[:, vs], on_ref[...])
        og = (o * _silu(gate_s[:, vs])).astype(BF16)
        y = y + _dot(og, wout_ref[vs, :])
    h1_ref[...] = y


def _gla_prompt(x, a_norm, wqk, wv, wg, wlow, wg2, bg, out_norm, wout):
    bsz, seq, d = x.shape
    tile = min(PROMPT_TILE, seq)
    assert seq % tile == 0 and tile % GLA_CHUNK == 0
    const = lambda shape: pl.BlockSpec(shape, lambda b, l: (0,) * len(shape), pipeline_mode=pl.Buffered(1))
    return pl.pallas_call(
        _gla_prompt_kernel,
        grid=(bsz, seq // tile),
        in_specs=[
            pl.BlockSpec((None, tile, d), lambda b, l: (b, l, 0)),
            const(a_norm.shape), const(wqk.shape), const(wv.shape), const(wg.shape), const(wlow.shape),
            const(wg2.shape), const(bg.shape), const(out_norm.shape), const(wout.shape),
        ],
        out_specs=[
            pl.BlockSpec((None, tile, d), lambda b, l: (b, l, 0)),
            pl.BlockSpec((None, A_HEADS, A_HEAD_DK, A_HEAD_DV), lambda b, l: (b, 0, 0, 0)),
        ],
        out_shape=[
            jax.ShapeDtypeStruct((bsz, seq, d), F32),
            jax.ShapeDtypeStruct((bsz, A_HEADS, A_HEAD_DK, A_HEAD_DV), F32),
        ],
        scratch_shapes=[
            pltpu.VMEM((tile, 2 * A_KEY_WIDTH), F32),
            pltpu.VMEM((tile, BRANCH_WIDTH), F32),
            pltpu.VMEM((tile, BRANCH_WIDTH), F32),
            pltpu.VMEM((tile, A_KEY_WIDTH), F32),
            pltpu.VMEM((tile, BRANCH_WIDTH), F32),
            pltpu.VMEM((tile, BRANCH_WIDTH), F32),
        ],
        compiler_params=pltpu.CompilerParams(
            dimension_semantics=("arbitrary", "arbitrary"), vmem_limit_bytes=VMEM_LIMIT_BYTES),
        name="gla_prompt",
    )(x, a_norm, wqk, wv, wg, wlow, wg2, bg, out_norm, wout)


def _swa_prompt_kernel(h_ref, kvn_ref, wkv_ref, kn_ref, bd_ref, cos_ref, sin_ref, bn_ref, wq_ref, wg_ref,
                       qn_ref, sink_ref, wout_ref,
                       y_ref, kc_ref, vc_ref,
                       ktop_s, kbot_s, vtop_s, vbot_s, qb_s, ub_s, kv_s, q_s, gsl_s, osl_s, og_s):
    tile = h_ref.shape[0]
    l = pl.program_id(1)
    blk = WINDOW
    kvw = B_KV_WIDTH
    pairs = B_GROUP // 2
    n_slabs = BRANCH_WIDTH // LANES
    n_iter = (tile // blk) * B_KV_HEADS
    gate_cols = wg_ref.shape[2]

    @pl.when(l == 0)
    def _():
        zeros = jnp.zeros((B_KV_HEADS, WINDOW, LANES), BF16)
        ktop_s[:, 0:WINDOW, :] = zeros
        kbot_s[:, 0:WINDOW, :] = zeros
        vtop_s[:, 0:WINDOW, 0:LANES] = zeros
        vbot_s[:, 0:WINDOW, 0:LANES] = zeros
        lane = lax.broadcasted_iota(jnp.int32, (B_KV_HEADS, WINDOW + tile, LANES), 2)
        vtop_s[:, :, LANES:] = jnp.where(lane < B_HEAD_DIM, 1.0, 0.0).astype(BF16)
        vbot_s[:, :, LANES:] = jnp.where(lane < B_HEAD_DIM, 0.0, 1.0).astype(BF16)

    h = h_ref[...]
    cos = cos_ref[...]
    sin = sin_ref[...]
    first_half, lo64 = _lane_masks(tile)
    bd = bd_ref[...]

    kv_s[...] = _dot(_rmsnorm(h, kvn_ref[...]).astype(BF16), wkv_ref[...])
    ub_s[...] = _rmsnorm(h, bn_ref[...]).astype(BF16)

    kn = _head_norm(kv_s[:, :kvw], bd, kn_ref[...])
    v = kv_s[:, kvw:]
    k_slabs = [_rope_slab(kn[:, s * LANES:(s + 1) * LANES], cos, sin, first_half) for s in range(kvw // LANES)]

    @pl.when(l == pl.num_programs(1) - 1)
    def _():
        kc_ref[...] = jnp.concatenate(k_slabs, axis=1)[tile - WINDOW:, :]
        vc_ref[...] = v[tile - WINDOW:, :]

    for g in range(B_KV_HEADS):
        s = g // 2
        top, bot = _split_group(k_slabs[s], g, lo64)
        ktop_s[g, WINDOW:WINDOW + tile, :] = top.astype(BF16)
        kbot_s[g, WINDOW:WINDOW + tile, :] = bot.astype(BF16)
        top, bot = _split_group(v[:, s * LANES:(s + 1) * LANES], g, lo64)
        vtop_s[g, WINDOW:WINDOW + tile, 0:LANES] = top.astype(BF16)
        vbot_s[g, WINDOW:WINDOW + tile, 0:LANES] = bot.astype(BF16)

    qscale = qn_ref[...] * (B_HEAD_DIM ** -0.5 * LOG2E)
    qcos = [cos * qscale[:, s * LANES:(s + 1) * LANES] for s in range(kvw // LANES)]
    qsin = [sin * jnp.where(first_half[0:1], pltpu.roll(qscale[:, s * LANES:(s + 1) * LANES], 96, 1),
                            pltpu.roll(qscale[:, s * LANES:(s + 1) * LANES], 32, 1)) for s in range(kvw // LANES)]

    q_s[...] = _dot(ub_s[...], wq_ref[...])
    for c in range(BRANCH_WIDTH // kvw):
        x = q_s[:, c * kvw:(c + 1) * kvw]
        hi, lo = _split2(x * x)
        rinv = lax.rsqrt((_dot(hi, bd) + _dot(lo, bd)) * (1.0 / B_HEAD_DIM) + RMS_EPS)
        for s in range(kvw // LANES):
            xs = x[:, s * LANES:(s + 1) * LANES]
            xr = jnp.where(first_half, pltpu.roll(xs, 96, 1), pltpu.roll(xs, 32, 1))
            qr = ((xs * qcos[s] + xr * qsin[s]) * rinv[:, s * LANES:(s + 1) * LANES]).astype(BF16)
            for i in range(tile // blk):
                qb_s[i, c * (kvw // LANES) + s] = qr[i * blk:(i + 1) * blk]

    rowi = lax.broadcasted_iota(jnp.int32, (blk, 2 * blk), 0)
    coli = lax.broadcasted_iota(jnp.int32, (blk, 2 * blk), 1)
    band = jnp.logical_and(coli - rowi >= 0, coli - rowi <= WINDOW)
    _, lo64_b = _lane_masks(blk)

    def attn_body(idx, carry):
        gc = _dot(ub_s[...], wg_ref[idx])
        for s in range(gate_cols // LANES):
            gsl_s[idx * (gate_cols // LANES) + s] = gc[:, s * LANES:(s + 1) * LANES]

        i = idx // B_KV_HEADS
        g = idx % B_KV_HEADS
        r0 = pl.multiple_of(i * blk, blk)
        first_col = jnp.where(l * tile + r0 > 0, 0, WINDOW)
        mask = jnp.logical_and(band, coli >= first_col)
        kpad = jnp.concatenate([ktop_s[g, pl.ds(r0, 2 * blk), :], kbot_s[g, pl.ds(r0, 2 * blk), :]], axis=0)
        vpad = jnp.concatenate([vtop_s[g, pl.ds(r0, 2 * blk), :], vbot_s[g, pl.ds(r0, 2 * blk), :]], axis=0)
        qg = qb_s[i, pl.ds(g * pairs, pairs)].reshape(pairs * blk, LANES)
        s = _dot_nt(qg, kpad)
        ps, esinks = [], []
        for j in range(pairs):
            pj, ej = [], []
            for half in range(2):
                sink = sink_ref[g * B_GROUP + 2 * j + half] * LOG2E
                sh = jnp.where(mask, s[j * blk:(j + 1) * blk, half * 2 * blk:(half + 1) * 2 * blk], -jnp.inf)
                m = jnp.maximum(jnp.max(sh, axis=-1, keepdims=True), sink)
                pj.append(jnp.exp2(sh - m).astype(BF16))
                ej.append(jnp.exp2(sink - m))
            ps.append(jnp.concatenate(pj, axis=1))
            esinks.append(jnp.where(lo64_b, ej[0], ej[1]))
        o = _dot(jnp.concatenate(ps, axis=0), vpad)
        for j in range(pairs):
            oj = o[j * blk:(j + 1) * blk]
            osl_s[g * pairs + j, pl.ds(r0, blk), :] = oj[:, :LANES] / (oj[:, LANES:] + esinks[j])
        return carry

    lax.fori_loop(0, n_iter, attn_body, 0, unroll=4)

    for buf in (ktop_s, kbot_s):
        buf[:, 0:WINDOW, :] = buf[:, tile:tile + WINDOW, :]
    for buf in (vtop_s, vbot_s):
        buf[:, 0:WINDOW, 0:LANES] = buf[:, tile:tile + WINDOW, 0:LANES]

    y = h
    per = n_slabs // OUT_PROJ_CHUNKS
    for c in range(OUT_PROJ_CHUNKS):
        for sl in range(c * per, (c + 1) * per):
            og_s[:, sl * LANES:(sl + 1) * LANES] = (osl_s[sl] * _silu(gsl_s[sl])).astype(BF16)
        cols = slice(c * per * LANES, (c + 1) * per * LANES)
        y = y + _dot(og_s[:, cols], wout_ref[cols, :])
    y_ref[...] = y


def _swa_prompt(h, kv_norm, wkv, k_norm, bd, cos, sin, b_norm, wq, wg, q_norm, sinks, wout):
    bsz, seq, d = h.shape
    tile = min(PROMPT_TILE, seq)
    assert seq % tile == 0 and tile % WINDOW == 0 and seq >= WINDOW
    n_iter = (tile // WINDOW) * B_KV_HEADS
    gate_cols = BRANCH_WIDTH // n_iter
    assert gate_cols % LANES == 0
    wg = wg.reshape(d, n_iter, gate_cols).transpose(1, 0, 2)
    const = lambda shape: pl.BlockSpec(shape, lambda b, l: (0,) * len(shape), pipeline_mode=pl.Buffered(1))
    k_scratch = pltpu.VMEM((B_KV_HEADS, WINDOW + tile, LANES), BF16)
    v_scratch = pltpu.VMEM((B_KV_HEADS, WINDOW + tile, 2 * LANES), BF16)
    n_slabs = BRANCH_WIDTH // LANES
    return pl.pallas_call(
        _swa_prompt_kernel,
        grid=(bsz, seq // tile),
        in_specs=[
            pl.BlockSpec((None, tile, d), lambda b, l: (b, l, 0)),
            const(kv_norm.shape), const(wkv.shape), const(k_norm.shape), const(bd.shape),
            pl.BlockSpec((tile, LANES), lambda b, l: (l, 0)),
            pl.BlockSpec((tile, LANES), lambda b, l: (l, 0)),
            const(b_norm.shape), const(wq.shape), const(wg.shape), const(q_norm.shape),
            pl.BlockSpec(memory_space=pltpu.SMEM),
            const(wout.shape),
        ],
        out_specs=[
            pl.BlockSpec((None, tile, d), lambda b, l: (b, l, 0)),
            pl.BlockSpec((None, WINDOW, B_KV_WIDTH), lambda b, l: (b, 0, 0)),
            pl.BlockSpec((None, WINDOW, B_KV_WIDTH), lambda b, l: (b, 0, 0)),
        ],
        out_shape=[
            jax.ShapeDtypeStruct((bsz, seq, d), F32),
            jax.ShapeDtypeStruct((bsz, WINDOW, B_KV_WIDTH), F32),
            jax.ShapeDtypeStruct((bsz, WINDOW, B_KV_WIDTH), F32),
        ],
        scratch_shapes=[
            k_scratch, k_scratch, v_scratch, v_scratch,
            pltpu.VMEM((tile // WINDOW, n_slabs, WINDOW, LANES), BF16),
            pltpu.VMEM((tile, d), BF16),
            pltpu.VMEM((tile, 2 * B_KV_WIDTH), F32),
            pltpu.VMEM((tile, BRANCH_WIDTH), F32),
            pltpu.VMEM((n_slabs, tile, LANES), F32),
            pltpu.VMEM((n_slabs, tile, LANES), F32),
            pltpu.VMEM((tile, BRANCH_WIDTH), BF16),
        ],
        compiler_params=pltpu.CompilerParams(
            dimension_semantics=("arbitrary", "arbitrary"), vmem_limit_bytes=VMEM_LIMIT_BYTES),
        name="swa_prompt",
    )(h, kv_norm, wkv, k_norm, bd, cos, sin, b_norm, wq, wg, q_norm, sinks, wout)


def _gla_proj_sample_kernel(x_ref, g_ref, wqk_ref, wv_ref, wg_ref, wlow_ref, qk_ref, v_ref, gate_ref, glow_ref):
    u = _rmsnorm(x_ref[...], g_ref[...]).astype(BF16)
    qk_ref[...] = _dot(u, wqk_ref[...])
    v_ref[...] = _dot(u, wv_ref[...])
    gate_ref[...] = _dot(u, wg_ref[...])
    glow_ref[...] = _dot(u, wlow_ref[...])


def _gla_proj_sample(x, g, wqk, wv, wg, wlow):
    m = x.shape[0]
    return pl.pallas_call(
        _gla_proj_sample_kernel,
        out_shape=[jax.ShapeDtypeStruct((m, w.shape[1]), F32) for w in (wqk, wv, wg, wlow)],
        compiler_params=pltpu.CompilerParams(vmem_limit_bytes=VMEM_LIMIT_BYTES),
        name="gla_proj_sample",
    )(x, g, wqk, wv, wg, wlow)


def _gla_out_kernel(o_ref, gate_ref, x_ref, on_ref, w_ref, y_ref, og_s):
    for h in range(A_HEADS):
        vs = slice(h * A_HEAD_DV, (h + 1) * A_HEAD_DV)
        o = _rmsnorm(o_ref[:, vs], on_ref[...])
        og_s[:, vs] = (o * _silu(gate_ref[:, vs])).astype(BF16)
    y_ref[...] = _dot(og_s[...], w_ref[...]) + x_ref[...]


def _gla_out(o, gate, x, out_norm, wout):
    m = o.shape[0]
    return pl.pallas_call(
        _gla_out_kernel,
        out_shape=jax.ShapeDtypeStruct(x.shape, F32),
        scratch_shapes=[pltpu.VMEM((m, BRANCH_WIDTH), BF16)],
        compiler_params=pltpu.CompilerParams(vmem_limit_bytes=VMEM_LIMIT_BYTES),
        name="gla_out_sample",
    )(o, gate, x, out_norm, wout)


def _out_proj_kernel(og_ref, x_ref, w_ref, y_ref):
    y_ref[...] = _dot(og_ref[...].astype(BF16), w_ref[...]) + x_ref[...]


def _out_proj(og, x, wout):
    return pl.pallas_call(
        _out_proj_kernel,
        out_shape=jax.ShapeDtypeStruct(x.shape, F32),
        compiler_params=pltpu.CompilerParams(vmem_limit_bytes=VMEM_LIMIT_BYTES),
        name="swa_out_sample",
    )(og, x, wout)


def _gla_sample_kernel(qk_ref, v_ref, glow_ref, wg2_ref, bg_ref, st_ref, o_ref, nst_ref):
    nb, seq = qk_ref.shape[0], qk_ref.shape[1]
    dk, dv, kw = A_HEAD_DK, A_HEAD_DV, A_KEY_WIDTH
    scale = A_HEAD_DK ** -0.5
    pad_rows = SUBLANES - seq

    def per_seq(i, carry):
        qk = qk_ref[i]
        q = qk[:, 0:kw] * scale
        k = qk[:, kw:2 * kw]
        v = v_ref[i]
        glow = glow_ref[i]
        glow8 = jnp.concatenate([glow, jnp.zeros((pad_rows, LANES), F32)], axis=0).astype(BF16)
        pre = _dot(glow8, wg2_ref[...])[0:seq] + bg_ref[...]
        loga = _log_sigmoid(pre) * (1.0 / A_GATE_NORMALIZER)
        bs = [loga[0:1]]
        for t in range(1, seq):
            bs.append(bs[-1] + loga[t:t + 1])
        bmat = jnp.concatenate(bs, axis=0)
        blast = bs[-1]
        qt = q * jnp.exp(bmat)
        kd = k * jnp.exp(blast - bmat)
        eblast = jnp.exp(blast)

        outs = [[jnp.zeros((1, dv), F32) for _ in range(A_HEADS)] for _ in range(seq)]
        for t in range(seq):
            for s in range(t + 1):
                w = q[t:t + 1] * k[s:s + 1] * jnp.exp(bs[t] - bs[s])
                for h in range(A_HEADS):
                    a = jnp.sum(w[:, h * dk:(h + 1) * dk], axis=-1, keepdims=True)
                    outs[t][h] = outs[t][h] + a * v[s:s + 1, h * dv:(h + 1) * dv]
        intra = jnp.concatenate([jnp.concatenate(outs[t], axis=1) for t in range(seq)], axis=0)

        qt8 = jnp.concatenate([qt, jnp.zeros((pad_rows, kw), F32)], axis=0).astype(BF16)
        inter = []
        for h in range(A_HEADS):
            ks = slice(h * dk, (h + 1) * dk)
            state = st_ref[i, h]
            inter.append(_dot(qt8[:, ks], state.astype(BF16))[0:seq])
            m = jnp.concatenate([kd[:, ks], eblast[:, ks], jnp.zeros((dk - seq - 1, dk), F32)], axis=0)
            mt = jnp.transpose(m)
            vpad = jnp.concatenate([v[:, h * dv:(h + 1) * dv], jnp.zeros((dk - seq, dv), F32)], axis=0)
            nst_ref[i, h] = state * mt[:, seq:seq + 1] + _dot(mt.astype(BF16), vpad.astype(BF16))
        o_ref[i] = intra + jnp.concatenate(inter, axis=1)
        return carry

    lax.fori_loop(0, nb, per_seq, 0)


def _gla_sample(qk, v, glow, wg2, bg, state):
    nbatch, seq, _ = qk.shape
    nb = min(SAMPLE_BATCH_BLOCK // 2, nbatch)
    assert nbatch % nb == 0
    st_spec = pl.BlockSpec((nb, A_HEADS, A_HEAD_DK, A_HEAD_DV), lambda b: (b, 0, 0, 0))
    tok_spec = lambda a: pl.BlockSpec((nb, seq, a.shape[2]), lambda b: (b, 0, 0))
    return pl.pallas_call(
        _gla_sample_kernel,
        grid=(nbatch // nb,),
        in_specs=[
            tok_spec(qk), tok_spec(v), tok_spec(glow),
            pl.BlockSpec(wg2.shape, lambda b: (0, 0)),
            pl.BlockSpec(bg.shape, lambda b: (0, 0)),
            st_spec,
        ],
        out_specs=[pl.BlockSpec((nb, seq, BRANCH_WIDTH), lambda b: (b, 0, 0)), st_spec],
        out_shape=[
            jax.ShapeDtypeStruct((nbatch, seq, BRANCH_WIDTH), F32),
            jax.ShapeDtypeStruct(state.shape, F32),
        ],
        compiler_params=pltpu.CompilerParams(
            dimension_semantics=("arbitrary",), vmem_limit_bytes=VMEM_LIMIT_BYTES),
        name="gla_sample",
    )(qk, v, glow, wg2, bg, state)


def _swa_pre_sample_kernel(h_ref, kvn_ref, wkv_ref, kn_ref, bd_ref, cos_ref, sin_ref, bn_ref, wq_ref, wg_ref, qn_ref,
                           k_ref, v_ref, q_ref, gate_ref):
    m = h_ref.shape[0]
    kvw = B_KV_WIDTH
    h = h_ref[...]
    cos = cos_ref[...]
    sin = sin_ref[...]
    first_half, _ = _lane_masks(m)
    bd = bd_ref[...]

    kv = _dot(_rmsnorm(h, kvn_ref[...]).astype(BF16), wkv_ref[...])
    kn = _head_norm(kv[:, :kvw], bd, kn_ref[...])
    for s in range(kvw // LANES):
        k_ref[:, s * LANES:(s + 1) * LANES] = _rope_slab(kn[:, s * LANES:(s + 1) * LANES], cos, sin, first_half)
    v_ref[...] = kv[:, kvw:]

    ub = _rmsnorm(h, bn_ref[...]).astype(BF16)
    _, lo64 = _lane_masks(m)
    qgain = qn_ref[...] * (B_HEAD_DIM ** -0.5 * LOG2E)
    gate = _dot(ub, wg_ref[...])
    q_slabs, g_slabs = [], []
    for c in range(BRANCH_WIDTH // kvw):
        qc = _head_norm(_dot(ub, wq_ref[:, c * kvw:(c + 1) * kvw]), bd, qgain)
        for s in range(kvw // LANES):
            q_slabs.append(_rope_slab(qc[:, s * LANES:(s + 1) * LANES], cos, sin, first_half))
            g_slabs.append(gate[:, c * kvw + s * LANES:c * kvw + (s + 1) * LANES])
    per_group = B_GROUP // 2
    for r in range(B_GROUP):
        for p in range(B_KV_HEADS // 2):
            a = 2 * p * per_group + r // 2
            b = (2 * p + 1) * per_group + r // 2
            lanes = slice((2 * r + p) * LANES, (2 * r + p + 1) * LANES)
            q_ref[:, lanes] = _regroup_slab(q_slabs[a], q_slabs[b], r % 2, lo64)
            gate_ref[:, lanes] = _regroup_slab(g_slabs[a], g_slabs[b], r % 2, lo64)


def _swa_pre_sample(h, kv_norm, wkv, k_norm, bd, cos, sin, b_norm, wq, wg, q_norm):
    m = h.shape[0]
    return pl.pallas_call(
        _swa_pre_sample_kernel,
        out_shape=[
            jax.ShapeDtypeStruct((m, B_KV_WIDTH), F32),
            jax.ShapeDtypeStruct((m, B_KV_WIDTH), F32),
            jax.ShapeDtypeStruct((m, BRANCH_WIDTH), F32),
            jax.ShapeDtypeStruct((m, BRANCH_WIDTH), F32),
        ],
        compiler_params=pltpu.CompilerParams(vmem_limit_bytes=VMEM_LIMIT_BYTES),
        name="swa_pre_sample",
    )(h, kv_norm, wkv, k_norm, bd, cos, sin, b_norm, wq, wg, q_norm)


def _swa_sample_kernel(q_ref, gate_ref, kn_ref, vn_ref, ck_ref, cv_ref, sink_ref, og_ref, nk_ref, nv_ref):
    nb, seq = q_ref.shape[0], q_ref.shape[1]
    kvw = B_KV_WIDTH
    trows = SUBLANES
    nrows = B_GROUP * B_KV_HEADS * trows
    rowi = lax.broadcasted_iota(jnp.int32, (nrows, 2 * WINDOW), 0)
    coli = lax.broadcasted_iota(jnp.int32, (nrows, 2 * WINDOW), 1)
    dpos = coli - rowi % trows
    mask = jnp.logical_and(dpos >= 0, dpos <= WINDOW)
    lane8 = lax.broadcasted_iota(jnp.int32, (trows, kvw), 1) // B_HEAD_DIM
    grow = (lax.broadcasted_iota(jnp.int32, (nrows, kvw), 0) // trows) % B_KV_HEADS
    gkeep = grow == lax.broadcasted_iota(jnp.int32, (nrows, kvw), 1) // B_HEAD_DIM
    sink = sink_ref[...][:, 0:1] * LOG2E
    zq = jnp.zeros((trows - seq, BRANCH_WIDTH), F32)
    lo64_q = lax.broadcasted_iota(jnp.int32, (seq, LANES), 1) < B_HEAD_DIM
    zk = jnp.zeros((WINDOW - seq, kvw), F32)

    def per_seq(i, carry):
        ck = ck_ref[i]
        cv = cv_ref[i]
        k_new = kn_ref[i]
        v_new = vn_ref[i]
        nk_ref[i, 0:WINDOW - seq, :] = ck[seq:, :]
        nk_ref[i, WINDOW - seq:WINDOW, :] = k_new
        nv_ref[i, 0:WINDOW - seq, :] = cv[seq:, :]
        nv_ref[i, WINDOW - seq:WINDOW, :] = v_new

        q8 = jnp.concatenate([q_ref[i], zq], axis=0)
        pieces = []
        for r in range(B_GROUP):
            slab = q8[:, r * kvw:(r + 1) * kvw]
            for g in range(B_KV_HEADS):
                pieces.append(jnp.where(lane8 == g, slab, 0.0))
        qrows = jnp.concatenate(pieces, axis=0).astype(BF16)
        kpad = jnp.concatenate([k_new, zk], axis=0).astype(BF16)
        vpad = jnp.concatenate([v_new, zk], axis=0).astype(BF16)
        s = jnp.concatenate([_dot_nt(qrows, ck.astype(BF16)), _dot_nt(qrows, kpad)], axis=1)
        s = jnp.where(mask, s, -jnp.inf)
        m = jnp.maximum(jnp.max(s, axis=-1, keepdims=True), sink)
        p = jnp.exp2(s - m)
        den = jnp.sum(p, axis=-1, keepdims=True) + jnp.exp2(sink - m)
        pb = p.astype(BF16)
        o = _dot(pb[:, :WINDOW], cv.astype(BF16)) + _dot(pb[:, WINDOW:], vpad)
        o = jnp.where(gkeep, o / den, 0.0)
        outs = []
        for r in range(B_GROUP):
            acc = o[r * B_KV_HEADS * trows:r * B_KV_HEADS * trows + trows]
            for g in range(1, B_KV_HEADS):
                acc = acc + o[(r * B_KV_HEADS + g) * trows:(r * B_KV_HEADS + g + 1) * trows]
            outs.append(acc)
        gated = jnp.concatenate(outs, axis=1)[0:seq] * _silu(gate_ref[i])
        slabs = []
        for g in range(B_KV_HEADS):
            for j in range(B_GROUP // 2):
                a = gated[:, (4 * j + g // 2) * LANES:(4 * j + g // 2 + 1) * LANES]
                b = gated[:, (4 * j + 2 + g // 2) * LANES:(4 * j + 2 + g // 2 + 1) * LANES]
                slabs.append(_regroup_slab(a, b, g % 2, lo64_q))
        og_ref[i] = jnp.concatenate(slabs, axis=1)
        return carry

    lax.fori_loop(0, nb, per_seq, 0, unroll=4)


def _swa_sample(q, gate, k_new, v_new, ck, cv, sink_rows):
    nbatch, seq, _ = q.shape
    nb = min(SAMPLE_BATCH_BLOCK, nbatch)
    assert nbatch % nb == 0 and ck.shape[1] == WINDOW
    cache_spec = pl.BlockSpec((nb, WINDOW, B_KV_WIDTH), lambda b: (b, 0, 0))
    wide_spec = pl.BlockSpec((nb, seq, BRANCH_WIDTH), lambda b: (b, 0, 0))
    new_spec = pl.BlockSpec((nb, seq, B_KV_WIDTH), lambda b: (b, 0, 0))
    return pl.pallas_call(
        _swa_sample_kernel,
        grid=(nbatch // nb,),
        in_specs=[wide_spec, wide_spec, new_spec, new_spec, cache_spec, cache_spec,
                  pl.BlockSpec(sink_rows.shape, lambda b: (0, 0))],
        out_specs=[wide_spec, cache_spec, cache_spec],
        out_shape=[
            jax.ShapeDtypeStruct((nbatch, seq, BRANCH_WIDTH), F32),
            jax.ShapeDtypeStruct(ck.shape, F32),
            jax.ShapeDtypeStruct(cv.shape, F32),
        ],
        compiler_params=pltpu.CompilerParams(
            dimension_semantics=("arbitrary",), vmem_limit_bytes=VMEM_LIMIT_BYTES),
        name="swa_sample",
    )(q, gate, k_new, v_new, ck, cv, sink_rows)


def _rope_tables(first_pos, n_pos, repeat=1):
    half = B_HEAD_DIM // 2
    inv_freq = ROPE_THETA ** (-np.arange(half, dtype=np.float64) / half)
    ang = (first_pos + np.arange(n_pos, dtype=np.float64))[:, None] * inv_freq[None, :]
    cos, sin = np.cos(ang), np.sin(ang)
    cos_t = np.tile(np.concatenate([cos, cos, cos, cos], axis=1), (repeat, 1))
    sin_t = np.tile(np.concatenate([-sin, sin, -sin, sin], axis=1), (repeat, 1))
    return jnp.asarray(cos_t, F32), jnp.asarray(sin_t, F32)


def kernel(x_prompt, x_sample, state_gla, cache_swa_k, cache_swa_v, a_norm, a_w_in, a_w_gate2, a_b_gate,
           a_out_norm, a_w_out, kv_norm, w_k, w_v, k_norm, b_norm, b_w_in, b_q_norm, b_sinks, b_w_out):
    assert a_norm.shape[0] == 1 and b_norm.shape[0] == 1
    bsz_p, seq_p, d = x_prompt.shape
    bsz_s, seq_s, _ = x_sample.shape
    kw, bw = A_KEY_WIDTH, BRANCH_WIDTH

    w_in = a_w_in[0]
    wa_qk = w_in[:, :2 * kw].astype(BF16)
    wa_v = w_in[:, 2 * kw:2 * kw + bw].astype(BF16)
    wa_g = w_in[:, 2 * kw + bw:2 * kw + 2 * bw].astype(BF16)
    wa_low = jnp.pad(w_in[:, 2 * kw + 2 * bw:].astype(BF16), ((0, 0), (0, LANES - A_GATE_RANK)))
    wa_g2 = jnp.pad(a_w_gate2[0].astype(BF16), ((0, LANES - A_GATE_RANK), (0, 0)))
    a_bg = a_b_gate[0][None, :]
    a_n = a_norm[0][None, :]
    a_on = a_out_norm[0][None, :]
    wa_out = a_w_out[0].astype(BF16)
    w_kv = jnp.concatenate([w_k.astype(BF16), w_v.astype(BF16)], axis=1)
    wb_q = b_w_in[0][:, :bw].astype(BF16)
    wb_g = b_w_in[0][:, bw:].astype(BF16)
    wb_out = b_w_out[0].astype(BF16)
    kvn = kv_norm[None, :]
    bn = b_norm[0][None, :]
    kn_t = jnp.tile(k_norm, B_KV_WIDTH // B_HEAD_DIM)[None, :]
    qn_t = jnp.tile(b_q_norm[0], B_KV_WIDTH // B_HEAD_DIM)[None, :]
    sinks = b_sinks[0]
    grp = np.arange(B_KV_WIDTH) // B_HEAD_DIM
    bd = jnp.asarray(grp[:, None] == grp[None, :], BF16)

    cos_p, sin_p = _rope_tables(0, seq_p)
    h1_p, st_p = _gla_prompt(x_prompt, a_n, wa_qk, wa_v, wa_g, wa_low, wa_g2, a_bg, a_on, wa_out)
    y_p, kc_p, vc_p = _swa_prompt(h1_p, kvn, w_kv, kn_t, bd, cos_p, sin_p, bn, wb_q, wb_g, qn_t, sinks, wb_out)

    m = bsz_s * seq_s
    xs = x_sample.reshape(m, d)
    qk_a, v_a, gate_a, glow_a = _gla_proj_sample(xs, a_n, wa_qk, wa_v, wa_g, wa_low)
    o_s, st_s = _gla_sample(qk_a.reshape(bsz_s, seq_s, -1), v_a.reshape(bsz_s, seq_s, -1),
                            glow_a.reshape(bsz_s, seq_s, -1), wa_g2, a_bg, state_gla[0])
    h1_s = _gla_out(o_s.reshape(m, bw), gate_a, xs, a_on, wa_out)

    sink_rows = jnp.broadcast_to(
        sinks.reshape(B_KV_HEADS, B_GROUP).T[:, :, None, None], (B_GROUP, B_KV_HEADS, SUBLANES, LANES)
    ).reshape(B_GROUP * B_KV_HEADS * SUBLANES, LANES)
    cos_s, sin_s = _rope_tables(PAST_LEN, seq_s, repeat=bsz_s)
    k_new, v_new, q_s, gate_b = _swa_pre_sample(h1_s, kvn, w_kv, kn_t, bd, cos_s, sin_s, bn, wb_q, wb_g, qn_t)
    og_s, nk_s, nv_s = _swa_sample(
        q_s.reshape(bsz_s, seq_s, bw), gate_b.reshape(bsz_s, seq_s, bw),
        k_new.reshape(bsz_s, seq_s, B_KV_WIDTH), v_new.reshape(bsz_s, seq_s, B_KV_WIDTH),
        cache_swa_k.reshape(bsz_s, WINDOW, B_KV_WIDTH), cache_swa_v.reshape(bsz_s, WINDOW, B_KV_WIDTH), sink_rows)
    y_s = _out_proj(og_s.reshape(m, bw), h1_s, wb_out)

    return (y_p, y_s.reshape(bsz_s, seq_s, d),
            st_p[None], st_s[None],
            kc_p.reshape(bsz_p, WINDOW, B_KV_HEADS, B_HEAD_DIM), vc_p.reshape(bsz_p, WINDOW, B_KV_HEADS, B_HEAD_DIM),
            nk_s.reshape(bsz_s, WINDOW, B_KV_HEADS, B_HEAD_DIM), nv_s.reshape(bsz_s, WINDOW, B_KV_HEADS, B_HEAD_DIM))
```

```python
import jax
import jax.numpy as jnp
import numpy as np
from jax import lax
from jax.experimental import pallas as pl
from jax.experimental.pallas import tpu as pltpu

F32 = jnp.float32
BF16 = jnp.bfloat16

A_HEADS = 4
A_HEAD_DK = 128
A_HEAD_DV = 512
A_KEY_WIDTH = A_HEADS * A_HEAD_DK
BRANCH_WIDTH = A_HEADS * A_HEAD_DV
A_GATE_RANK = 16
A_GATE_NORMALIZER = 16.0
B_HEAD_DIM = 64
B_HEADS = BRANCH_WIDTH // B_HEAD_DIM
B_KV_HEADS = 4
B_GROUP = B_HEADS // B_KV_HEADS
B_KV_WIDTH = B_KV_HEADS * B_HEAD_DIM
WINDOW = 128
ROPE_THETA = 10000.0
RMS_EPS = 1e-6
PAST_LEN = 16384

LANES = 128
SUBLANES = 8
VMEM_LIMIT_BYTES = 56 * 1024 * 1024

PROMPT_TILE = 512
GLA_CHUNK = 256
GLA_SAFE_DECAY = 80.0
SAMPLE_BATCH_BLOCK = 8
OUT_PROJ_CHUNKS = 4
LOG2E = 1.4426950408889634


def _dot(a, b):
    return jnp.dot(a, b, preferred_element_type=F32)


def _dot_nt(a, b):
    return lax.dot_general(a, b, (((1,), (1,)), ((), ())), preferred_element_type=F32)


def _split2(x):
    hi = x.astype(BF16)
    lo = (x - hi.astype(F32)).astype(BF16)
    return hi, lo


def _rmsnorm(x, g):
    ms = jnp.mean(x * x, axis=-1, keepdims=True)
    return x * lax.rsqrt(ms + RMS_EPS) * g


def _log_sigmoid(x):
    return jnp.minimum(x, 0.0) - jnp.log1p(jnp.exp(-jnp.abs(x)))


def _silu(x):
    return x * (1.0 / (1.0 + jnp.exp(-x)))


def _head_norm(x, bd, gain):
    hi, lo = _split2(x * x)
    ss = _dot(hi, bd) + _dot(lo, bd)
    return x * lax.rsqrt(ss * (1.0 / B_HEAD_DIM) + RMS_EPS) * gain


def _rope_slab(x, cos, sin_signed, first_half):
    xr = jnp.where(first_half, pltpu.roll(x, 96, 1), pltpu.roll(x, 32, 1))
    return x * cos + xr * sin_signed


def _lane_masks(rows):
    lane = lax.broadcasted_iota(jnp.int32, (rows, LANES), 1)
    first_half = (lane % B_HEAD_DIM) < (B_HEAD_DIM // 2)
    lo64 = lane < B_HEAD_DIM
    return first_half, lo64


def _split_group(slab, g, lo64):
    swapped = pltpu.roll(slab, B_HEAD_DIM, 1)
    zero = jnp.zeros_like(slab)
    if g % 2 == 0:
        return jnp.where(lo64, slab, zero), jnp.where(lo64, zero, swapped)
    return jnp.where(lo64, swapped, zero), jnp.where(lo64, zero, slab)


def _regroup_slab(a, b, odd, lo64):
    if odd:
        return jnp.where(lo64, pltpu.roll(a, B_HEAD_DIM, 1), b)
    return jnp.where(lo64, a, pltpu.roll(b, B_HEAD_DIM, 1))


def _gla_prompt_kernel(x_ref, an_ref, wqk_ref, wv_ref, wg_ref, wlow_ref, wg2_ref, bg_ref, on_ref, wout_ref,
                       h1_ref, st_ref,
                       qk_s, v_s, gate_s, b_s, o_s, oi_s):
    tile = x_ref.shape[0]
    chunk = GLA_CHUNK
    n_chunks = tile // chunk
    dk, dv, kw = A_HEAD_DK, A_HEAD_DV, A_KEY_WIDTH

    @pl.when(pl.program_id(1) == 0)
    def _():
        st_ref[...] = jnp.zeros_like(st_ref)

    x = x_ref[...]
    u = _rmsnorm(x, an_ref[...]).astype(BF16)
    qk_s[...] = _dot(u, wqk_ref[...])
    v_s[...] = _dot(u, wv_ref[...])
    gate_s[...] = _dot(u, wg_ref[...])
    glow = _dot(u, wlow_ref[...]).astype(BF16)
    pre = _dot(glow, wg2_ref[...]) + bg_ref[...]
    loga = _log_sigmoid(pre) * (1.0 / A_GATE_NORMALIZER)

    row = lax.broadcasted_iota(jnp.int32, (chunk, chunk), 0)
    col = lax.broadcasted_iota(jnp.int32, (chunk, chunk), 1)
    lower = row >= col
    tri = jnp.where(lower, 1.0, 0.0).astype(BF16)
    for c in range(n_chunks):
        hi, lo = _split2(loga[c * chunk:(c + 1) * chunk])
        b_s[c * chunk:(c + 1) * chunk, :] = _dot(tri, hi) + _dot(tri, lo)
    safe = jnp.min(b_s[...]) >= -GLA_SAFE_DECAY

    scale = A_HEAD_DK ** -0.5
    for c in range(n_chunks):
        rows = slice(c * chunk, (c + 1) * chunk)
        b = b_s[rows, :]
        eb = jnp.exp(b)
        enb = jnp.exp(-b)
        blast = b[chunk - 1:chunk, :]
        ekl = jnp.exp(blast - b)
        eblast = jnp.exp(blast)
        for h in range(A_HEADS):
            ks = slice(h * dk, (h + 1) * dk)
            vs = slice(h * dv, (h + 1) * dv)
            q = qk_s[rows, ks] * scale
            k = qk_s[rows, kw + h * dk: kw + (h + 1) * dk]
            vh = v_s[rows, vs].astype(BF16)
            state = st_ref[h]
            qt = (q * eb[:, ks]).astype(BF16)
            o_inter = _dot(qt, state.astype(BF16))
            a = _dot_nt(qt, (k * enb[:, ks]).astype(BF16))
            a = jnp.where(lower, a, 0.0).astype(BF16)
            oi_s[rows, vs] = o_inter
            o_s[rows, vs] = o_inter + _dot(a, vh)
            kd_t = jnp.transpose(k * ekl[:, ks]).astype(BF16)
            dec = jnp.transpose(jnp.broadcast_to(eblast[:, ks], (dk, dk)))
            st_ref[h] = state * jnp.concatenate([dec] * (dv // dk), axis=1) + _dot(kd_t, vh)

    @pl.when(jnp.logical_not(safe))
    def _():
        trow = lax.broadcasted_iota(jnp.int32, (chunk, dk), 0)
        sub8 = lax.broadcasted_iota(jnp.int32, (SUBLANES, 1), 0)
        for c in range(n_chunks):
            rows = slice(c * chunk, (c + 1) * chunk)
            for h in range(A_HEADS):
                ks = slice(h * dk, (h + 1) * dk)
                vs = slice(h * dv, (h + 1) * dv)
                q = qk_s[rows, ks] * scale
                b = b_s[rows, ks]

                def body(s, acc, c=c, h=h, q=q, b=b):
                    r8 = pl.multiple_of(c * chunk + (s // SUBLANES) * SUBLANES, SUBLANES)
                    pick = sub8 == s % SUBLANES

                    def row_of(ref, lanes):
                        return jnp.sum(jnp.where(pick, ref[pl.ds(r8, SUBLANES), lanes], 0.0), axis=0, keepdims=True)

                    brow = row_of(b_s, slice(h * dk, (h + 1) * dk))
                    krow = row_of(qk_s, slice(kw + h * dk, kw + (h + 1) * dk))
                    vrow = row_of(v_s, slice(h * dv, (h + 1) * dv))
                    w = jnp.where(trow >= s, jnp.exp(jnp.minimum(b - brow, 0.0)), 0.0)
                    colv = jnp.sum(q * krow * w, axis=-1, keepdims=True)
                    return acc + colv * vrow

                o_s[rows, vs] = oi_s[rows, vs] + lax.fori_loop(0, chunk, body, jnp.zeros((chunk, dv), F32))

    y = x
    for h in range(A_HEADS):
        vs = slice(h * dv, (h + 1) * dv)
        o = _rmsnorm(o_s[:, vs], on_ref[...])
        og = (o * _silu(gate_s[:, vs])).astype(BF16)
        y = y + _dot(og, wout_ref[vs, :])
    h1_ref[...] = y


def _gla_prompt(x, a_norm, wqk, wv, wg, wlow, wg2, bg, out_norm, wout):
    bsz, seq, d = x.shape
    tile = min(PROMPT_TILE, seq)
    assert seq % tile == 0 and tile % GLA_CHUNK == 0
    const = lambda shape: pl.BlockSpec(shape, lambda b, l: (0,) * len(shape), pipeline_mode=pl.Buffered(1))
    return pl.pallas_call(
        _gla_prompt_kernel,
        grid=(bsz, seq // tile),
        in_specs=[
            pl.BlockSpec((None, tile, d), lambda b, l: (b, l, 0)),
            const(a_norm.shape), const(wqk.shape), const(wv.shape), const(wg.shape), const(wlow.shape),
            const(wg2.shape), const(bg.shape), const(out_norm.shape), const(wout.shape),
        ],
        out_specs=[
            pl.BlockSpec((None, tile, d), lambda b, l: (b, l, 0)),
            pl.BlockSpec((None, A_HEADS, A_HEAD_DK, A_HEAD_DV), lambda b, l: (b, 0, 0, 0)),
        ],
        out_shape=[
            jax.ShapeDtypeStruct((bsz, seq, d), F32),
            jax.ShapeDtypeStruct((bsz, A_HEADS, A_HEAD_DK, A_HEAD_DV), F32),
        ],
        scratch_shapes=[
            pltpu.VMEM((tile, 2 * A_KEY_WIDTH), F32),
            pltpu.VMEM((tile, BRANCH_WIDTH), F32),
            pltpu.VMEM((tile, BRANCH_WIDTH), F32),
            pltpu.VMEM((tile, A_KEY_WIDTH), F32),
            pltpu.VMEM((tile, BRANCH_WIDTH), F32),
            pltpu.VMEM((tile, BRANCH_WIDTH), F32),
        ],
        compiler_params=pltpu.CompilerParams(
            dimension_semantics=("arbitrary", "arbitrary"), vmem_limit_bytes=VMEM_LIMIT_BYTES),
        name="gla_prompt",
    )(x, a_norm, wqk, wv, wg, wlow, wg2, bg, out_norm, wout)


def _swa_prompt_kernel(h_ref, kvn_ref, wkv_ref, kn_ref, bd_ref, cos_ref, sin_ref, bn_ref, wq_ref, wg_ref,
                       qn_ref, sink_ref, wout_ref,
                       y_ref, kc_ref, vc_ref,
                       ktop_s, kbot_s, vtop_s, vbot_s, qb_s, ub_s, kv_s, q_s, gsl_s, osl_s, og_s):
    tile = h_ref.shape[0]
    l = pl.program_id(1)
    blk = WINDOW
    kvw = B_KV_WIDTH
    pairs = B_GROUP // 2
    n_slabs = BRANCH_WIDTH // LANES
    n_iter = (tile // blk) * B_KV_HEADS
    gate_cols = wg_ref.shape[2]

    @pl.when(l == 0)
    def _():
        zeros = jnp.zeros((B_KV_HEADS, WINDOW, LANES), BF16)
        ktop_s[:, 0:WINDOW, :] = zeros
        kbot_s[:, 0:WINDOW, :] = zeros
        vtop_s[:, 0:WINDOW, 0:LANES] = zeros
        vbot_s[:, 0:WINDOW, 0:LANES] = zeros
        lane = lax.broadcasted_iota(jnp.int32, (B_KV_HEADS, WINDOW + tile, LANES), 2)
        vtop_s[:, :, LANES:] = jnp.where(lane < B_HEAD_DIM, 1.0, 0.0).astype(BF16)
        vbot_s[:, :, LANES:] = jnp.where(lane < B_HEAD_DIM, 0.0, 1.0).astype(BF16)

    h = h_ref[...]
    cos = cos_ref[...]
    sin = sin_ref[...]
    first_half, lo64 = _lane_masks(tile)
    bd = bd_ref[...]

    kv_s[...] = _dot(_rmsnorm(h, kvn_ref[...]).astype(BF16), wkv_ref[...])
    ub_s[...] = _rmsnorm(h, bn_ref[...]).astype(BF16)

    kn = _head_norm(kv_s[:, :kvw], bd, kn_ref[...])
    v = kv_s[:, kvw:]
    k_slabs = [_rope_slab(kn[:, s * LANES:(s + 1) * LANES], cos, sin, first_half) for s in range(kvw // LANES)]

    @pl.when(l == pl.num_programs(1) - 1)
    def _():
        kc_ref[...] = jnp.concatenate(k_slabs, axis=1)[tile - WINDOW:, :]
        vc_ref[...] = v[tile - WINDOW:, :]

    for g in range(B_KV_HEADS):
        s = g // 2
        top, bot = _split_group(k_slabs[s], g, lo64)
        ktop_s[g, WINDOW:WINDOW + tile, :] = top.astype(BF16)
        kbot_s[g, WINDOW:WINDOW + tile, :] = bot.astype(BF16)
        top, bot = _split_group(v[:, s * LANES:(s + 1) * LANES], g, lo64)
        vtop_s[g, WINDOW:WINDOW + tile, 0:LANES] = top.astype(BF16)
        vbot_s[g, WINDOW:WINDOW + tile, 0:LANES] = bot.astype(BF16)

    qscale = qn_ref[...] * (B_HEAD_DIM ** -0.5 * LOG2E)
    qcos = [cos * qscale[:, s * LANES:(s + 1) * LANES] for s in range(kvw // LANES)]
    qsin = [sin * jnp.where(first_half[0:1], pltpu.roll(qscale[:, s * LANES:(s + 1) * LANES], 96, 1),
                            pltpu.roll(qscale[:, s * LANES:(s + 1) * LANES], 32, 1)) for s in range(kvw // LANES)]

    q_s[...] = _dot(ub_s[...], wq_ref[...])
    for c in range(BRANCH_WIDTH // kvw):
        x = q_s[:, c * kvw:(c + 1) * kvw]
        hi, lo = _split2(x * x)
        rinv = lax.rsqrt((_dot(hi, bd) + _dot(lo, bd)) * (1.0 / B_HEAD_DIM) + RMS_EPS)
        for s in range(kvw // LANES):
            xs = x[:, s * LANES:(s + 1) * LANES]
            xr = jnp.where(first_half, pltpu.roll(xs, 96, 1), pltpu.roll(xs, 32, 1))
            qr = ((xs * qcos[s] + xr * qsin[s]) * rinv[:, s * LANES:(s + 1) * LANES]).astype(BF16)
            for i in range(tile // blk):
                qb_s[i, c * (kvw // LANES) + s] = qr[i * blk:(i + 1) * blk]

    rowi = lax.broadcasted_iota(jnp.int32, (blk, 2 * blk), 0)
    coli = lax.broadcasted_iota(jnp.int32, (blk, 2 * blk), 1)
    band = jnp.logical_and(coli - rowi >= 0, coli - rowi <= WINDOW)
    _, lo64_b = _lane_masks(blk)

    def attn_pair(idx):
        i = idx // B_KV_HEADS
        g = idx % B_KV_HEADS
        r0 = pl.multiple_of(i * blk, blk)
        first_col = jnp.where(l * tile + r0 > 0, 0, WINDOW)
        mask = jnp.logical_and(band, coli >= first_col)
        kpad = jnp.concatenate([ktop_s[g, pl.ds(r0, 2 * blk), :], kbot_s[g, pl.ds(r0, 2 * blk), :]], axis=0)
        vpad = jnp.concatenate([vtop_s[g, pl.ds(r0, 2 * blk), :], vbot_s[g, pl.ds(r0, 2 * blk), :]], axis=0)
        qg = qb_s[i, pl.ds(g * pairs, pairs)].reshape(pairs * blk, LANES)
        s = _dot_nt(qg, kpad)
        ps, esinks = [], []
        for j in range(pairs):
            pj, ej = [], []
            for half in range(2):
                sink = sink_ref[g * B_GROUP + 2 * j + half] * LOG2E
                sh = jnp.where(mask, s[j * blk:(j + 1) * blk, half * 2 * blk:(half + 1) * 2 * blk], -jnp.inf)
                m = jnp.maximum(jnp.max(sh, axis=-1, keepdims=True), sink)
                pj.append(jnp.exp2(sh - m).astype(BF16))
                ej.append(jnp.exp2(sink - m))
            ps.append(jnp.concatenate(pj, axis=1))
            esinks.append(jnp.where(lo64_b, ej[0], ej[1]))
        o = _dot(jnp.concatenate(ps, axis=0), vpad)
        for j in range(pairs):
            oj = o[j * blk:(j + 1) * blk]
            osl_s[g * pairs + j, pl.ds(r0, blk), :] = oj[:, :LANES] / (oj[:, LANES:] + esinks[j])

    def attn_body(k, carry):
        gc = _dot(ub_s[...], wg_ref[k])
        for s in range(gate_cols // LANES):
            gsl_s[k * (gate_cols // LANES) + s] = gc[:, s * LANES:(s + 1) * LANES]
        attn_pair(2 * k)
        attn_pair(2 * k + 1)
        return carry

    lax.fori_loop(0, n_iter // 2, attn_body, 0, unroll=2)

    for buf in (ktop_s, kbot_s):
        buf[:, 0:WINDOW, :] = buf[:, tile:tile + WINDOW, :]
    for buf in (vtop_s, vbot_s):
        buf[:, 0:WINDOW, 0:LANES] = buf[:, tile:tile + WINDOW, 0:LANES]

    y = h
    per = n_slabs // OUT_PROJ_CHUNKS
    for c in range(OUT_PROJ_CHUNKS):
        for sl in range(c * per, (c + 1) * per):
            og_s[:, sl * LANES:(sl + 1) * LANES] = (osl_s[sl] * _silu(gsl_s[sl])).astype(BF16)
        cols = slice(c * per * LANES, (c + 1) * per * LANES)
        y = y + _dot(og_s[:, cols], wout_ref[cols, :])
    y_ref[...] = y


def _swa_prompt(h, kv_norm, wkv, k_norm, bd, cos, sin, b_norm, wq, wg, q_norm, sinks, wout):
    bsz, seq, d = h.shape
    tile = min(PROMPT_TILE, seq)
    assert seq % tile == 0 and tile % WINDOW == 0 and seq >= WINDOW
    n_iter = (tile // WINDOW) * B_KV_HEADS
    gate_cols = BRANCH_WIDTH // (n_iter // 2)
    assert gate_cols % LANES == 0
    wg = wg.reshape(d, n_iter // 2, gate_cols).transpose(1, 0, 2)
    const = lambda shape: pl.BlockSpec(shape, lambda b, l: (0,) * len(shape), pipeline_mode=pl.Buffered(1))
    k_scratch = pltpu.VMEM((B_KV_HEADS, WINDOW + tile, LANES), BF16)
    v_scratch = pltpu.VMEM((B_KV_HEADS, WINDOW + tile, 2 * LANES), BF16)
    n_slabs = BRANCH_WIDTH // LANES
    return pl.pallas_call(
        _swa_prompt_kernel,
        grid=(bsz, seq // tile),
        in_specs=[
            pl.BlockSpec((None, tile, d), lambda b, l: (b, l, 0)),
            const(kv_norm.shape), const(wkv.shape), const(k_norm.shape), const(bd.shape),
            pl.BlockSpec((tile, LANES), lambda b, l: (l, 0)),
            pl.BlockSpec((tile, LANES), lambda b, l: (l, 0)),
            const(b_norm.shape), const(wq.shape), const(wg.shape), const(q_norm.shape),
            pl.BlockSpec(memory_space=pltpu.SMEM),
            const(wout.shape),
        ],
        out_specs=[
            pl.BlockSpec((None, tile, d), lambda b, l: (b, l, 0)),
            pl.BlockSpec((None, WINDOW, B_KV_WIDTH), lambda b, l: (b, 0, 0)),
            pl.BlockSpec((None, WINDOW, B_KV_WIDTH), lambda b, l: (b, 0, 0)),
        ],
        out_shape=[
            jax.ShapeDtypeStruct((bsz, seq, d), F32),
            jax.ShapeDtypeStruct((bsz, WINDOW, B_KV_WIDTH), F32),
            jax.ShapeDtypeStruct((bsz, WINDOW, B_KV_WIDTH), F32),
        ],
        scratch_shapes=[
            k_scratch, k_scratch, v_scratch, v_scratch,
            pltpu.VMEM((tile // WINDOW, n_slabs, WINDOW, LANES), BF16),
            pltpu.VMEM((tile, d), BF16),
            pltpu.VMEM((tile, 2 * B_KV_WIDTH), F32),
            pltpu.VMEM((tile, BRANCH_WIDTH), F32),
            pltpu.VMEM((n_slabs, tile, LANES), F32),
            pltpu.VMEM((n_slabs, tile, LANES), F32),
            pltpu.VMEM((tile, BRANCH_WIDTH), BF16),
        ],
        compiler_params=pltpu.CompilerParams(
            dimension_semantics=("arbitrary", "arbitrary"), vmem_limit_bytes=VMEM_LIMIT_BYTES),
        name="swa_prompt",
    )(h, kv_norm, wkv, k_norm, bd, cos, sin, b_norm, wq, wg, q_norm, sinks, wout)


def _gla_proj_sample_kernel(x_ref, g_ref, wqk_ref, wv_ref, wg_ref, wlow_ref, qk_ref, v_ref, gate_ref, glow_ref):
    u = _rmsnorm(x_ref[...], g_ref[...]).astype(BF16)
    qk_ref[...] = _dot(u, wqk_ref[...])
    v_ref[...] = _dot(u, wv_ref[...])
    gate_ref[...] = _dot(u, wg_ref[...])
    glow_ref[...] = _dot(u, wlow_ref[...])


def _gla_proj_sample(x, g, wqk, wv, wg, wlow):
    m = x.shape[0]
    return pl.pallas_call(
        _gla_proj_sample_kernel,
        out_shape=[jax.ShapeDtypeStruct((m, w.shape[1]), F32) for w in (wqk, wv, wg, wlow)],
        compiler_params=pltpu.CompilerParams(vmem_limit_bytes=VMEM_LIMIT_BYTES),
        name="gla_proj_sample",
    )(x, g, wqk, wv, wg, wlow)


def _gla_out_kernel(o_ref, gate_ref, x_ref, on_ref, w_ref, y_ref, og_s):
    for h in range(A_HEADS):
        vs = slice(h * A_HEAD_DV, (h + 1) * A_HEAD_DV)
        o = _rmsnorm(o_ref[:, vs], on_ref[...])
        og_s[:, vs] = (o * _silu(gate_ref[:, vs])).astype(BF16)
    y_ref[...] = _dot(og_s[...], w_ref[...]) + x_ref[...]


def _gla_out(o, gate, x, out_norm, wout):
    m = o.shape[0]
    return pl.pallas_call(
        _gla_out_kernel,
        out_shape=jax.ShapeDtypeStruct(x.shape, F32),
        scratch_shapes=[pltpu.VMEM((m, BRANCH_WIDTH), BF16)],
        compiler_params=pltpu.CompilerParams(vmem_limit_bytes=VMEM_LIMIT_BYTES),
        name="gla_out_sample",
    )(o, gate, x, out_norm, wout)


def _out_proj_kernel(og_ref, x_ref, w_ref, y_ref):
    y_ref[...] = _dot(og_ref[...].astype(BF16), w_ref[...]) + x_ref[...]


def _out_proj(og, x, wout):
    return pl.pallas_call(
        _out_proj_kernel,
        out_shape=jax.ShapeDtypeStruct(x.shape, F32),
        compiler_params=pltpu.CompilerParams(vmem_limit_bytes=VMEM_LIMIT_BYTES),
        name="swa_out_sample",
    )(og, x, wout)


def _gla_sample_kernel(qk_ref, v_ref, glow_ref, wg2_ref, bg_ref, st_ref, o_ref, nst_ref):
    nb, seq = qk_ref.shape[0], qk_ref.shape[1]
    dk, dv, kw = A_HEAD_DK, A_HEAD_DV, A_KEY_WIDTH
    scale = A_HEAD_DK ** -0.5
    pad_rows = SUBLANES - seq

    def per_seq(i, carry):
        qk = qk_ref[i]
        q = qk[:, 0:kw] * scale
        k = qk[:, kw:2 * kw]
        v = v_ref[i]
        glow = glow_ref[i]
        glow8 = jnp.concatenate([glow, jnp.zeros((pad_rows, LANES), F32)], axis=0).astype(BF16)
        pre = _dot(glow8, wg2_ref[...])[0:seq] + bg_ref[...]
        loga = _log_sigmoid(pre) * (1.0 / A_GATE_NORMALIZER)
        bs = [loga[0:1]]
        for t in range(1, seq):
            bs.append(bs[-1] + loga[t:t + 1])
        bmat = jnp.concatenate(bs, axis=0)
        blast = bs[-1]
        qt = q * jnp.exp(bmat)
        kd = k * jnp.exp(blast - bmat)
        eblast = jnp.exp(blast)

        outs = [[jnp.zeros((1, dv), F32) for _ in range(A_HEADS)] for _ in range(seq)]
        for t in range(seq):
            for s in range(t + 1):
                w = q[t:t + 1] * k[s:s + 1] * jnp.exp(bs[t] - bs[s])
                for h in range(A_HEADS):
                    a = jnp.sum(w[:, h * dk:(h + 1) * dk], axis=-1, keepdims=True)
                    outs[t][h] = outs[t][h] + a * v[s:s + 1, h * dv:(h + 1) * dv]
        intra = jnp.concatenate([jnp.concatenate(outs[t], axis=1) for t in range(seq)], axis=0)

        qt8 = jnp.concatenate([qt, jnp.zeros((pad_rows, kw), F32)], axis=0).astype(BF16)
        inter = []
        for h in range(A_HEADS):
            ks = slice(h * dk, (h + 1) * dk)
            state = st_ref[i, h]
            inter.append(_dot(qt8[:, ks], state.astype(BF16))[0:seq])
            m = jnp.concatenate([kd[:, ks], eblast[:, ks], jnp.zeros((dk - seq - 1, dk), F32)], axis=0)
            mt = jnp.transpose(m)
            vpad = jnp.concatenate([v[:, h * dv:(h + 1) * dv], jnp.zeros((dk - seq, dv), F32)], axis=0)
            nst_ref[i, h] = state * mt[:, seq:seq + 1] + _dot(mt.astype(BF16), vpad.astype(BF16))
        o_ref[i] = intra + jnp.concatenate(inter, axis=1)
        return carry

    lax.fori_loop(0, nb, per_seq, 0)


def _gla_sample(qk, v, glow, wg2, bg, state):
    nbatch, seq, _ = qk.shape
    nb = min(SAMPLE_BATCH_BLOCK // 2, nbatch)
    assert nbatch % nb == 0
    st_spec = pl.BlockSpec((nb, A_HEADS, A_HEAD_DK, A_HEAD_DV), lambda b: (b, 0, 0, 0))
    tok_spec = lambda a: pl.BlockSpec((nb, seq, a.shape[2]), lambda b: (b, 0, 0))
    return pl.pallas_call(
        _gla_sample_kernel,
        grid=(nbatch // nb,),
        in_specs=[
            tok_spec(qk), tok_spec(v), tok_spec(glow),
            pl.BlockSpec(wg2.shape, lambda b: (0, 0)),
            pl.BlockSpec(bg.shape, lambda b: (0, 0)),
            st_spec,
        ],
        out_specs=[pl.BlockSpec((nb, seq, BRANCH_WIDTH), lambda b: (b, 0, 0)), st_spec],
        out_shape=[
            jax.ShapeDtypeStruct((nbatch, seq, BRANCH_WIDTH), F32),
            jax.ShapeDtypeStruct(state.shape, F32),
        ],
        compiler_params=pltpu.CompilerParams(
            dimension_semantics=("arbitrary",), vmem_limit_bytes=VMEM_LIMIT_BYTES),
        name="gla_sample",
    )(qk, v, glow, wg2, bg, state)


def _swa_pre_sample_kernel(h_ref, kvn_ref, wkv_ref, kn_ref, bd_ref, cos_ref, sin_ref, bn_ref, wq_ref, wg_ref, qn_ref,
                           k_ref, v_ref, q_ref, gate_ref):
    m = h_ref.shape[0]
    kvw = B_KV_WIDTH
    h = h_ref[...]
    cos = cos_ref[...]
    sin = sin_ref[...]
    first_half, _ = _lane_masks(m)
    bd = bd_ref[...]

    kv = _dot(_rmsnorm(h, kvn_ref[...]).astype(BF16), wkv_ref[...])
    kn = _head_norm(kv[:, :kvw], bd, kn_ref[...])
    for s in range(kvw // LANES):
        k_ref[:, s * LANES:(s + 1) * LANES] = _rope_slab(kn[:, s * LANES:(s + 1) * LANES], cos, sin, first_half)
    v_ref[...] = kv[:, kvw:]

    ub = _rmsnorm(h, bn_ref[...]).astype(BF16)
    _, lo64 = _lane_masks(m)
    qgain = qn_ref[...] * (B_HEAD_DIM ** -0.5 * LOG2E)
    gate = _dot(ub, wg_ref[...])
    q_slabs, g_slabs = [], []
    for c in range(BRANCH_WIDTH // kvw):
        qc = _head_norm(_dot(ub, wq_ref[:, c * kvw:(c + 1) * kvw]), bd, qgain)
        for s in range(kvw // LANES):
            q_slabs.append(_rope_slab(qc[:, s * LANES:(s + 1) * LANES], cos, sin, first_half))
            g_slabs.append(gate[:, c * kvw + s * LANES:c * kvw + (s + 1) * LANES])
    per_group = B_GROUP // 2
    for r in range(B_GROUP):
        for p in range(B_KV_HEADS // 2):
            a = 2 * p * per_group + r // 2
            b = (2 * p + 1) * per_group + r // 2
            lanes = slice((2 * r + p) * LANES, (2 * r + p + 1) * LANES)
            q_ref[:, lanes] = _regroup_slab(q_slabs[a], q_slabs[b], r % 2, lo64)
            gate_ref[:, lanes] = _regroup_slab(g_slabs[a], g_slabs[b], r % 2, lo64)


def _swa_pre_sample(h, kv_norm, wkv, k_norm, bd, cos, sin, b_norm, wq, wg, q_norm):
    m = h.shape[0]
    return pl.pallas_call(
        _swa_pre_sample_kernel,
        out_shape=[
            jax.ShapeDtypeStruct((m, B_KV_WIDTH), F32),
            jax.ShapeDtypeStruct((m, B_KV_WIDTH), F32),
            jax.ShapeDtypeStruct((m, BRANCH_WIDTH), F32),
            jax.ShapeDtypeStruct((m, BRANCH_WIDTH), F32),
        ],
        compiler_params=pltpu.CompilerParams(vmem_limit_bytes=VMEM_LIMIT_BYTES),
        name="swa_pre_sample",
    )(h, kv_norm, wkv, k_norm, bd, cos, sin, b_norm, wq, wg, q_norm)


def _swa_sample_kernel(q_ref, gate_ref, kn_ref, vn_ref, ck_ref, cv_ref, sink_ref, og_ref, nk_ref, nv_ref):
    nb, seq = q_ref.shape[0], q_ref.shape[1]
    kvw = B_KV_WIDTH
    trows = SUBLANES
    nrows = B_GROUP * B_KV_HEADS * trows
    rowi = lax.broadcasted_iota(jnp.int32, (nrows, 2 * WINDOW), 0)
    coli = lax.broadcasted_iota(jnp.int32, (nrows, 2 * WINDOW), 1)
    dpos = coli - rowi % trows
    mask = jnp.logical_and(dpos >= 0, dpos <= WINDOW)
    lane8 = lax.broadcasted_iota(jnp.int32, (trows, kvw), 1) // B_HEAD_DIM
    grow = (lax.broadcasted_iota(jnp.int32, (nrows, kvw), 0) // trows) % B_KV_HEADS
    gkeep = grow == lax.broadcasted_iota(jnp.int32, (nrows, kvw), 1) // B_HEAD_DIM
    sink = sink_ref[...][:, 0:1] * LOG2E
    zq = jnp.zeros((trows - seq, BRANCH_WIDTH), F32)
    lo64_q = lax.broadcasted_iota(jnp.int32, (seq, LANES), 1) < B_HEAD_DIM
    zk = jnp.zeros((WINDOW - seq, kvw), F32)

    def per_seq(i, carry):
        ck = ck_ref[i]
        cv = cv_ref[i]
        k_new = kn_ref[i]
        v_new = vn_ref[i]
        nk_ref[i, 0:WINDOW - seq, :] = ck[seq:, :]
        nk_ref[i, WINDOW - seq:WINDOW, :] = k_new
        nv_ref[i, 0:WINDOW - seq, :] = cv[seq:, :]
        nv_ref[i, WINDOW - seq:WINDOW, :] = v_new

        q8 = jnp.concatenate([q_ref[i], zq], axis=0)
        pieces = []
        for r in range(B_GROUP):
            slab = q8[:, r * kvw:(r + 1) * kvw]
            for g in range(B_KV_HEADS):
                pieces.append(jnp.where(lane8 == g, slab, 0.0))
        qrows = jnp.concatenate(pieces, axis=0).astype(BF16)
        kpad = jnp.concatenate([k_new, zk], axis=0).astype(BF16)
        vpad = jnp.concatenate([v_new, zk], axis=0).astype(BF16)
        s = jnp.concatenate([_dot_nt(qrows, ck.astype(BF16)), _dot_nt(qrows, kpad)], axis=1)
        s = jnp.where(mask, s, -jnp.inf)
        m = jnp.maximum(jnp.max(s, axis=-1, keepdims=True), sink)
        p = jnp.exp2(s - m)
        den = jnp.sum(p, axis=-1, keepdims=True) + jnp.exp2(sink - m)
        pb = p.astype(BF16)
        o = _dot(pb[:, :WINDOW], cv.astype(BF16)) + _dot(pb[:, WINDOW:], vpad)
        o = jnp.where(gkeep, o / den, 0.0)
        outs = []
        for r in range(B_GROUP):
            acc = o[r * B_KV_HEADS * trows:r * B_KV_HEADS * trows + trows]
            for g in range(1, B_KV_HEADS):
                acc = acc + o[(r * B_KV_HEADS + g) * trows:(r * B_KV_HEADS + g + 1) * trows]
            outs.append(acc)
        gated = jnp.concatenate(outs, axis=1)[0:seq] * _silu(gate_ref[i])
        slabs = []
        for g in range(B_KV_HEADS):
            for j in range(B_GROUP // 2):
                a = gated[:, (4 * j + g // 2) * LANES:(4 * j + g // 2 + 1) * LANES]
                b = gated[:, (4 * j + 2 + g // 2) * LANES:(4 * j + 2 + g // 2 + 1) * LANES]
                slabs.append(_regroup_slab(a, b, g % 2, lo64_q))
        og_ref[i] = jnp.concatenate(slabs, axis=1)
        return carry

    lax.fori_loop(0, nb, per_seq, 0, unroll=4)


def _swa_sample(q, gate, k_new, v_new, ck, cv, sink_rows):
    nbatch, seq, _ = q.shape
    nb = min(SAMPLE_BATCH_BLOCK, nbatch)
    assert nbatch % nb == 0 and ck.shape[1] == WINDOW
    cache_spec = pl.BlockSpec((nb, WINDOW, B_KV_WIDTH), lambda b: (b, 0, 0))
    wide_spec = pl.BlockSpec((nb, seq, BRANCH_WIDTH), lambda b: (b, 0, 0))
    new_spec = pl.BlockSpec((nb, seq, B_KV_WIDTH), lambda b: (b, 0, 0))
    return pl.pallas_call(
        _swa_sample_kernel,
        grid=(nbatch // nb,),
        in_specs=[wide_spec, wide_spec, new_spec, new_spec, cache_spec, cache_spec,
                  pl.BlockSpec(sink_rows.shape, lambda b: (0, 0))],
        out_specs=[wide_spec, cache_spec, cache_spec],
        out_shape=[
            jax.ShapeDtypeStruct((nbatch, seq, BRANCH_WIDTH), F32),
            jax.ShapeDtypeStruct(ck.shape, F32),
            jax.ShapeDtypeStruct(cv.shape, F32),
        ],
        compiler_params=pltpu.CompilerParams(
            dimension_semantics=("arbitrary",), vmem_limit_bytes=VMEM_LIMIT_BYTES),
        name="swa_sample",
    )(q, gate, k_new, v_new, ck, cv, sink_rows)


def _rope_tables(first_pos, n_pos, repeat=1):
    half = B_HEAD_DIM // 2
    inv_freq = ROPE_THETA ** (-np.arange(half, dtype=np.float64) / half)
    ang = (first_pos + np.arange(n_pos, dtype=np.float64))[:, None] * inv_freq[None, :]
    cos, sin = np.cos(ang), np.sin(ang)
    cos_t = np.tile(np.concatenate([cos, cos, cos, cos], axis=1), (repeat, 1))
    sin_t = np.tile(np.concatenate([-sin, sin, -sin, sin], axis=1), (repeat, 1))
    return jnp.asarray(cos_t, F32), jnp.asarray(sin_t, F32)


def kernel(x_prompt, x_sample, state_gla, cache_swa_k, cache_swa_v, a_norm, a_w_in, a_w_gate2, a_b_gate,
           a_out_norm, a_w_out, kv_norm, w_k, w_v, k_norm, b_norm, b_w_in, b_q_norm, b_sinks, b_w_out):
    assert a_norm.shape[0] == 1 and b_norm.shape[0] == 1
    bsz_p, seq_p, d = x_prompt.shape
    bsz_s, seq_s, _ = x_sample.shape
    kw, bw = A_KEY_WIDTH, BRANCH_WIDTH

    w_in = a_w_in[0]
    wa_qk = w_in[:, :2 * kw].astype(BF16)
    wa_v = w_in[:, 2 * kw:2 * kw + bw].astype(BF16)
    wa_g = w_in[:, 2 * kw + bw:2 * kw + 2 * bw].astype(BF16)
    wa_low = jnp.pad(w_in[:, 2 * kw + 2 * bw:].astype(BF16), ((0, 0), (0, LANES - A_GATE_RANK)))
    wa_g2 = jnp.pad(a_w_gate2[0].astype(BF16), ((0, LANES - A_GATE_RANK), (0, 0)))
    a_bg = a_b_gate[0][None, :]
    a_n = a_norm[0][None, :]
    a_on = a_out_norm[0][None, :]
    wa_out = a_w_out[0].astype(BF16)
    w_kv = jnp.concatenate([w_k.astype(BF16), w_v.astype(BF16)], axis=1)
    wb_q = b_w_in[0][:, :bw].astype(BF16)
    wb_g = b_w_in[0][:, bw:].astype(BF16)
    wb_out = b_w_out[0].astype(BF16)
    kvn = kv_norm[None, :]
    bn = b_norm[0][None, :]
    kn_t = jnp.tile(k_norm, B_KV_WIDTH // B_HEAD_DIM)[None, :]
    qn_t = jnp.tile(b_q_norm[0], B_KV_WIDTH // B_HEAD_DIM)[None, :]
    sinks = b_sinks[0]
    grp = np.arange(B_KV_WIDTH) // B_HEAD_DIM
    bd = jnp.asarray(grp[:, None] == grp[None, :], BF16)

    cos_p, sin_p = _rope_tables(0, seq_p)
    h1_p, st_p = _gla_prompt(x_prompt, a_n, wa_qk, wa_v, wa_g, wa_low, wa_g2, a_bg, a_on, wa_out)
    y_p, kc_p, vc_p = _swa_prompt(h1_p, kvn, w_kv, kn_t, bd, cos_p, sin_p, bn, wb_q, wb_g, qn_t, sinks, wb_out)

    m = bsz_s * seq_s
    xs = x_sample.reshape(m, d)
    qk_a, v_a, gate_a, glow_a = _gla_proj_sample(xs, a_n, wa_qk, wa_v, wa_g, wa_low)
    o_s, st_s = _gla_sample(qk_a.reshape(bsz_s, seq_s, -1), v_a.reshape(bsz_s, seq_s, -1),
                            glow_a.reshape(bsz_s, seq_s, -1), wa_g2, a_bg, state_gla[0])
    h1_s = _gla_out(o_s.reshape(m, bw), gate_a, xs, a_on, wa_out)

    sink_rows = jnp.broadcast_to(
        sinks.reshape(B_KV_HEADS, B_GROUP).T[:, :, None, None], (B_GROUP, B_KV_HEADS, SUBLANES, LANES)
    ).reshape(B_GROUP * B_KV_HEADS * SUBLANES, LANES)
    cos_s, sin_s = _rope_tables(PAST_LEN, seq_s, repeat=bsz_s)
    k_new, v_new, q_s, gate_b = _swa_pre_sample(h1_s, kvn, w_kv, kn_t, bd, cos_s, sin_s, bn, wb_q, wb_g, qn_t)
    og_s, nk_s, nv_s = _swa_sample(
        q_s.reshape(bsz_s, seq_s, bw), gate_b.reshape(bsz_s, seq_s, bw),
        k_new.reshape(bsz_s, seq_s, B_KV_WIDTH), v_new.reshape(bsz_s, seq_s, B_KV_WIDTH),
        cache_swa_k.reshape(bsz_s, WINDOW, B_KV_WIDTH), cache_swa_v.reshape(bsz_s, WINDOW, B_KV_WIDTH), sink_rows)
    y_s = _out_proj(og_s.reshape(m, bw), h1_s, wb_out)

    return (y_p, y_s.reshape(bsz_s, seq_s, d),
            st_p[None], st_s[None],
            kc_p.reshape(bsz_p, WINDOW, B_KV_HEADS, B_HEAD_DIM), vc_p.reshape(bsz_p, WINDOW, B_KV_HEADS, B_HEAD_DIM),
            nk_s.reshape(bsz_s, WINDOW, B_KV_HEADS, B_HEAD_DIM), nv_s.reshape(bsz_s, WINDOW, B_KV_HEADS, B_HEAD_DIM))
```

```python
import jax
import jax.numpy as jnp
import numpy as np
from jax import lax
from jax.experimental import pallas as pl
from jax.experimental.pallas import tpu as pltpu

F32 = jnp.float32
BF16 = jnp.bfloat16

A_HEADS = 4
A_HEAD_DK = 128
A_HEAD_DV = 512
A_KEY_WIDTH = A_HEADS * A_HEAD_DK
BRANCH_WIDTH = A_HEADS * A_HEAD_DV
A_GATE_RANK = 16
A_GATE_NORMALIZER = 16.0
B_HEAD_DIM = 64
B_HEADS = BRANCH_WIDTH // B_HEAD_DIM
B_KV_HEADS = 4
B_GROUP = B_HEADS // B_KV_HEADS
B_KV_WIDTH = B_KV_HEADS * B_HEAD_DIM
WINDOW = 128
ROPE_THETA = 10000.0
RMS_EPS = 1e-6
PAST_LEN = 16384

LANES = 128
SUBLANES = 8
VMEM_LIMIT_BYTES = 56 * 1024 * 1024

PROMPT_TILE = 512
GLA_CHUNK = 256
GLA_SAFE_DECAY = 80.0
SAMPLE_BATCH_BLOCK = 8
OUT_PROJ_CHUNKS = 4
LOG2E = 1.4426950408889634


def _dot(a, b):
    return jnp.dot(a, b, preferred_element_type=F32)


def _dot_nt(a, b):
    return lax.dot_general(a, b, (((1,), (1,)), ((), ())), preferred_element_type=F32)


def _split2(x):
    hi = x.astype(BF16)
    lo = (x - hi.astype(F32)).astype(BF16)
    return hi, lo


def _rmsnorm(x, g):
    ms = jnp.mean(x * x, axis=-1, keepdims=True)
    return x * lax.rsqrt(ms + RMS_EPS) * g


def _log_sigmoid(x):
    return jnp.minimum(x, 0.0) - jnp.log1p(jnp.exp(-jnp.abs(x)))


def _silu(x):
    return x * (1.0 / (1.0 + jnp.exp(-x)))


def _head_norm(x, bd, gain):
    hi, lo = _split2(x * x)
    ss = _dot(hi, bd) + _dot(lo, bd)
    return x * lax.rsqrt(ss * (1.0 / B_HEAD_DIM) + RMS_EPS) * gain


def _rope_slab(x, cos, sin_signed, first_half):
    xr = jnp.where(first_half, pltpu.roll(x, 96, 1), pltpu.roll(x, 32, 1))
    return x * cos + xr * sin_signed


def _lane_masks(rows):
    lane = lax.broadcasted_iota(jnp.int32, (rows, LANES), 1)
    first_half = (lane % B_HEAD_DIM) < (B_HEAD_DIM // 2)
    lo64 = lane < B_HEAD_DIM
    return first_half, lo64


def _split_group(slab, g, lo64):
    swapped = pltpu.roll(slab, B_HEAD_DIM, 1)
    zero = jnp.zeros_like(slab)
    if g % 2 == 0:
        return jnp.where(lo64, slab, zero), jnp.where(lo64, zero, swapped)
    return jnp.where(lo64, swapped, zero), jnp.where(lo64, zero, slab)


def _regroup_slab(a, b, odd, lo64):
    if odd:
        return jnp.where(lo64, pltpu.roll(a, B_HEAD_DIM, 1), b)
    return jnp.where(lo64, a, pltpu.roll(b, B_HEAD_DIM, 1))


def _gla_prompt_kernel(x_ref, an_ref, wqk_ref, wv_ref, wg_ref, wlow_ref, wg2_ref, bg_ref, on_ref, wout_ref,
                       h1_ref, st_ref,
                       qk_s, v_s, gate_s, b_s, o_s, oi_s):
    tile = x_ref.shape[0]
    chunk = GLA_CHUNK
    n_chunks = tile // chunk
    dk, dv, kw = A_HEAD_DK, A_HEAD_DV, A_KEY_WIDTH

    @pl.when(pl.program_id(1) == 0)
    def _():
        st_ref[...] = jnp.zeros_like(st_ref)

    x = x_ref[...]
    u = _rmsnorm(x, an_ref[...]).astype(BF16)
    qk_s[...] = _dot(u, wqk_ref[...])
    v_s[...] = _dot(u, wv_ref[...])
    gate_s[...] = _dot(u, wg_ref[...])
    glow = _dot(u, wlow_ref[...]).astype(BF16)
    pre = _dot(glow, wg2_ref[...]) + bg_ref[...]
    loga = _log_sigmoid(pre) * (1.0 / A_GATE_NORMALIZER)

    row = lax.broadcasted_iota(jnp.int32, (chunk, chunk), 0)
    col = lax.broadcasted_iota(jnp.int32, (chunk, chunk), 1)
    lower = row >= col
    tri = jnp.where(lower, 1.0, 0.0).astype(BF16)
    for c in range(n_chunks):
        hi, lo = _split2(loga[c * chunk:(c + 1) * chunk])
        b_s[c * chunk:(c + 1) * chunk, :] = _dot(tri, hi) + _dot(tri, lo)
    safe = jnp.min(b_s[...]) >= -GLA_SAFE_DECAY

    scale = A_HEAD_DK ** -0.5
    for c in range(n_chunks):
        rows = slice(c * chunk, (c + 1) * chunk)
        b = b_s[rows, :]
        eb = jnp.exp(b)
        enb = jnp.exp(-b)
        blast = b[chunk - 1:chunk, :]
        ekl = jnp.exp(blast - b)
        eblast = jnp.exp(blast)
        for h in range(A_HEADS):
            ks = slice(h * dk, (h + 1) * dk)
            vs = slice(h * dv, (h + 1) * dv)
            q = qk_s[rows, ks] * scale
            k = qk_s[rows, kw + h * dk: kw + (h + 1) * dk]
            vh = v_s[rows, vs].astype(BF16)
            state = st_ref[h]
            qt = (q * eb[:, ks]).astype(BF16)
            o_inter = _dot(qt, state.astype(BF16))
            a = _dot_nt(qt, (k * enb[:, ks]).astype(BF16))
            a = jnp.where(lower, a, 0.0).astype(BF16)
            oi_s[rows, vs] = o_inter
            o_s[rows, vs] = o_inter + _dot(a, vh)
            kd_t = jnp.transpose(k * ekl[:, ks]).astype(BF16)
            dec = jnp.transpose(jnp.broadcast_to(eblast[:, ks], (dk, dk)))
            st_ref[h] = state * jnp.concatenate([dec] * (dv // dk), axis=1) + _dot(kd_t, vh)

    @pl.when(jnp.logical_not(safe))
    def _():
        trow = lax.broadcasted_iota(jnp.int32, (chunk, dk), 0)
        sub8 = lax.broadcasted_iota(jnp.int32, (SUBLANES, 1), 0)
        for c in range(n_chunks):
            rows = slice(c * chunk, (c + 1) * chunk)
            for h in range(A_HEADS):
                ks = slice(h * dk, (h + 1) * dk)
                vs = slice(h * dv, (h + 1) * dv)
                q = qk_s[rows, ks] * scale
                b = b_s[rows, ks]

                def body(s, acc, c=c, h=h, q=q, b=b):
                    r8 = pl.multiple_of(c * chunk + (s // SUBLANES) * SUBLANES, SUBLANES)
                    pick = sub8 == s % SUBLANES

                    def row_of(ref, lanes):
                        return jnp.sum(jnp.where(pick, ref[pl.ds(r8, SUBLANES), lanes], 0.0), axis=0, keepdims=True)

                    brow = row_of(b_s, slice(h * dk, (h + 1) * dk))
                    krow = row_of(qk_s, slice(kw + h * dk, kw + (h + 1) * dk))
                    vrow = row_of(v_s, slice(h * dv, (h + 1) * dv))
                    w = jnp.where(trow >= s, jnp.exp(jnp.minimum(b - brow, 0.0)), 0.0)
                    colv = jnp.sum(q * krow * w, axis=-1, keepdims=True)
                    return acc + colv * vrow

                o_s[rows, vs] = oi_s[rows, vs] + lax.fori_loop(0, chunk, body, jnp.zeros((chunk, dv), F32))

    y = x
    for h in range(A_HEADS):
        vs = slice(h * dv, (h + 1) * dv)
        o = _rmsnorm(o_s[:, vs], on_ref[...])
        og = (o * _silu(gate_s[:, vs])).astype(BF16)
        y = y + _dot(og, wout_ref[vs, :])
    h1_ref[...] = y


def _gla_prompt(x, a_norm, wqk, wv, wg, wlow, wg2, bg, out_norm, wout):
    bsz, seq, d = x.shape
    tile = min(PROMPT_TILE, seq)
    assert seq % tile == 0 and tile % GLA_CHUNK == 0
    const = lambda shape: pl.BlockSpec(shape, lambda b, l: (0,) * len(shape), pipeline_mode=pl.Buffered(1))
    return pl.pallas_call(
        _gla_prompt_kernel,
        grid=(bsz, seq // tile),
        in_specs=[
            pl.BlockSpec((None, tile, d), lambda b, l: (b, l, 0)),
            const(a_norm.shape), const(wqk.shape), const(wv.shape), const(wg.shape), const(wlow.shape),
            const(wg2.shape), const(bg.shape), const(out_norm.shape), const(wout.shape),
        ],
        out_specs=[
            pl.BlockSpec((None, tile, d), lambda b, l: (b, l, 0)),
            pl.BlockSpec((None, A_HEADS, A_HEAD_DK, A_HEAD_DV), lambda b, l: (b, 0, 0, 0)),
        ],
        out_shape=[
            jax.ShapeDtypeStruct((bsz, seq, d), F32),
            jax.ShapeDtypeStruct((bsz, A_HEADS, A_HEAD_DK, A_HEAD_DV), F32),
        ],
        scratch_shapes=[
            pltpu.VMEM((tile, 2 * A_KEY_WIDTH), F32),
            pltpu.VMEM((tile, BRANCH_WIDTH), F32),
            pltpu.VMEM((tile, BRANCH_WIDTH), F32),
            pltpu.VMEM((tile, A_KEY_WIDTH), F32),
            pltpu.VMEM((tile, BRANCH_WIDTH), F32),
            pltpu.VMEM((tile, BRANCH_WIDTH), F32),
        ],
        compiler_params=pltpu.CompilerParams(
            dimension_semantics=("arbitrary", "arbitrary"), vmem_limit_bytes=VMEM_LIMIT_BYTES),
        name="gla_prompt",
    )(x, a_norm, wqk, wv, wg, wlow, wg2, bg, out_norm, wout)


def _swa_prompt_kernel(h_ref, kvn_ref, wkv_ref, kn_ref, bd_ref, cos_ref, sin_ref, bn_ref, wq_ref, wg_ref,
                       qn_ref, sink_ref, wout_ref,
                       y_ref, kc_ref, vc_ref,
                       ktop_s, kbot_s, vtop_s, vbot_s, qb_s, ub_s, kv_s, q_s, gsl_s, osl_s, og_s):
    tile = h_ref.shape[0]
    l = pl.program_id(1)
    blk = WINDOW
    kvw = B_KV_WIDTH
    pairs = B_GROUP // 2
    n_slabs = BRANCH_WIDTH // LANES
    n_iter = (tile // blk) * B_KV_HEADS
    gate_cols = wg_ref.shape[2]

    @pl.when(l == 0)
    def _():
        zeros = jnp.zeros((B_KV_HEADS, WINDOW, LANES), BF16)
        ktop_s[:, 0:WINDOW, :] = zeros
        kbot_s[:, 0:WINDOW, :] = zeros
        vtop_s[:, 0:WINDOW, 0:LANES] = zeros
        vbot_s[:, 0:WINDOW, 0:LANES] = zeros
        lane = lax.broadcasted_iota(jnp.int32, (B_KV_HEADS, WINDOW + tile, LANES), 2)
        vtop_s[:, :, LANES:] = jnp.where(lane < B_HEAD_DIM, 1.0, 0.0).astype(BF16)
        vbot_s[:, :, LANES:] = jnp.where(lane < B_HEAD_DIM, 0.0, 1.0).astype(BF16)

    h = h_ref[...]
    cos = cos_ref[...]
    sin = sin_ref[...]
    first_half, lo64 = _lane_masks(tile)
    bd = bd_ref[...]

    kv_s[...] = _dot(_rmsnorm(h, kvn_ref[...]).astype(BF16), wkv_ref[...])
    ub_s[...] = _rmsnorm(h, bn_ref[...]).astype(BF16)

    kn = _head_norm(kv_s[:, :kvw], bd, kn_ref[...])
    v = kv_s[:, kvw:]
    k_slabs = [_rope_slab(kn[:, s * LANES:(s + 1) * LANES], cos, sin, first_half) for s in range(kvw // LANES)]

    @pl.when(l == pl.num_programs(1) - 1)
    def _():
        kc_ref[...] = jnp.concatenate(k_slabs, axis=1)[tile - WINDOW:, :]
        vc_ref[...] = v[tile - WINDOW:, :]

    for g in range(B_KV_HEADS):
        s = g // 2
        top, bot = _split_group(k_slabs[s], g, lo64)
        ktop_s[g, WINDOW:WINDOW + tile, :] = top.astype(BF16)
        kbot_s[g, WINDOW:WINDOW + tile, :] = bot.astype(BF16)
        top, bot = _split_group(v[:, s * LANES:(s + 1) * LANES], g, lo64)
        vtop_s[g, WINDOW:WINDOW + tile, 0:LANES] = top.astype(BF16)
        vbot_s[g, WINDOW:WINDOW + tile, 0:LANES] = bot.astype(BF16)

    qscale = qn_ref[...] * (B_HEAD_DIM ** -0.5 * LOG2E)
    qcos = [cos * qscale[:, s * LANES:(s + 1) * LANES] for s in range(kvw // LANES)]
    qsin = [sin * jnp.where(first_half[0:1], pltpu.roll(qscale[:, s * LANES:(s + 1) * LANES], 96, 1),
                            pltpu.roll(qscale[:, s * LANES:(s + 1) * LANES], 32, 1)) for s in range(kvw // LANES)]

    q_s[...] = _dot(ub_s[...], wq_ref[...])
    for c in range(BRANCH_WIDTH // kvw):
        x = q_s[:, c * kvw:(c + 1) * kvw]
        hi, lo = _split2(x * x)
        rinv = lax.rsqrt((_dot(hi, bd) + _dot(lo, bd)) * (1.0 / B_HEAD_DIM) + RMS_EPS)
        for s in range(kvw // LANES):
            xs = x[:, s * LANES:(s + 1) * LANES]
            xr = jnp.where(first_half, pltpu.roll(xs, 96, 1), pltpu.roll(xs, 32, 1))
            qr = ((xs * qcos[s] + xr * qsin[s]) * rinv[:, s * LANES:(s + 1) * LANES]).astype(BF16)
            for i in range(tile // blk):
                qb_s[i, c * (kvw // LANES) + s] = qr[i * blk:(i + 1) * blk]

    rowi = lax.broadcasted_iota(jnp.int32, (blk, 2 * blk), 0)
    coli = lax.broadcasted_iota(jnp.int32, (blk, 2 * blk), 1)
    band = jnp.logical_and(coli - rowi >= 0, coli - rowi <= WINDOW)
    _, lo64_b = _lane_masks(blk)

    def attn_pair(idx):
        i = idx // B_KV_HEADS
        g = idx % B_KV_HEADS
        r0 = pl.multiple_of(i * blk, blk)
        first_col = jnp.where(l * tile + r0 > 0, 0, WINDOW)
        mask = jnp.logical_and(band, coli >= first_col)
        kpad = jnp.concatenate([ktop_s[g, pl.ds(r0, 2 * blk), :], kbot_s[g, pl.ds(r0, 2 * blk), :]], axis=0)
        vpad = jnp.concatenate([vtop_s[g, pl.ds(r0, 2 * blk), :], vbot_s[g, pl.ds(r0, 2 * blk), :]], axis=0)
        qg = qb_s[i, pl.ds(g * pairs, pairs)].reshape(pairs * blk, LANES)
        s = _dot_nt(qg, kpad)
        ps, esinks = [], []
        for j in range(pairs):
            pj, ej = [], []
            for half in range(2):
                sink = sink_ref[g * B_GROUP + 2 * j + half] * LOG2E
                sh = jnp.where(mask, s[j * blk:(j + 1) * blk, half * 2 * blk:(half + 1) * 2 * blk], -jnp.inf)
                m = jnp.maximum(jnp.max(sh, axis=-1, keepdims=True), sink)
                pj.append(jnp.exp2(sh - m).astype(BF16))
                ej.append(jnp.exp2(sink - m))
            ps.append(jnp.concatenate(pj, axis=1))
            esinks.append(jnp.where(lo64_b, ej[0], ej[1]))
        o = _dot(jnp.concatenate(ps, axis=0), vpad)
        for j in range(pairs):
            oj = o[j * blk:(j + 1) * blk]
            osl_s[g * pairs + j, pl.ds(r0, blk), :] = oj[:, :LANES] / (oj[:, LANES:] + esinks[j])

    def attn_body(k, carry):
        gc = _dot(ub_s[...], wg_ref[k])
        for s in range(gate_cols // LANES):
            gsl_s[k * (gate_cols // LANES) + s] = gc[:, s * LANES:(s + 1) * LANES]
        attn_pair(2 * k)
        attn_pair(2 * k + 1)
        return carry

    lax.fori_loop(0, n_iter // 2, attn_body, 0, unroll=4)

    for buf in (ktop_s, kbot_s):
        buf[:, 0:WINDOW, :] = buf[:, tile:tile + WINDOW, :]
    for buf in (vtop_s, vbot_s):
        buf[:, 0:WINDOW, 0:LANES] = buf[:, tile:tile + WINDOW, 0:LANES]

    y = h
    per = n_slabs // OUT_PROJ_CHUNKS
    for c in range(OUT_PROJ_CHUNKS):
        for sl in range(c * per, (c + 1) * per):
            og_s[:, sl * LANES:(sl + 1) * LANES] = (osl_s[sl] * _silu(gsl_s[sl])).astype(BF16)
        cols = slice(c * per * LANES, (c + 1) * per * LANES)
        y = y + _dot(og_s[:, cols], wout_ref[cols, :])
    y_ref[...] = y


def _swa_prompt(h, kv_norm, wkv, k_norm, bd, cos, sin, b_norm, wq, wg, q_norm, sinks, wout):
    bsz, seq, d = h.shape
    tile = min(PROMPT_TILE, seq)
    assert seq % tile == 0 and tile % WINDOW == 0 and seq >= WINDOW
    n_iter = (tile // WINDOW) * B_KV_HEADS
    gate_cols = BRANCH_WIDTH // (n_iter // 2)
    assert gate_cols % LANES == 0
    wg = wg.reshape(d, n_iter // 2, gate_cols).transpose(1, 0, 2)
    const = lambda shape: pl.BlockSpec(shape, lambda b, l: (0,) * len(shape), pipeline_mode=pl.Buffered(1))
    k_scratch = pltpu.VMEM((B_KV_HEADS, WINDOW + tile, LANES), BF16)
    v_scratch = pltpu.VMEM((B_KV_HEADS, WINDOW + tile, 2 * LANES), BF16)
    n_slabs = BRANCH_WIDTH // LANES
    return pl.pallas_call(
        _swa_prompt_kernel,
        grid=(bsz, seq // tile),
        in_specs=[
            pl.BlockSpec((None, tile, d), lambda b, l: (b, l, 0)),
            const(kv_norm.shape), const(wkv.shape), const(k_norm.shape), const(bd.shape),
            pl.BlockSpec((tile, LANES), lambda b, l: (l, 0)),
            pl.BlockSpec((tile, LANES), lambda b, l: (l, 0)),
            const(b_norm.shape), const(wq.shape), const(wg.shape), const(q_norm.shape),
            pl.BlockSpec(memory_space=pltpu.SMEM),
            const(wout.shape),
        ],
        out_specs=[
            pl.BlockSpec((None, tile, d), lambda b, l: (b, l, 0)),
            pl.BlockSpec((None, WINDOW, B_KV_WIDTH), lambda b, l: (b, 0, 0)),
            pl.BlockSpec((None, WINDOW, B_KV_WIDTH), lambda b, l: (b, 0, 0)),
        ],
        out_shape=[
            jax.ShapeDtypeStruct((bsz, seq, d), F32),
            jax.ShapeDtypeStruct((bsz, WINDOW, B_KV_WIDTH), F32),
            jax.ShapeDtypeStruct((bsz, WINDOW, B_KV_WIDTH), F32),
        ],
        scratch_shapes=[
            k_scratch, k_scratch, v_scratch, v_scratch,
            pltpu.VMEM((tile // WINDOW, n_slabs, WINDOW, LANES), BF16),
            pltpu.VMEM((tile, d), BF16),
            pltpu.VMEM((tile, 2 * B_KV_WIDTH), F32),
            pltpu.VMEM((tile, BRANCH_WIDTH), F32),
            pltpu.VMEM((n_slabs, tile, LANES), F32),
            pltpu.VMEM((n_slabs, tile, LANES), F32),
            pltpu.VMEM((tile, BRANCH_WIDTH), BF16),
        ],
        compiler_params=pltpu.CompilerParams(
            dimension_semantics=("arbitrary", "arbitrary"), vmem_limit_bytes=VMEM_LIMIT_BYTES),
        name="swa_prompt",
    )(h, kv_norm, wkv, k_norm, bd, cos, sin, b_norm, wq, wg, q_norm, sinks, wout)


def _gla_proj_sample_kernel(x_ref, g_ref, wqk_ref, wv_ref, wg_ref, wlow_ref, qk_ref, v_ref, gate_ref, glow_ref):
    u = _rmsnorm(x_ref[...], g_ref[...]).astype(BF16)
    qk_ref[...] = _dot(u, wqk_ref[...])
    v_ref[...] = _dot(u, wv_ref[...])
    gate_ref[...] = _dot(u, wg_ref[...])
    glow_ref[...] = _dot(u, wlow_ref[...])


def _gla_proj_sample(x, g, wqk, wv, wg, wlow):
    m = x.shape[0]
    return pl.pallas_call(
        _gla_proj_sample_kernel,
        out_shape=[jax.ShapeDtypeStruct((m, w.shape[1]), F32) for w in (wqk, wv, wg, wlow)],
        compiler_params=pltpu.CompilerParams(vmem_limit_bytes=VMEM_LIMIT_BYTES),
        name="gla_proj_sample",
    )(x, g, wqk, wv, wg, wlow)


def _gla_out_kernel(o_ref, gate_ref, x_ref, on_ref, w_ref, y_ref, og_s):
    for h in range(A_HEADS):
        vs = slice(h * A_HEAD_DV, (h + 1) * A_HEAD_DV)
        o = _rmsnorm(o_ref[:, vs], on_ref[...])
        og_s[:, vs] = (o * _silu(gate_ref[:, vs])).astype(BF16)
    y_ref[...] = _dot(og_s[...], w_ref[...]) + x_ref[...]


def _gla_out(o, gate, x, out_norm, wout):
    m = o.shape[0]
    return pl.pallas_call(
        _gla_out_kernel,
        out_shape=jax.ShapeDtypeStruct(x.shape, F32),
        scratch_shapes=[pltpu.VMEM((m, BRANCH_WIDTH), BF16)],
        compiler_params=pltpu.CompilerParams(vmem_limit_bytes=VMEM_LIMIT_BYTES),
        name="gla_out_sample",
    )(o, gate, x, out_norm, wout)


def _out_proj_kernel(og_ref, x_ref, w_ref, y_ref):
    y_ref[...] = _dot(og_ref[...].astype(BF16), w_ref[...]) + x_ref[...]


def _out_proj(og, x, wout):
    return pl.pallas_call(
        _out_proj_kernel,
        out_shape=jax.ShapeDtypeStruct(x.shape, F32),
        compiler_params=pltpu.CompilerParams(vmem_limit_bytes=VMEM_LIMIT_BYTES),
        name="swa_out_sample",
    )(og, x, wout)


def _gla_sample_kernel(qk_ref, v_ref, glow_ref, wg2_ref, bg_ref, st_ref, o_ref, nst_ref):
    nb, seq = qk_ref.shape[0], qk_ref.shape[1]
    dk, dv, kw = A_HEAD_DK, A_HEAD_DV, A_KEY_WIDTH
    scale = A_HEAD_DK ** -0.5
    pad_rows = SUBLANES - seq

    def per_seq(i, carry):
        qk = qk_ref[i]
        q = qk[:, 0:kw] * scale
        k = qk[:, kw:2 * kw]
        v = v_ref[i]
        glow = glow_ref[i]
        glow8 = jnp.concatenate([glow, jnp.zeros((pad_rows, LANES), F32)], axis=0).astype(BF16)
        pre = _dot(glow8, wg2_ref[...])[0:seq] + bg_ref[...]
        loga = _log_sigmoid(pre) * (1.0 / A_GATE_NORMALIZER)
        bs = [loga[0:1]]
        for t in range(1, seq):
            bs.append(bs[-1] + loga[t:t + 1])
        bmat = jnp.concatenate(bs, axis=0)
        blast = bs[-1]
        qt = q * jnp.exp(bmat)
        kd = k * jnp.exp(blast - bmat)
        eblast = jnp.exp(blast)

        outs = [[jnp.zeros((1, dv), F32) for _ in range(A_HEADS)] for _ in range(seq)]
        for t in range(seq):
            for s in range(t + 1):
                w = q[t:t + 1] * k[s:s + 1] * jnp.exp(bs[t] - bs[s])
                for h in range(A_HEADS):
                    a = jnp.sum(w[:, h * dk:(h + 1) * dk], axis=-1, keepdims=True)
                    outs[t][h] = outs[t][h] + a * v[s:s + 1, h * dv:(h + 1) * dv]
        intra = jnp.concatenate([jnp.concatenate(outs[t], axis=1) for t in range(seq)], axis=0)

        qt8 = jnp.concatenate([qt, jnp.zeros((pad_rows, kw), F32)], axis=0).astype(BF16)
        inter = []
        for h in range(A_HEADS):
            ks = slice(h * dk, (h + 1) * dk)
            state = st_ref[i, h]
            inter.append(_dot(qt8[:, ks], state.astype(BF16))[0:seq])
            m = jnp.concatenate([kd[:, ks], eblast[:, ks], jnp.zeros((dk - seq - 1, dk), F32)], axis=0)
            mt = jnp.transpose(m)
            vpad = jnp.concatenate([v[:, h * dv:(h + 1) * dv], jnp.zeros((dk - seq, dv), F32)], axis=0)
            nst_ref[i, h] = state * mt[:, seq:seq + 1] + _dot(mt.astype(BF16), vpad.astype(BF16))
        o_ref[i] = intra + jnp.concatenate(inter, axis=1)
        return carry

    lax.fori_loop(0, nb, per_seq, 0)


def _gla_sample(qk, v, glow, wg2, bg, state):
    nbatch, seq, _ = qk.shape
    nb = min(SAMPLE_BATCH_BLOCK // 2, nbatch)
    assert nbatch % nb == 0
    st_spec = pl.BlockSpec((nb, A_HEADS, A_HEAD_DK, A_HEAD_DV), lambda b: (b, 0, 0, 0))
    tok_spec = lambda a: pl.BlockSpec((nb, seq, a.shape[2]), lambda b: (b, 0, 0))
    return pl.pallas_call(
        _gla_sample_kernel,
        grid=(nbatch // nb,),
        in_specs=[
            tok_spec(qk), tok_spec(v), tok_spec(glow),
            pl.BlockSpec(wg2.shape, lambda b: (0, 0)),
            pl.BlockSpec(bg.shape, lambda b: (0, 0)),
            st_spec,
        ],
        out_specs=[pl.BlockSpec((nb, seq, BRANCH_WIDTH), lambda b: (b, 0, 0)), st_spec],
        out_shape=[
            jax.ShapeDtypeStruct((nbatch, seq, BRANCH_WIDTH), F32),
            jax.ShapeDtypeStruct(state.shape, F32),
        ],
        compiler_params=pltpu.CompilerParams(
            dimension_semantics=("arbitrary",), vmem_limit_bytes=VMEM_LIMIT_BYTES),
        name="gla_sample",
    )(qk, v, glow, wg2, bg, state)


def _swa_pre_sample_kernel(h_ref, kvn_ref, wkv_ref, kn_ref, bd_ref, cos_ref, sin_ref, bn_ref, wq_ref, wg_ref, qn_ref,
                           k_ref, v_ref, q_ref, gate_ref):
    m = h_ref.shape[0]
    kvw = B_KV_WIDTH
    h = h_ref[...]
    cos = cos_ref[...]
    sin = sin_ref[...]
    first_half, _ = _lane_masks(m)
    bd = bd_ref[...]

    kv = _dot(_rmsnorm(h, kvn_ref[...]).astype(BF16), wkv_ref[...])
    kn = _head_norm(kv[:, :kvw], bd, kn_ref[...])
    for s in range(kvw // LANES):
        k_ref[:, s * LANES:(s + 1) * LANES] = _rope_slab(kn[:, s * LANES:(s + 1) * LANES], cos, sin, first_half)
    v_ref[...] = kv[:, kvw:]

    ub = _rmsnorm(h, bn_ref[...]).astype(BF16)
    _, lo64 = _lane_masks(m)
    qgain = qn_ref[...] * (B_HEAD_DIM ** -0.5 * LOG2E)
    gate = _dot(ub, wg_ref[...])
    q_slabs, g_slabs = [], []
    for c in range(BRANCH_WIDTH // kvw):
        qc = _head_norm(_dot(ub, wq_ref[:, c * kvw:(c + 1) * kvw]), bd, qgain)
        for s in range(kvw // LANES):
            q_slabs.append(_rope_slab(qc[:, s * LANES:(s + 1) * LANES], cos, sin, first_half))
            g_slabs.append(gate[:, c * kvw + s * LANES:c * kvw + (s + 1) * LANES])
    per_group = B_GROUP // 2
    for r in range(B_GROUP):
        for p in range(B_KV_HEADS // 2):
            a = 2 * p * per_group + r // 2
            b = (2 * p + 1) * per_group + r // 2
            lanes = slice((2 * r + p) * LANES, (2 * r + p + 1) * LANES)
            q_ref[:, lanes] = _regroup_slab(q_slabs[a], q_slabs[b], r % 2, lo64)
            gate_ref[:, lanes] = _regroup_slab(g_slabs[a], g_slabs[b], r % 2, lo64)


def _swa_pre_sample(h, kv_norm, wkv, k_norm, bd, cos, sin, b_norm, wq, wg, q_norm):
    m = h.shape[0]
    return pl.pallas_call(
        _swa_pre_sample_kernel,
        out_shape=[
            jax.ShapeDtypeStruct((m, B_KV_WIDTH), F32),
            jax.ShapeDtypeStruct((m, B_KV_WIDTH), F32),
            jax.ShapeDtypeStruct((m, BRANCH_WIDTH), F32),
            jax.ShapeDtypeStruct((m, BRANCH_WIDTH), F32),
        ],
        compiler_params=pltpu.CompilerParams(vmem_limit_bytes=VMEM_LIMIT_BYTES),
        name="swa_pre_sample",
    )(h, kv_norm, wkv, k_norm, bd, cos, sin, b_norm, wq, wg, q_norm)


def _swa_sample_kernel(q_ref, gate_ref, kn_ref, vn_ref, ck_ref, cv_ref, sink_ref, og_ref, nk_ref, nv_ref):
    nb, seq = q_ref.shape[0], q_ref.shape[1]
    kvw = B_KV_WIDTH
    trows = SUBLANES
    nrows = B_GROUP * B_KV_HEADS * trows
    rowi = lax.broadcasted_iota(jnp.int32, (nrows, 2 * WINDOW), 0)
    coli = lax.broadcasted_iota(jnp.int32, (nrows, 2 * WINDOW), 1)
    dpos = coli - rowi % trows
    mask = jnp.logical_and(dpos >= 0, dpos <= WINDOW)
    lane8 = lax.broadcasted_iota(jnp.int32, (trows, kvw), 1) // B_HEAD_DIM
    grow = (lax.broadcasted_iota(jnp.int32, (nrows, kvw), 0) // trows) % B_KV_HEADS
    gkeep = grow == lax.broadcasted_iota(jnp.int32, (nrows, kvw), 1) // B_HEAD_DIM
    sink = sink_ref[...][:, 0:1] * LOG2E
    zq = jnp.zeros((trows - seq, BRANCH_WIDTH), F32)
    lo64_q = lax.broadcasted_iota(jnp.int32, (seq, LANES), 1) < B_HEAD_DIM
    zk = jnp.zeros((WINDOW - seq, kvw), F32)

    def per_seq(i, carry):
        ck = ck_ref[i]
        cv = cv_ref[i]
        k_new = kn_ref[i]
        v_new = vn_ref[i]
        nk_ref[i, 0:WINDOW - seq, :] = ck[seq:, :]
        nk_ref[i, WINDOW - seq:WINDOW, :] = k_new
        nv_ref[i, 0:WINDOW - seq, :] = cv[seq:, :]
        nv_ref[i, WINDOW - seq:WINDOW, :] = v_new

        q8 = jnp.concatenate([q_ref[i], zq], axis=0)
        pieces = []
        for r in range(B_GROUP):
            slab = q8[:, r * kvw:(r + 1) * kvw]
            for g in range(B_KV_HEADS):
                pieces.append(jnp.where(lane8 == g, slab, 0.0))
        qrows = jnp.concatenate(pieces, axis=0).astype(BF16)
        kpad = jnp.concatenate([k_new, zk], axis=0).astype(BF16)
        vpad = jnp.concatenate([v_new, zk], axis=0).astype(BF16)
        s = jnp.concatenate([_dot_nt(qrows, ck.astype(BF16)), _dot_nt(qrows, kpad)], axis=1)
        s = jnp.where(mask, s, -jnp.inf)
        m = jnp.maximum(jnp.max(s, axis=-1, keepdims=True), sink)
        p = jnp.exp2(s - m)
        den = jnp.sum(p, axis=-1, keepdims=True) + jnp.exp2(sink - m)
        pb = p.astype(BF16)
        o = _dot(pb[:, :WINDOW], cv.astype(BF16)) + _dot(pb[:, WINDOW:], vpad)
        o = jnp.where(gkeep, o / den, 0.0)
        outs = []
        for r in range(B_GROUP):
            acc = o[r * B_KV_HEADS * trows:r * B_KV_HEADS * trows + trows]
            for g in range(1, B_KV_HEADS):
                acc = acc + o[(r * B_KV_HEADS + g) * trows:(r * B_KV_HEADS + g + 1) * trows]
            outs.append(acc)
        gated = jnp.concatenate(outs, axis=1)[0:seq] * _silu(gate_ref[i])
        slabs = []
        for g in range(B_KV_HEADS):
            for j in range(B_GROUP // 2):
                a = gated[:, (4 * j + g // 2) * LANES:(4 * j + g // 2 + 1) * LANES]
                b = gated[:, (4 * j + 2 + g // 2) * LANES:(4 * j + 2 + g // 2 + 1) * LANES]
                slabs.append(_regroup_slab(a, b, g % 2, lo64_q))
        og_ref[i] = jnp.concatenate(slabs, axis=1)
        return carry

    lax.fori_loop(0, nb, per_seq, 0, unroll=4)


def _swa_sample(q, gate, k_new, v_new, ck, cv, sink_rows):
    nbatch, seq, _ = q.shape
    nb = min(SAMPLE_BATCH_BLOCK, nbatch)
    assert nbatch % nb == 0 and ck.shape[1] == WINDOW
    cache_spec = pl.BlockSpec((nb, WINDOW, B_KV_WIDTH), lambda b: (b, 0, 0))
    wide_spec = pl.BlockSpec((nb, seq, BRANCH_WIDTH), lambda b: (b, 0, 0))
    new_spec = pl.BlockSpec((nb, seq, B_KV_WIDTH), lambda b: (b, 0, 0))
    return pl.pallas_call(
        _swa_sample_kernel,
        grid=(nbatch // nb,),
        in_specs=[wide_spec, wide_spec, new_spec, new_spec, cache_spec, cache_spec,
                  pl.BlockSpec(sink_rows.shape, lambda b: (0, 0))],
        out_specs=[wide_spec, cache_spec, cache_spec],
        out_shape=[
            jax.ShapeDtypeStruct((nbatch, seq, BRANCH_WIDTH), F32),
            jax.ShapeDtypeStruct(ck.shape, F32),
            jax.ShapeDtypeStruct(cv.shape, F32),
        ],
        compiler_params=pltpu.CompilerParams(
            dimension_semantics=("arbitrary",), vmem_limit_bytes=VMEM_LIMIT_BYTES),
        name="swa_sample",
    )(q, gate, k_new, v_new, ck, cv, sink_rows)


def _rope_tables(first_pos, n_pos, repeat=1):
    half = B_HEAD_DIM // 2
    inv_freq = ROPE_THETA ** (-np.arange(half, dtype=np.float64) / half)
    ang = (first_pos + np.arange(n_pos, dtype=np.float64))[:, None] * inv_freq[None, :]
    cos, sin = np.cos(ang), np.sin(ang)
    cos_t = np.tile(np.concatenate([cos, cos, cos, cos], axis=1), (repeat, 1))
    sin_t = np.tile(np.concatenate([-sin, sin, -sin, sin], axis=1), (repeat, 1))
    return jnp.asarray(cos_t, F32), jnp.asarray(sin_t, F32)


def kernel(x_prompt, x_sample, state_gla, cache_swa_k, cache_swa_v, a_norm, a_w_in, a_w_gate2, a_b_gate,
           a_out_norm, a_w_out, kv_norm, w_k, w_v, k_norm, b_norm, b_w_in, b_q_norm, b_sinks, b_w_out):
    assert a_norm.shape[0] == 1 and b_norm.shape[0] == 1
    bsz_p, seq_p, d = x_prompt.shape
    bsz_s, seq_s, _ = x_sample.shape
    kw, bw = A_KEY_WIDTH, BRANCH_WIDTH

    w_in = a_w_in[0]
    wa_qk = w_in[:, :2 * kw].astype(BF16)
    wa_v = w_in[:, 2 * kw:2 * kw + bw].astype(BF16)
    wa_g = w_in[:, 2 * kw + bw:2 * kw + 2 * bw].astype(BF16)
    wa_low = jnp.pad(w_in[:, 2 * kw + 2 * bw:].astype(BF16), ((0, 0), (0, LANES - A_GATE_RANK)))
    wa_g2 = jnp.pad(a_w_gate2[0].astype(BF16), ((0, LANES - A_GATE_RANK), (0, 0)))
    a_bg = a_b_gate[0][None, :]
    a_n = a_norm[0][None, :]
    a_on = a_out_norm[0][None, :]
    wa_out = a_w_out[0].astype(BF16)
    w_kv = jnp.concatenate([w_k.astype(BF16), w_v.astype(BF16)], axis=1)
    wb_q = b_w_in[0][:, :bw].astype(BF16)
    wb_g = b_w_in[0][:, bw:].astype(BF16)
    wb_out = b_w_out[0].astype(BF16)
    kvn = kv_norm[None, :]
    bn = b_norm[0][None, :]
    kn_t = jnp.tile(k_norm, B_KV_WIDTH // B_HEAD_DIM)[None, :]
    qn_t = jnp.tile(b_q_norm[0], B_KV_WIDTH // B_HEAD_DIM)[None, :]
    sinks = b_sinks[0]
    grp = np.arange(B_KV_WIDTH) // B_HEAD_DIM
    bd = jnp.asarray(grp[:, None] == grp[None, :], BF16)

    cos_p, sin_p = _rope_tables(0, seq_p)
    h1_p, st_p = _gla_prompt(x_prompt, a_n, wa_qk, wa_v, wa_g, wa_low, wa_g2, a_bg, a_on, wa_out)
    y_p, kc_p, vc_p = _swa_prompt(h1_p, kvn, w_kv, kn_t, bd, cos_p, sin_p, bn, wb_q, wb_g, qn_t, sinks, wb_out)

    m = bsz_s * seq_s
    xs = x_sample.reshape(m, d)
    qk_a, v_a, gate_a, glow_a = _gla_proj_sample(xs, a_n, wa_qk, wa_v, wa_g, wa_low)
    o_s, st_s = _gla_sample(qk_a.reshape(bsz_s, seq_s, -1), v_a.reshape(bsz_s, seq_s, -1),
                            glow_a.reshape(bsz_s, seq_s, -1), wa_g2, a_bg, state_gla[0])
    h1_s = _gla_out(o_s.reshape(m, bw), gate_a, xs, a_on, wa_out)

    sink_rows = jnp.broadcast_to(
        sinks.reshape(B_KV_HEADS, B_GROUP).T[:, :, None, None], (B_GROUP, B_KV_HEADS, SUBLANES, LANES)
    ).reshape(B_GROUP * B_KV_HEADS * SUBLANES, LANES)
    cos_s, sin_s = _rope_tables(PAST_LEN, seq_s, repeat=bsz_s)
    k_new, v_new, q_s, gate_b = _swa_pre_sample(h1_s, kvn, w_kv, kn_t, bd, cos_s, sin_s, bn, wb_q, wb_g, qn_t)
    og_s, nk_s, nv_s = _swa_sample(
        q_s.reshape(bsz_s, seq_s, bw), gate_b.reshape(bsz_s, seq_s, bw),
        k_new.reshape(bsz_s, seq_s, B_KV_WIDTH), v_new.reshape(bsz_s, seq_s, B_KV_WIDTH),
        cache_swa_k.reshape(bsz_s, WINDOW, B_KV_WIDTH), cache_swa_v.reshape(bsz_s, WINDOW, B_KV_WIDTH), sink_rows)
    y_s = _out_proj(og_s.reshape(m, bw), h1_s, wb_out)

    return (y_p, y_s.reshape(bsz_s, seq_s, d),
            st_p[None], st_s[None],
            kc_p.reshape(bsz_p, WINDOW, B_KV_HEADS, B_HEAD_DIM), vc_p.reshape(bsz_p, WINDOW, B_KV_HEADS, B_HEAD_DIM),
            nk_s.reshape(bsz_s, WINDOW, B_KV_HEADS, B_HEAD_DIM), nv_s.reshape(bsz_s, WINDOW, B_KV_HEADS, B_HEAD_DIM))
```

```python
import functools

import jax
import jax.numpy as jnp
import numpy as np
from jax import lax
from jax.experimental import pallas as pl
from jax.experimental.pallas import tpu as pltpu

F32 = jnp.float32
BF16 = jnp.bfloat16

A_HEADS = 4
A_HEAD_DK = 128
A_HEAD_DV = 512
A_KEY_WIDTH = A_HEADS * A_HEAD_DK
BRANCH_WIDTH = A_HEADS * A_HEAD_DV
A_GATE_RANK = 16
A_GATE_NORMALIZER = 16.0
B_HEAD_DIM = 64
B_HEADS = BRANCH_WIDTH // B_HEAD_DIM
B_KV_HEADS = 4
B_GROUP = B_HEADS // B_KV_HEADS
B_KV_WIDTH = B_KV_HEADS * B_HEAD_DIM
WINDOW = 128
ROPE_THETA = 10000.0
RMS_EPS = 1e-6
PAST_LEN = 16384

LANES = 128
SUBLANES = 8
VMEM_LIMIT_BYTES = 56 * 1024 * 1024

PROMPT_TILE = 512
GLA_CHUNK = 256
GLA_SAFE_DECAY = 80.0
SAMPLE_BATCH_BLOCK = 8
OUT_PROJ_CHUNKS = 4
LOG2E = 1.4426950408889634


def _dot(a, b):
    return jnp.dot(a, b, preferred_element_type=F32)


def _dot_nt(a, b):
    return lax.dot_general(a, b, (((1,), (1,)), ((), ())), preferred_element_type=F32)


def _split2(x):
    hi = x.astype(BF16)
    lo = (x - hi.astype(F32)).astype(BF16)
    return hi, lo


def _rmsnorm(x, g):
    ms = jnp.mean(x * x, axis=-1, keepdims=True)
    return x * lax.rsqrt(ms + RMS_EPS) * g


def _log_sigmoid(x):
    return jnp.minimum(x, 0.0) - jnp.log1p(jnp.exp(-jnp.abs(x)))


def _silu(x):
    return x * (1.0 / (1.0 + jnp.exp(-x)))


def _head_norm(x, bd, gain):
    hi, lo = _split2(x * x)
    ss = _dot(hi, bd) + _dot(lo, bd)
    return x * lax.rsqrt(ss * (1.0 / B_HEAD_DIM) + RMS_EPS) * gain


def _rope_slab(x, cos, sin_signed, first_half):
    xr = jnp.where(first_half, pltpu.roll(x, 96, 1), pltpu.roll(x, 32, 1))
    return x * cos + xr * sin_signed


def _lane_masks(rows):
    lane = lax.broadcasted_iota(jnp.int32, (rows, LANES), 1)
    first_half = (lane % B_HEAD_DIM) < (B_HEAD_DIM // 2)
    lo64 = lane < B_HEAD_DIM
    return first_half, lo64


def _split_group(slab, g, lo64):
    swapped = pltpu.roll(slab, B_HEAD_DIM, 1)
    zero = jnp.zeros_like(slab)
    if g % 2 == 0:
        return jnp.where(lo64, slab, zero), jnp.where(lo64, zero, swapped)
    return jnp.where(lo64, swapped, zero), jnp.where(lo64, zero, slab)


def _regroup_slab(a, b, odd, lo64):
    if odd:
        return jnp.where(lo64, pltpu.roll(a, B_HEAD_DIM, 1), b)
    return jnp.where(lo64, a, pltpu.roll(b, B_HEAD_DIM, 1))


def _gla_prompt_kernel(x_ref, an_ref, wqk_ref, wv_ref, wg_ref, wlow_ref, wg2_ref, bg_ref, on_ref, wout_ref,
                       h1_ref, st_ref,
                       qk_s, v_s, gate_s, b_s, o_s, oi_s):
    tile = x_ref.shape[0]
    chunk = GLA_CHUNK
    n_chunks = tile // chunk
    dk, dv, kw = A_HEAD_DK, A_HEAD_DV, A_KEY_WIDTH

    @pl.when(pl.program_id(1) == 0)
    def _():
        st_ref[...] = jnp.zeros_like(st_ref)

    x = x_ref[...]
    u = _rmsnorm(x, an_ref[...]).astype(BF16)
    qk_s[...] = _dot(u, wqk_ref[...])
    v_s[...] = _dot(u, wv_ref[...])
    gate_s[...] = _dot(u, wg_ref[...])
    glow = _dot(u, wlow_ref[...]).astype(BF16)
    pre = _dot(glow, wg2_ref[...]) + bg_ref[...]
    loga = _log_sigmoid(pre) * (1.0 / A_GATE_NORMALIZER)

    row = lax.broadcasted_iota(jnp.int32, (chunk, chunk), 0)
    col = lax.broadcasted_iota(jnp.int32, (chunk, chunk), 1)
    lower = row >= col
    tri = jnp.where(lower, 1.0, 0.0).astype(BF16)
    for c in range(n_chunks):
        hi, lo = _split2(loga[c * chunk:(c + 1) * chunk])
        b_s[c * chunk:(c + 1) * chunk, :] = _dot(tri, hi) + _dot(tri, lo)
    safe = jnp.min(b_s[...]) >= -GLA_SAFE_DECAY

    scale = A_HEAD_DK ** -0.5
    for c in range(n_chunks):
        rows = slice(c * chunk, (c + 1) * chunk)
        b = b_s[rows, :]
        eb = jnp.exp(b)
        enb = jnp.exp(-b)
        blast = b[chunk - 1:chunk, :]
        ekl = jnp.exp(blast - b)
        eblast = jnp.exp(blast)
        for h in range(A_HEADS):
            ks = slice(h * dk, (h + 1) * dk)
            vs = slice(h * dv, (h + 1) * dv)
            q = qk_s[rows, ks] * scale
            k = qk_s[rows, kw + h * dk: kw + (h + 1) * dk]
            vh = v_s[rows, vs].astype(BF16)
            state = st_ref[h]
            qt = (q * eb[:, ks]).astype(BF16)
            o_inter = _dot(qt, state.astype(BF16))
            a = _dot_nt(qt, (k * enb[:, ks]).astype(BF16))
            a = jnp.where(lower, a, 0.0).astype(BF16)
            oi_s[rows, vs] = o_inter
            o_s[rows, vs] = o_inter + _dot(a, vh)
            kd_t = jnp.transpose(k * ekl[:, ks]).astype(BF16)
            dec = jnp.transpose(jnp.broadcast_to(eblast[:, ks], (dk, dk)))
            st_ref[h] = state * jnp.concatenate([dec] * (dv // dk), axis=1) + _dot(kd_t, vh)

    @pl.when(jnp.logical_not(safe))
    def _():
        trow = lax.broadcasted_iota(jnp.int32, (chunk, dk), 0)
        sub8 = lax.broadcasted_iota(jnp.int32, (SUBLANES, 1), 0)
        for c in range(n_chunks):
            rows = slice(c * chunk, (c + 1) * chunk)
            for h in range(A_HEADS):
                ks = slice(h * dk, (h + 1) * dk)
                vs = slice(h * dv, (h + 1) * dv)
                q = qk_s[rows, ks] * scale
                b = b_s[rows, ks]

                def body(s, acc, c=c, h=h, q=q, b=b):
                    r8 = pl.multiple_of(c * chunk + (s // SUBLANES) * SUBLANES, SUBLANES)
                    pick = sub8 == s % SUBLANES

                    def row_of(ref, lanes):
                        return jnp.sum(jnp.where(pick, ref[pl.ds(r8, SUBLANES), lanes], 0.0), axis=0, keepdims=True)

                    brow = row_of(b_s, slice(h * dk, (h + 1) * dk))
                    krow = row_of(qk_s, slice(kw + h * dk, kw + (h + 1) * dk))
                    vrow = row_of(v_s, slice(h * dv, (h + 1) * dv))
                    w = jnp.where(trow >= s, jnp.exp(jnp.minimum(b - brow, 0.0)), 0.0)
                    colv = jnp.sum(q * krow * w, axis=-1, keepdims=True)
                    return acc + colv * vrow

                o_s[rows, vs] = oi_s[rows, vs] + lax.fori_loop(0, chunk, body, jnp.zeros((chunk, dv), F32))

    y = x
    for h in range(A_HEADS):
        vs = slice(h * dv, (h + 1) * dv)
        o = _rmsnorm(o_s[:, vs], on_ref[...])
        og = (o * _silu(gate_s[:, vs])).astype(BF16)
        y = y + _dot(og, wout_ref[vs, :])
    h1_ref[...] = y


def _gla_prompt(x, a_norm, wqk, wv, wg, wlow, wg2, bg, out_norm, wout):
    bsz, seq, d = x.shape
    tile = min(PROMPT_TILE, seq)
    assert seq % tile == 0 and tile % GLA_CHUNK == 0
    const = lambda shape: pl.BlockSpec(shape, lambda b, l: (0,) * len(shape), pipeline_mode=pl.Buffered(1))
    return pl.pallas_call(
        _gla_prompt_kernel,
        grid=(bsz, seq // tile),
        in_specs=[
            pl.BlockSpec((None, tile, d), lambda b, l: (b, l, 0)),
            const(a_norm.shape), const(wqk.shape), const(wv.shape), const(wg.shape), const(wlow.shape),
            const(wg2.shape), const(bg.shape), const(out_norm.shape), const(wout.shape),
        ],
        out_specs=[
            pl.BlockSpec((None, tile, d), lambda b, l: (b, l, 0)),
            pl.BlockSpec((None, A_HEADS, A_HEAD_DK, A_HEAD_DV), lambda b, l: (b, 0, 0, 0)),
        ],
        out_shape=[
            jax.ShapeDtypeStruct((bsz, seq, d), F32),
            jax.ShapeDtypeStruct((bsz, A_HEADS, A_HEAD_DK, A_HEAD_DV), F32),
        ],
        scratch_shapes=[
            pltpu.VMEM((tile, 2 * A_KEY_WIDTH), F32),
            pltpu.VMEM((tile, BRANCH_WIDTH), F32),
            pltpu.VMEM((tile, BRANCH_WIDTH), F32),
            pltpu.VMEM((tile, A_KEY_WIDTH), F32),
            pltpu.VMEM((tile, BRANCH_WIDTH), F32),
            pltpu.VMEM((tile, BRANCH_WIDTH), F32),
        ],
        compiler_params=pltpu.CompilerParams(
            dimension_semantics=("arbitrary", "arbitrary"), vmem_limit_bytes=VMEM_LIMIT_BYTES),
        name="gla_prompt",
    )(x, a_norm, wqk, wv, wg, wlow, wg2, bg, out_norm, wout)


def _swa_prompt_kernel(h_ref, kvn_ref, wkv_ref, kn_ref, bd_ref, cos_ref, sin_ref, bn_ref, wq_ref, wg_ref,
                       qn_ref, sink_ref, wout_ref,
                       y_ref, kc_ref, vc_ref,
                       ktop_s, kbot_s, vtop_s, vbot_s, qb_s, ub_s, kv_s, q_s, gsl_s, osl_s, og_s):
    tile = h_ref.shape[0]
    l = pl.program_id(1)
    blk = WINDOW
    kvw = B_KV_WIDTH
    pairs = B_GROUP // 2
    n_slabs = BRANCH_WIDTH // LANES
    n_iter = (tile // blk) * B_KV_HEADS
    gate_cols = wg_ref.shape[2]

    @pl.when(l == 0)
    def _():
        zeros = jnp.zeros((B_KV_HEADS, WINDOW, LANES), BF16)
        ktop_s[:, 0:WINDOW, :] = zeros
        kbot_s[:, 0:WINDOW, :] = zeros
        vtop_s[:, 0:WINDOW, 0:LANES] = zeros
        vbot_s[:, 0:WINDOW, 0:LANES] = zeros
        lane = lax.broadcasted_iota(jnp.int32, (B_KV_HEADS, WINDOW + tile, LANES), 2)
        vtop_s[:, :, LANES:] = jnp.where(lane < B_HEAD_DIM, 1.0, 0.0).astype(BF16)
        vbot_s[:, :, LANES:] = jnp.where(lane < B_HEAD_DIM, 0.0, 1.0).astype(BF16)

    h = h_ref[...]
    cos = cos_ref[...]
    sin = sin_ref[...]
    first_half, lo64 = _lane_masks(tile)
    bd = bd_ref[...]

    kv_s[...] = _dot(_rmsnorm(h, kvn_ref[...]).astype(BF16), wkv_ref[...])
    ub_s[...] = _rmsnorm(h, bn_ref[...]).astype(BF16)

    kn = _head_norm(kv_s[:, :kvw], bd, kn_ref[...])
    v = kv_s[:, kvw:]
    k_slabs = [_rope_slab(kn[:, s * LANES:(s + 1) * LANES], cos, sin, first_half) for s in range(kvw // LANES)]

    @pl.when(l == pl.num_programs(1) - 1)
    def _():
        kc_ref[...] = jnp.concatenate(k_slabs, axis=1)[tile - WINDOW:, :]
        vc_ref[...] = v[tile - WINDOW:, :]

    for g in range(B_KV_HEADS):
        s = g // 2
        top, bot = _split_group(k_slabs[s], g, lo64)
        ktop_s[g, WINDOW:WINDOW + tile, :] = top.astype(BF16)
        kbot_s[g, WINDOW:WINDOW + tile, :] = bot.astype(BF16)
        top, bot = _split_group(v[:, s * LANES:(s + 1) * LANES], g, lo64)
        vtop_s[g, WINDOW:WINDOW + tile, 0:LANES] = top.astype(BF16)
        vbot_s[g, WINDOW:WINDOW + tile, 0:LANES] = bot.astype(BF16)

    qscale = qn_ref[...] * (B_HEAD_DIM ** -0.5 * LOG2E)
    qcos = [cos * qscale[:, s * LANES:(s + 1) * LANES] for s in range(kvw // LANES)]
    qsin = [sin * jnp.where(first_half[0:1], pltpu.roll(qscale[:, s * LANES:(s + 1) * LANES], 96, 1),
                            pltpu.roll(qscale[:, s * LANES:(s + 1) * LANES], 32, 1)) for s in range(kvw // LANES)]

    q_s[...] = _dot(ub_s[...], wq_ref[...])
    for c in range(BRANCH_WIDTH // kvw):
        x = q_s[:, c * kvw:(c + 1) * kvw]
        hi, lo = _split2(x * x)
        rinv = lax.rsqrt((_dot(hi, bd) + _dot(lo, bd)) * (1.0 / B_HEAD_DIM) + RMS_EPS)
        for s in range(kvw // LANES):
            xs = x[:, s * LANES:(s + 1) * LANES]
            xr = jnp.where(first_half, pltpu.roll(xs, 96, 1), pltpu.roll(xs, 32, 1))
            qr = ((xs * qcos[s] + xr * qsin[s]) * rinv[:, s * LANES:(s + 1) * LANES]).astype(BF16)
            for i in range(tile // blk):
                qb_s[i, c * (kvw // LANES) + s] = qr[i * blk:(i + 1) * blk]

    rowi = lax.broadcasted_iota(jnp.int32, (blk, 2 * blk), 0)
    coli = lax.broadcasted_iota(jnp.int32, (blk, 2 * blk), 1)
    band = jnp.logical_and(coli - rowi >= 0, coli - rowi <= WINDOW)
    _, lo64_b = _lane_masks(blk)

    def attn_pair(idx):
        i = idx // B_KV_HEADS
        g = idx % B_KV_HEADS
        r0 = pl.multiple_of(i * blk, blk)
        first_col = jnp.where(l * tile + r0 > 0, 0, WINDOW)
        mask = jnp.logical_and(band, coli >= first_col)
        kpad = jnp.concatenate([ktop_s[g, pl.ds(r0, 2 * blk), :], kbot_s[g, pl.ds(r0, 2 * blk), :]], axis=0)
        vpad = jnp.concatenate([vtop_s[g, pl.ds(r0, 2 * blk), :], vbot_s[g, pl.ds(r0, 2 * blk), :]], axis=0)
        qg = qb_s[i, pl.ds(g * pairs, pairs)].reshape(pairs * blk, LANES)
        s = _dot_nt(qg, kpad)
        ps, esinks = [], []
        for j in range(pairs):
            pj, ej = [], []
            for half in range(2):
                sink = sink_ref[g * B_GROUP + 2 * j + half] * LOG2E
                sh = jnp.where(mask, s[j * blk:(j + 1) * blk, half * 2 * blk:(half + 1) * 2 * blk], -jnp.inf)
                m = jnp.maximum(jnp.max(sh, axis=-1, keepdims=True), sink)
                pj.append(jnp.exp2(sh - m).astype(BF16))
                ej.append(jnp.exp2(sink - m))
            ps.append(jnp.concatenate(pj, axis=1))
            esinks.append(jnp.where(lo64_b, ej[0], ej[1]))
        o = _dot(jnp.concatenate(ps, axis=0), vpad)
        for j in range(pairs):
            oj = o[j * blk:(j + 1) * blk]
            osl_s[g * pairs + j, pl.ds(r0, blk), :] = oj[:, :LANES] / (oj[:, LANES:] + esinks[j])

    def attn_body(k, carry):
        gc = _dot(ub_s[...], wg_ref[k])
        for s in range(gate_cols // LANES):
            gsl_s[k * (gate_cols // LANES) + s] = gc[:, s * LANES:(s + 1) * LANES]
        attn_pair(2 * k)
        attn_pair(2 * k + 1)
        return carry

    lax.fori_loop(0, n_iter // 2, attn_body, 0, unroll=4)

    for buf in (ktop_s, kbot_s):
        buf[:, 0:WINDOW, :] = buf[:, tile:tile + WINDOW, :]
    for buf in (vtop_s, vbot_s):
        buf[:, 0:WINDOW, 0:LANES] = buf[:, tile:tile + WINDOW, 0:LANES]

    y = h
    per = n_slabs // OUT_PROJ_CHUNKS
    for c in range(OUT_PROJ_CHUNKS):
        for sl in range(c * per, (c + 1) * per):
            og_s[:, sl * LANES:(sl + 1) * LANES] = (osl_s[sl] * _silu(gsl_s[sl])).astype(BF16)
        cols = slice(c * per * LANES, (c + 1) * per * LANES)
        y = y + _dot(og_s[:, cols], wout_ref[cols, :])
    y_ref[...] = y


def _swa_prompt(h, kv_norm, wkv, k_norm, bd, cos, sin, b_norm, wq, wg, q_norm, sinks, wout):
    bsz, seq, d = h.shape
    tile = min(PROMPT_TILE, seq)
    assert seq % tile == 0 and tile % WINDOW == 0 and seq >= WINDOW
    n_iter = (tile // WINDOW) * B_KV_HEADS
    gate_cols = BRANCH_WIDTH // (n_iter // 2)
    assert gate_cols % LANES == 0
    wg = wg.reshape(d, n_iter // 2, gate_cols).transpose(1, 0, 2)
    const = lambda shape: pl.BlockSpec(shape, lambda b, l: (0,) * len(shape), pipeline_mode=pl.Buffered(1))
    k_scratch = pltpu.VMEM((B_KV_HEADS, WINDOW + tile, LANES), BF16)
    v_scratch = pltpu.VMEM((B_KV_HEADS, WINDOW + tile, 2 * LANES), BF16)
    n_slabs = BRANCH_WIDTH // LANES
    return pl.pallas_call(
        _swa_prompt_kernel,
        grid=(bsz, seq // tile),
        in_specs=[
            pl.BlockSpec((None, tile, d), lambda b, l: (b, l, 0)),
            const(kv_norm.shape), const(wkv.shape), const(k_norm.shape), const(bd.shape),
            pl.BlockSpec((tile, LANES), lambda b, l: (l, 0)),
            pl.BlockSpec((tile, LANES), lambda b, l: (l, 0)),
            const(b_norm.shape), const(wq.shape), const(wg.shape), const(q_norm.shape),
            pl.BlockSpec(memory_space=pltpu.SMEM),
            const(wout.shape),
        ],
        out_specs=[
            pl.BlockSpec((None, tile, d), lambda b, l: (b, l, 0)),
            pl.BlockSpec((None, WINDOW, B_KV_WIDTH), lambda b, l: (b, 0, 0)),
            pl.BlockSpec((None, WINDOW, B_KV_WIDTH), lambda b, l: (b, 0, 0)),
        ],
        out_shape=[
            jax.ShapeDtypeStruct((bsz, seq, d), F32),
            jax.ShapeDtypeStruct((bsz, WINDOW, B_KV_WIDTH), F32),
            jax.ShapeDtypeStruct((bsz, WINDOW, B_KV_WIDTH), F32),
        ],
        scratch_shapes=[
            k_scratch, k_scratch, v_scratch, v_scratch,
            pltpu.VMEM((tile // WINDOW, n_slabs, WINDOW, LANES), BF16),
            pltpu.VMEM((tile, d), BF16),
            pltpu.VMEM((tile, 2 * B_KV_WIDTH), F32),
            pltpu.VMEM((tile, BRANCH_WIDTH), F32),
            pltpu.VMEM((n_slabs, tile, LANES), F32),
            pltpu.VMEM((n_slabs, tile, LANES), F32),
            pltpu.VMEM((tile, BRANCH_WIDTH), BF16),
        ],
        compiler_params=pltpu.CompilerParams(
            dimension_semantics=("arbitrary", "arbitrary"), vmem_limit_bytes=VMEM_LIMIT_BYTES),
        name="swa_prompt",
    )(h, kv_norm, wkv, k_norm, bd, cos, sin, b_norm, wq, wg, q_norm, sinks, wout)


def _gla_proj_sample_kernel(x_ref, g_ref, wqk_ref, wv_ref, wg_ref, wlow_ref, qk_ref, v_ref, gate_ref, glow_ref):
    u = _rmsnorm(x_ref[...], g_ref[...]).astype(BF16)
    qk_ref[...] = _dot(u, wqk_ref[...])
    v_ref[...] = _dot(u, wv_ref[...])
    gate_ref[...] = _dot(u, wg_ref[...])
    glow_ref[...] = _dot(u, wlow_ref[...])


def _gla_proj_sample(x, g, wqk, wv, wg, wlow):
    m = x.shape[0]
    return pl.pallas_call(
        _gla_proj_sample_kernel,
        out_shape=[jax.ShapeDtypeStruct((m, w.shape[1]), F32) for w in (wqk, wv, wg, wlow)],
        compiler_params=pltpu.CompilerParams(vmem_limit_bytes=VMEM_LIMIT_BYTES),
        name="gla_proj_sample",
    )(x, g, wqk, wv, wg, wlow)


def _gla_out_kernel(o_ref, gate_ref, x_ref, on_ref, w_ref, y_ref, og_s):
    for h in range(A_HEADS):
        vs = slice(h * A_HEAD_DV, (h + 1) * A_HEAD_DV)
        o = _rmsnorm(o_ref[:, vs], on_ref[...])
        og_s[:, vs] = (o * _silu(gate_ref[:, vs])).astype(BF16)
    y_ref[...] = _dot(og_s[...], w_ref[...]) + x_ref[...]


def _gla_out(o, gate, x, out_norm, wout):
    m = o.shape[0]
    return pl.pallas_call(
        _gla_out_kernel,
        out_shape=jax.ShapeDtypeStruct(x.shape, F32),
        scratch_shapes=[pltpu.VMEM((m, BRANCH_WIDTH), BF16)],
        compiler_params=pltpu.CompilerParams(vmem_limit_bytes=VMEM_LIMIT_BYTES),
        name="gla_out_sample",
    )(o, gate, x, out_norm, wout)


def _out_proj_kernel(og_ref, x_ref, w_ref, y_ref):
    y_ref[...] = _dot(og_ref[...].astype(BF16), w_ref[...]) + x_ref[...]


def _out_proj(og, x, wout):
    return pl.pallas_call(
        _out_proj_kernel,
        out_shape=jax.ShapeDtypeStruct(x.shape, F32),
        compiler_params=pltpu.CompilerParams(vmem_limit_bytes=VMEM_LIMIT_BYTES),
        name="swa_out_sample",
    )(og, x, wout)


def _gla_sample_kernel(qk_ref, v_ref, glow_ref, wg2_ref, bg_ref, st_ref, o_ref, nst_ref, *, seq):
    nb = st_ref.shape[0]
    dk, dv, kw = A_HEAD_DK, A_HEAD_DV, A_KEY_WIDTH
    scale = A_HEAD_DK ** -0.5
    per_tile = SUBLANES // seq

    def one_seq(i, qk, v, loga):
        q = qk[:, 0:kw] * scale
        k = qk[:, kw:2 * kw]
        bs = [loga[0:1]]
        for t in range(1, seq):
            bs.append(bs[-1] + loga[t:t + 1])
        bmat = jnp.concatenate(bs, axis=0)
        blast = bs[-1]
        qt = q * jnp.exp(bmat)
        kd = k * jnp.exp(blast - bmat)
        eblast = jnp.exp(blast)

        outs = [[jnp.zeros((1, dv), F32) for _ in range(A_HEADS)] for _ in range(seq)]
        for t in range(seq):
            for s in range(t + 1):
                w = q[t:t + 1] * k[s:s + 1] * jnp.exp(bs[t] - bs[s])
                for h in range(A_HEADS):
                    a = jnp.sum(w[:, h * dk:(h + 1) * dk], axis=-1, keepdims=True)
                    outs[t][h] = outs[t][h] + a * v[s:s + 1, h * dv:(h + 1) * dv]
        intra = jnp.concatenate([jnp.concatenate(outs[t], axis=1) for t in range(seq)], axis=0)

        qt8 = jnp.concatenate([qt, jnp.zeros((SUBLANES - seq, kw), F32)], axis=0).astype(BF16)
        inter = []
        for h in range(A_HEADS):
            ks = slice(h * dk, (h + 1) * dk)
            state = st_ref[i, h]
            inter.append(_dot(qt8[:, ks], state.astype(BF16))[0:seq])
            m = jnp.concatenate([kd[:, ks], eblast[:, ks], jnp.zeros((dk - seq - 1, dk), F32)], axis=0)
            mt = jnp.transpose(m)
            vpad = jnp.concatenate([v[:, h * dv:(h + 1) * dv], jnp.zeros((dk - seq, dv), F32)], axis=0)
            nst_ref[i, h] = state * mt[:, seq:seq + 1] + _dot(mt.astype(BF16), vpad.astype(BF16))
        return intra + jnp.concatenate(inter, axis=1)

    def per_tile_body(j, carry):
        rows = pl.ds(pl.multiple_of(j * SUBLANES, SUBLANES), SUBLANES)
        qk8, v8 = qk_ref[rows, :], v_ref[rows, :]
        pre = _dot(glow_ref[rows, :].astype(BF16), wg2_ref[...]) + bg_ref[...]
        loga8 = _log_sigmoid(pre) * (1.0 / A_GATE_NORMALIZER)
        outs = []
        for p in range(per_tile):
            r = slice(p * seq, (p + 1) * seq)
            outs.append(one_seq(j * per_tile + p, qk8[r], v8[r], loga8[r]))
        o_ref[rows, :] = jnp.concatenate(outs, axis=0)
        return carry

    lax.fori_loop(0, nb // per_tile, per_tile_body, 0)


def _gla_sample(qk, v, glow, wg2, bg, state, seq):
    nbatch = state.shape[0]
    nb = min(SAMPLE_BATCH_BLOCK // 2, nbatch)
    assert nbatch % nb == 0 and SUBLANES % seq == 0 and (nb * seq) % SUBLANES == 0
    st_spec = pl.BlockSpec((nb, A_HEADS, A_HEAD_DK, A_HEAD_DV), lambda b: (b, 0, 0, 0))
    tok_spec = lambda a: pl.BlockSpec((nb * seq, a.shape[1]), lambda b: (b, 0))
    return pl.pallas_call(
        functools.partial(_gla_sample_kernel, seq=seq),
        grid=(nbatch // nb,),
        in_specs=[
            tok_spec(qk), tok_spec(v), tok_spec(glow),
            pl.BlockSpec(wg2.shape, lambda b: (0, 0)),
            pl.BlockSpec(bg.shape, lambda b: (0, 0)),
            st_spec,
        ],
        out_specs=[pl.BlockSpec((nb * seq, BRANCH_WIDTH), lambda b: (b, 0)), st_spec],
        out_shape=[
            jax.ShapeDtypeStruct((nbatch * seq, BRANCH_WIDTH), F32),
            jax.ShapeDtypeStruct(state.shape, F32),
        ],
        compiler_params=pltpu.CompilerParams(
            dimension_semantics=("arbitrary",), vmem_limit_bytes=VMEM_LIMIT_BYTES),
        name="gla_sample",
    )(qk, v, glow, wg2, bg, state)


def _swa_pre_sample_kernel(h_ref, kvn_ref, wkv_ref, kn_ref, bd_ref, cos_ref, sin_ref, bn_ref, wq_ref, wg_ref, qn_ref,
                           k_ref, v_ref, q_ref, gate_ref):
    m = h_ref.shape[0]
    kvw = B_KV_WIDTH
    h = h_ref[...]
    cos = cos_ref[...]
    sin = sin_ref[...]
    first_half, _ = _lane_masks(m)
    bd = bd_ref[...]

    kv = _dot(_rmsnorm(h, kvn_ref[...]).astype(BF16), wkv_ref[...])
    kn = _head_norm(kv[:, :kvw], bd, kn_ref[...])
    for s in range(kvw // LANES):
        k_ref[:, s * LANES:(s + 1) * LANES] = _rope_slab(kn[:, s * LANES:(s + 1) * LANES], cos, sin, first_half)
    v_ref[...] = kv[:, kvw:]

    ub = _rmsnorm(h, bn_ref[...]).astype(BF16)
    _, lo64 = _lane_masks(m)
    qgain = qn_ref[...] * (B_HEAD_DIM ** -0.5 * LOG2E)
    gate = _dot(ub, wg_ref[...])
    q_slabs, g_slabs = [], []
    for c in range(BRANCH_WIDTH // kvw):
        qc = _head_norm(_dot(ub, wq_ref[:, c * kvw:(c + 1) * kvw]), bd, qgain)
        for s in range(kvw // LANES):
            q_slabs.append(_rope_slab(qc[:, s * LANES:(s + 1) * LANES], cos, sin, first_half))
            g_slabs.append(gate[:, c * kvw + s * LANES:c * kvw + (s + 1) * LANES])
    per_group = B_GROUP // 2
    for r in range(B_GROUP):
        for p in range(B_KV_HEADS // 2):
            a = 2 * p * per_group + r // 2
            b = (2 * p + 1) * per_group + r // 2
            lanes = slice((2 * r + p) * LANES, (2 * r + p + 1) * LANES)
            q_ref[:, lanes] = _regroup_slab(q_slabs[a], q_slabs[b], r % 2, lo64)
            gate_ref[:, lanes] = _regroup_slab(g_slabs[a], g_slabs[b], r % 2, lo64)


def _swa_pre_sample(h, kv_norm, wkv, k_norm, bd, cos, sin, b_norm, wq, wg, q_norm):
    m = h.shape[0]
    return pl.pallas_call(
        _swa_pre_sample_kernel,
        out_shape=[
            jax.ShapeDtypeStruct((m, B_KV_WIDTH), F32),
            jax.ShapeDtypeStruct((m, B_KV_WIDTH), F32),
            jax.ShapeDtypeStruct((m, BRANCH_WIDTH), F32),
            jax.ShapeDtypeStruct((m, BRANCH_WIDTH), F32),
        ],
        compiler_params=pltpu.CompilerParams(vmem_limit_bytes=VMEM_LIMIT_BYTES),
        name="swa_pre_sample",
    )(h, kv_norm, wkv, k_norm, bd, cos, sin, b_norm, wq, wg, q_norm)


def _swa_sample_kernel(q_ref, gate_ref, kn_ref, vn_ref, ck_ref, cv_ref, sink_ref, og_ref, nk_ref, nv_ref, *, seq):
    nb = ck_ref.shape[0]
    per_tile = SUBLANES // seq
    kvw = B_KV_WIDTH
    trows = SUBLANES
    nrows = B_GROUP * B_KV_HEADS * trows
    rowi = lax.broadcasted_iota(jnp.int32, (nrows, 2 * WINDOW), 0)
    coli = lax.broadcasted_iota(jnp.int32, (nrows, 2 * WINDOW), 1)
    dpos = coli - rowi % trows
    mask = jnp.logical_and(dpos >= 0, dpos <= WINDOW)
    lane8 = lax.broadcasted_iota(jnp.int32, (trows, kvw), 1) // B_HEAD_DIM
    grow = (lax.broadcasted_iota(jnp.int32, (nrows, kvw), 0) // trows) % B_KV_HEADS
    gkeep = grow == lax.broadcasted_iota(jnp.int32, (nrows, kvw), 1) // B_HEAD_DIM
    sink = sink_ref[...][:, 0:1] * LOG2E
    zq = jnp.zeros((trows - seq, BRANCH_WIDTH), F32)
    lo64_q = lax.broadcasted_iota(jnp.int32, (seq, LANES), 1) < B_HEAD_DIM
    zk = jnp.zeros((WINDOW - seq, kvw), F32)

    def one_seq(i, q4, gate4, k_new, v_new):
        ck = ck_ref[i]
        cv = cv_ref[i]
        nk_ref[i, 0:WINDOW - seq, :] = ck[seq:, :]
        nk_ref[i, WINDOW - seq:WINDOW, :] = k_new
        nv_ref[i, 0:WINDOW - seq, :] = cv[seq:, :]
        nv_ref[i, WINDOW - seq:WINDOW, :] = v_new

        q8 = jnp.concatenate([q4, zq], axis=0)
        pieces = []
        for r in range(B_GROUP):
            slab = q8[:, r * kvw:(r + 1) * kvw]
            for g in range(B_KV_HEADS):
                pieces.append(jnp.where(lane8 == g, slab, 0.0))
        qrows = jnp.concatenate(pieces, axis=0).astype(BF16)
        kpad = jnp.concatenate([k_new, zk], axis=0).astype(BF16)
        vpad = jnp.concatenate([v_new, zk], axis=0).astype(BF16)
        s = jnp.concatenate([_dot_nt(qrows, ck.astype(BF16)), _dot_nt(qrows, kpad)], axis=1)
        s = jnp.where(mask, s, -jnp.inf)
        m = jnp.maximum(jnp.max(s, axis=-1, keepdims=True), sink)
        p = jnp.exp2(s - m)
        den = jnp.sum(p, axis=-1, keepdims=True) + jnp.exp2(sink - m)
        pb = p.astype(BF16)
        o = _dot(pb[:, :WINDOW], cv.astype(BF16)) + _dot(pb[:, WINDOW:], vpad)
        o = jnp.where(gkeep, o / den, 0.0)
        outs = []
        for r in range(B_GROUP):
            acc = o[r * B_KV_HEADS * trows:r * B_KV_HEADS * trows + trows]
            for g in range(1, B_KV_HEADS):
                acc = acc + o[(r * B_KV_HEADS + g) * trows:(r * B_KV_HEADS + g + 1) * trows]
            outs.append(acc)
        gated = jnp.concatenate(outs, axis=1)[0:seq] * _silu(gate4)
        slabs = []
        for g in range(B_KV_HEADS):
            for j in range(B_GROUP // 2):
                a = gated[:, (4 * j + g // 2) * LANES:(4 * j + g // 2 + 1) * LANES]
                b = gated[:, (4 * j + 2 + g // 2) * LANES:(4 * j + 2 + g // 2 + 1) * LANES]
                slabs.append(_regroup_slab(a, b, g % 2, lo64_q))
        return jnp.concatenate(slabs, axis=1)

    def per_tile_body(j, carry):
        rows = pl.ds(pl.multiple_of(j * SUBLANES, SUBLANES), SUBLANES)
        q8, gate8, kn8, vn8 = q_ref[rows, :], gate_ref[rows, :], kn_ref[rows, :], vn_ref[rows, :]
        outs = []
        for p in range(per_tile):
            r = slice(p * seq, (p + 1) * seq)
            outs.append(one_seq(j * per_tile + p, q8[r], gate8[r], kn8[r], vn8[r]))
        og_ref[rows, :] = jnp.concatenate(outs, axis=0)
        return carry

    lax.fori_loop(0, nb // per_tile, per_tile_body, 0, unroll=2)


def _swa_sample(q, gate, k_new, v_new, ck, cv, sink_rows, seq):
    nbatch = ck.shape[0]
    nb = min(SAMPLE_BATCH_BLOCK, nbatch)
    assert nbatch % nb == 0 and ck.shape[1] == WINDOW and SUBLANES % seq == 0 and (nb * seq) % SUBLANES == 0
    cache_spec = pl.BlockSpec((nb, WINDOW, B_KV_WIDTH), lambda b: (b, 0, 0))
    wide_spec = pl.BlockSpec((nb * seq, BRANCH_WIDTH), lambda b: (b, 0))
    new_spec = pl.BlockSpec((nb * seq, B_KV_WIDTH), lambda b: (b, 0))
    return pl.pallas_call(
        functools.partial(_swa_sample_kernel, seq=seq),
        grid=(nbatch // nb,),
        in_specs=[wide_spec, wide_spec, new_spec, new_spec, cache_spec, cache_spec,
                  pl.BlockSpec(sink_rows.shape, lambda b: (0, 0))],
        out_specs=[wide_spec, cache_spec, cache_spec],
        out_shape=[
            jax.ShapeDtypeStruct((nbatch * seq, BRANCH_WIDTH), F32),
            jax.ShapeDtypeStruct(ck.shape, F32),
            jax.ShapeDtypeStruct(cv.shape, F32),
        ],
        compiler_params=pltpu.CompilerParams(
            dimension_semantics=("arbitrary",), vmem_limit_bytes=VMEM_LIMIT_BYTES),
        name="swa_sample",
    )(q, gate, k_new, v_new, ck, cv, sink_rows)


def _rope_tables(first_pos, n_pos, repeat=1):
    half = B_HEAD_DIM // 2
    inv_freq = ROPE_THETA ** (-np.arange(half, dtype=np.float64) / half)
    ang = (first_pos + np.arange(n_pos, dtype=np.float64))[:, None] * inv_freq[None, :]
    cos, sin = np.cos(ang), np.sin(ang)
    cos_t = np.tile(np.concatenate([cos, cos, cos, cos], axis=1), (repeat, 1))
    sin_t = np.tile(np.concatenate([-sin, sin, -sin, sin], axis=1), (repeat, 1))
    return jnp.asarray(cos_t, F32), jnp.asarray(sin_t, F32)


def kernel(x_prompt, x_sample, state_gla, cache_swa_k, cache_swa_v, a_norm, a_w_in, a_w_gate2, a_b_gate,
           a_out_norm, a_w_out, kv_norm, w_k, w_v, k_norm, b_norm, b_w_in, b_q_norm, b_sinks, b_w_out):
    assert a_norm.shape[0] == 1 and b_norm.shape[0] == 1
    bsz_p, seq_p, d = x_prompt.shape
    bsz_s, seq_s, _ = x_sample.shape
    kw, bw = A_KEY_WIDTH, BRANCH_WIDTH

    w_in = a_w_in[0]
    wa_qk = w_in[:, :2 * kw].astype(BF16)
    wa_v = w_in[:, 2 * kw:2 * kw + bw].astype(BF16)
    wa_g = w_in[:, 2 * kw + bw:2 * kw + 2 * bw].astype(BF16)
    wa_low = jnp.pad(w_in[:, 2 * kw + 2 * bw:].astype(BF16), ((0, 0), (0, LANES - A_GATE_RANK)))
    wa_g2 = jnp.pad(a_w_gate2[0].astype(BF16), ((0, LANES - A_GATE_RANK), (0, 0)))
    a_bg = a_b_gate[0][None, :]
    a_n = a_norm[0][None, :]
    a_on = a_out_norm[0][None, :]
    wa_out = a_w_out[0].astype(BF16)
    w_kv = jnp.concatenate([w_k.astype(BF16), w_v.astype(BF16)], axis=1)
    wb_q = b_w_in[0][:, :bw].astype(BF16)
    wb_g = b_w_in[0][:, bw:].astype(BF16)
    wb_out = b_w_out[0].astype(BF16)
    kvn = kv_norm[None, :]
    bn = b_norm[0][None, :]
    kn_t = jnp.tile(k_norm, B_KV_WIDTH // B_HEAD_DIM)[None, :]
    qn_t = jnp.tile(b_q_norm[0], B_KV_WIDTH // B_HEAD_DIM)[None, :]
    sinks = b_sinks[0]
    grp = np.arange(B_KV_WIDTH) // B_HEAD_DIM
    bd = jnp.asarray(grp[:, None] == grp[None, :], BF16)

    cos_p, sin_p = _rope_tables(0, seq_p)
    h1_p, st_p = _gla_prompt(x_prompt, a_n, wa_qk, wa_v, wa_g, wa_low, wa_g2, a_bg, a_on, wa_out)
    y_p, kc_p, vc_p = _swa_prompt(h1_p, kvn, w_kv, kn_t, bd, cos_p, sin_p, bn, wb_q, wb_g, qn_t, sinks, wb_out)

    m = bsz_s * seq_s
    xs = x_sample.reshape(m, d)
    qk_a, v_a, gate_a, glow_a = _gla_proj_sample(xs, a_n, wa_qk, wa_v, wa_g, wa_low)
    o_s, st_s = _gla_sample(qk_a, v_a, glow_a, wa_g2, a_bg, state_gla[0], seq_s)
    h1_s = _gla_out(o_s, gate_a, xs, a_on, wa_out)

    sink_rows = jnp.broadcast_to(
        sinks.reshape(B_KV_HEADS, B_GROUP).T[:, :, None, None], (B_GROUP, B_KV_HEADS, SUBLANES, LANES)
    ).reshape(B_GROUP * B_KV_HEADS * SUBLANES, LANES)
    cos_s, sin_s = _rope_tables(PAST_LEN, seq_s, repeat=bsz_s)
    k_new, v_new, q_s, gate_b = _swa_pre_sample(h1_s, kvn, w_kv, kn_t, bd, cos_s, sin_s, bn, wb_q, wb_g, qn_t)
    og_s, nk_s, nv_s = _swa_sample(
        q_s, gate_b, k_new, v_new,
        cache_swa_k.reshape(bsz_s, WINDOW, B_KV_WIDTH), cache_swa_v.reshape(bsz_s, WINDOW, B_KV_WIDTH), sink_rows, seq_s)
    y_s = _out_proj(og_s, h1_s, wb_out)

    return (y_p, y_s.reshape(bsz_s, seq_s, d),
            st_p[None], st_s[None],
            kc_p.reshape(bsz_p, WINDOW, B_KV_HEADS, B_HEAD_DIM), vc_p.reshape(bsz_p, WINDOW, B_KV_HEADS, B_HEAD_DIM),
            nk_s.reshape(bsz_s, WINDOW, B_KV_HEADS, B_HEAD_DIM), nv_s.reshape(bsz_s, WINDOW, B_KV_HEADS, B_HEAD_DIM))
```

```python
import functools

import jax
import jax.numpy as jnp
import numpy as np
from jax import lax
from jax.experimental import pallas as pl
from jax.experimental.pallas import tpu as pltpu

F32 = jnp.float32
BF16 = jnp.bfloat16

A_HEADS = 4
A_HEAD_DK = 128
A_HEAD_DV = 512
A_KEY_WIDTH = A_HEADS * A_HEAD_DK
BRANCH_WIDTH = A_HEADS * A_HEAD_DV
A_GATE_RANK = 16
A_GATE_NORMALIZER = 16.0
B_HEAD_DIM = 64
B_HEADS = BRANCH_WIDTH // B_HEAD_DIM
B_KV_HEADS = 4
B_GROUP = B_HEADS // B_KV_HEADS
B_KV_WIDTH = B_KV_HEADS * B_HEAD_DIM
WINDOW = 128
ROPE_THETA = 10000.0
RMS_EPS = 1e-6
PAST_LEN = 16384

LANES = 128
SUBLANES = 8
VMEM_LIMIT_BYTES = 56 * 1024 * 1024

PROMPT_TILE = 512
GLA_CHUNK = 256
GLA_SAFE_DECAY = 80.0
SAMPLE_BATCH_BLOCK = 8
OUT_PROJ_CHUNKS = 4
LOG2E = 1.4426950408889634


def _dot(a, b):
    return jnp.dot(a, b, preferred_element_type=F32)


def _dot_nt(a, b):
    return lax.dot_general(a, b, (((1,), (1,)), ((), ())), preferred_element_type=F32)


def _split2(x):
    hi = x.astype(BF16)
    lo = (x - hi.astype(F32)).astype(BF16)
    return hi, lo


def _rmsnorm(x, g):
    ms = jnp.mean(x * x, axis=-1, keepdims=True)
    return x * lax.rsqrt(ms + RMS_EPS) * g


def _log_sigmoid(x):
    return jnp.minimum(x, 0.0) - jnp.log1p(jnp.exp(-jnp.abs(x)))


def _silu(x):
    return x * (1.0 / (1.0 + jnp.exp(-x)))


def _head_norm(x, bd, gain):
    hi, lo = _split2(x * x)
    ss = _dot(hi, bd) + _dot(lo, bd)
    return x * lax.rsqrt(ss * (1.0 / B_HEAD_DIM) + RMS_EPS) * gain


def _rope_slab(x, cos, sin_signed, first_half):
    xr = jnp.where(first_half, pltpu.roll(x, 96, 1), pltpu.roll(x, 32, 1))
    return x * cos + xr * sin_signed


def _lane_masks(rows):
    lane = lax.broadcasted_iota(jnp.int32, (rows, LANES), 1)
    first_half = (lane % B_HEAD_DIM) < (B_HEAD_DIM // 2)
    lo64 = lane < B_HEAD_DIM
    return first_half, lo64


def _split_group(slab, g, lo64):
    swapped = pltpu.roll(slab, B_HEAD_DIM, 1)
    zero = jnp.zeros_like(slab)
    if g % 2 == 0:
        return jnp.where(lo64, slab, zero), jnp.where(lo64, zero, swapped)
    return jnp.where(lo64, swapped, zero), jnp.where(lo64, zero, slab)


def _regroup_slab(a, b, odd, lo64):
    if odd:
        return jnp.where(lo64, pltpu.roll(a, B_HEAD_DIM, 1), b)
    return jnp.where(lo64, a, pltpu.roll(b, B_HEAD_DIM, 1))


def _gla_prompt_kernel(x_ref, an_ref, wqk_ref, wv_ref, wg_ref, wlow_ref, wg2_ref, bg_ref, on_ref, wout_ref,
                       h1_ref, st_ref,
                       qk_s, v_s, gate_s, b_s, o_s, oi_s):
    tile = x_ref.shape[0]
    chunk = GLA_CHUNK
    n_chunks = tile // chunk
    dk, dv, kw = A_HEAD_DK, A_HEAD_DV, A_KEY_WIDTH

    @pl.when(pl.program_id(1) == 0)
    def _():
        st_ref[...] = jnp.zeros_like(st_ref)

    x = x_ref[...]
    u = _rmsnorm(x, an_ref[...]).astype(BF16)
    qk_s[...] = _dot(u, wqk_ref[...])
    v_s[...] = _dot(u, wv_ref[...])
    gate_s[...] = _dot(u, wg_ref[...])
    glow = _dot(u, wlow_ref[...]).astype(BF16)
    pre = _dot(glow, wg2_ref[...]) + bg_ref[...]
    loga = _log_sigmoid(pre) * (1.0 / A_GATE_NORMALIZER)

    row = lax.broadcasted_iota(jnp.int32, (chunk, chunk), 0)
    col = lax.broadcasted_iota(jnp.int32, (chunk, chunk), 1)
    lower = row >= col
    tri = jnp.where(lower, 1.0, 0.0).astype(BF16)
    for c in range(n_chunks):
        hi, lo = _split2(loga[c * chunk:(c + 1) * chunk])
        b_s[c * chunk:(c + 1) * chunk, :] = _dot(tri, hi) + _dot(tri, lo)
    safe = jnp.min(b_s[...]) >= -GLA_SAFE_DECAY

    scale = A_HEAD_DK ** -0.5
    for c in range(n_chunks):
        rows = slice(c * chunk, (c + 1) * chunk)
        b = b_s[rows, :]
        eb = jnp.exp(b)
        enb = jnp.exp(-b)
        blast = b[chunk - 1:chunk, :]
        ekl = jnp.exp(blast - b)
        eblast = jnp.exp(blast)
        for h in range(A_HEADS):
            ks = slice(h * dk, (h + 1) * dk)
            vs = slice(h * dv, (h + 1) * dv)
            q = qk_s[rows, ks] * scale
            k = qk_s[rows, kw + h * dk: kw + (h + 1) * dk]
            vh = v_s[rows, vs].astype(BF16)
            state = st_ref[h]
            qt = (q * eb[:, ks]).astype(BF16)
            o_inter = _dot(qt, state.astype(BF16))
            a = _dot_nt(qt, (k * enb[:, ks]).astype(BF16))
            a = jnp.where(lower, a, 0.0).astype(BF16)
            oi_s[rows, vs] = o_inter
            o_s[rows, vs] = o_inter + _dot(a, vh)
            kd_t = jnp.transpose(k * ekl[:, ks]).astype(BF16)
            dec = jnp.transpose(jnp.broadcast_to(eblast[:, ks], (dk, dk)))
            st_ref[h] = state * jnp.concatenate([dec] * (dv // dk), axis=1) + _dot(kd_t, vh)

    def finish():
        y = x
        for h in range(A_HEADS):
            vs = slice(h * dv, (h + 1) * dv)
            o = _rmsnorm(o_s[:, vs], on_ref[...])
            og = (o * _silu(gate_s[:, vs])).astype(BF16)
            y = y + _dot(og, wout_ref[vs, :])
        h1_ref[...] = y

    finish()

    @pl.when(jnp.logical_not(safe))
    def _():
        trow = lax.broadcasted_iota(jnp.int32, (chunk, dk), 0)
        sub8 = lax.broadcasted_iota(jnp.int32, (SUBLANES, 1), 0)
        for c in range(n_chunks):
            rows = slice(c * chunk, (c + 1) * chunk)
            for h in range(A_HEADS):
                ks = slice(h * dk, (h + 1) * dk)
                vs = slice(h * dv, (h + 1) * dv)
                q = qk_s[rows, ks] * scale
                b = b_s[rows, ks]

                def body(s, acc, c=c, h=h, q=q, b=b):
                    r8 = pl.multiple_of(c * chunk + (s // SUBLANES) * SUBLANES, SUBLANES)
                    pick = sub8 == s % SUBLANES

                    def row_of(ref, lanes):
                        return jnp.sum(jnp.where(pick, ref[pl.ds(r8, SUBLANES), lanes], 0.0), axis=0, keepdims=True)

                    brow = row_of(b_s, slice(h * dk, (h + 1) * dk))
                    krow = row_of(qk_s, slice(kw + h * dk, kw + (h + 1) * dk))
                    vrow = row_of(v_s, slice(h * dv, (h + 1) * dv))
                    w = jnp.where(trow >= s, jnp.exp(jnp.minimum(b - brow, 0.0)), 0.0)
                    colv = jnp.sum(q * krow * w, axis=-1, keepdims=True)
                    return acc + colv * vrow

                o_s[rows, vs] = oi_s[rows, vs] + lax.fori_loop(0, chunk, body, jnp.zeros((chunk, dv), F32))
        finish()


def _gla_prompt(x, a_norm, wqk, wv, wg, wlow, wg2, bg, out_norm, wout):
    bsz, seq, d = x.shape
    tile = min(PROMPT_TILE, seq)
    assert seq % tile == 0 and tile % GLA_CHUNK == 0
    const = lambda shape: pl.BlockSpec(shape, lambda b, l: (0,) * len(shape), pipeline_mode=pl.Buffered(1))
    return pl.pallas_call(
        _gla_prompt_kernel,
        grid=(bsz, seq // tile),
        in_specs=[
            pl.BlockSpec((None, tile, d), lambda b, l: (b, l, 0)),
            const(a_norm.shape), const(wqk.shape), const(wv.shape), const(wg.shape), const(wlow.shape),
            const(wg2.shape), const(bg.shape), const(out_norm.shape), const(wout.shape),
        ],
        out_specs=[
            pl.BlockSpec((None, tile, d), lambda b, l: (b, l, 0)),
            pl.BlockSpec((None, A_HEADS, A_HEAD_DK, A_HEAD_DV), lambda b, l: (b, 0, 0, 0)),
        ],
        out_shape=[
            jax.ShapeDtypeStruct((bsz, seq, d), F32),
            jax.ShapeDtypeStruct((bsz, A_HEADS, A_HEAD_DK, A_HEAD_DV), F32),
        ],
        scratch_shapes=[
            pltpu.VMEM((tile, 2 * A_KEY_WIDTH), F32),
            pltpu.VMEM((tile, BRANCH_WIDTH), F32),
            pltpu.VMEM((tile, BRANCH_WIDTH), F32),
            pltpu.VMEM((tile, A_KEY_WIDTH), F32),
            pltpu.VMEM((tile, BRANCH_WIDTH), F32),
            pltpu.VMEM((tile, BRANCH_WIDTH), F32),
        ],
        compiler_params=pltpu.CompilerParams(
            dimension_semantics=("arbitrary", "arbitrary"), vmem_limit_bytes=VMEM_LIMIT_BYTES),
        name="gla_prompt",
    )(x, a_norm, wqk, wv, wg, wlow, wg2, bg, out_norm, wout)


def _swa_prompt_kernel(h_ref, kvn_ref, wkv_ref, kn_ref, bd_ref, cos_ref, sin_ref, bn_ref, wq_ref, wg_ref,
                       qn_ref, sink_ref, wout_ref,
                       y_ref, kc_ref, vc_ref,
                       ktop_s, kbot_s, vtop_s, vbot_s, qb_s, ub_s, kv_s, q_s, gsl_s, osl_s, og_s):
    tile = h_ref.shape[0]
    l = pl.program_id(1)
    blk = WINDOW
    kvw = B_KV_WIDTH
    pairs = B_GROUP // 2
    n_slabs = BRANCH_WIDTH // LANES
    n_iter = (tile // blk) * B_KV_HEADS
    gate_cols = wg_ref.shape[2]

    @pl.when(l == 0)
    def _():
        zeros = jnp.zeros((B_KV_HEADS, WINDOW, LANES), BF16)
        ktop_s[:, 0:WINDOW, :] = zeros
        kbot_s[:, 0:WINDOW, :] = zeros
        vtop_s[:, 0:WINDOW, 0:LANES] = zeros
        vbot_s[:, 0:WINDOW, 0:LANES] = zeros
        lane = lax.broadcasted_iota(jnp.int32, (B_KV_HEADS, WINDOW + tile, LANES), 2)
        vtop_s[:, :, LANES:] = jnp.where(lane < B_HEAD_DIM, 1.0, 0.0).astype(BF16)
        vbot_s[:, :, LANES:] = jnp.where(lane < B_HEAD_DIM, 0.0, 1.0).astype(BF16)

    h = h_ref[...]
    cos = cos_ref[...]
    sin = sin_ref[...]
    first_half, lo64 = _lane_masks(tile)
    bd = bd_ref[...]

    kv_s[...] = _dot(_rmsnorm(h, kvn_ref[...]).astype(BF16), wkv_ref[...])
    ub_s[...] = _rmsnorm(h, bn_ref[...]).astype(BF16)

    kn = _head_norm(kv_s[:, :kvw], bd, kn_ref[...])
    v = kv_s[:, kvw:]
    k_slabs = [_rope_slab(kn[:, s * LANES:(s + 1) * LANES], cos, sin, first_half) for s in range(kvw // LANES)]

    @pl.when(l == pl.num_programs(1) - 1)
    def _():
        kc_ref[...] = jnp.concatenate(k_slabs, axis=1)[tile - WINDOW:, :]
        vc_ref[...] = v[tile - WINDOW:, :]

    for g in range(B_KV_HEADS):
        s = g // 2
        top, bot = _split_group(k_slabs[s], g, lo64)
        ktop_s[g, WINDOW:WINDOW + tile, :] = top.astype(BF16)
        kbot_s[g, WINDOW:WINDOW + tile, :] = bot.astype(BF16)
        top, bot = _split_group(v[:, s * LANES:(s + 1) * LANES], g, lo64)
        vtop_s[g, WINDOW:WINDOW + tile, 0:LANES] = top.astype(BF16)
        vbot_s[g, WINDOW:WINDOW + tile, 0:LANES] = bot.astype(BF16)

    qscale = qn_ref[...] * (B_HEAD_DIM ** -0.5 * LOG2E)
    qcos = [cos * qscale[:, s * LANES:(s + 1) * LANES] for s in range(kvw // LANES)]
    qsin = [sin * jnp.where(first_half[0:1], pltpu.roll(qscale[:, s * LANES:(s + 1) * LANES], 96, 1),
                            pltpu.roll(qscale[:, s * LANES:(s + 1) * LANES], 32, 1)) for s in range(kvw // LANES)]

    q_s[...] = _dot(ub_s[...], wq_ref[...])
    for c in range(BRANCH_WIDTH // kvw):
        x = q_s[:, c * kvw:(c + 1) * kvw]
        hi, lo = _split2(x * x)
        rinv = lax.rsqrt((_dot(hi, bd) + _dot(lo, bd)) * (1.0 / B_HEAD_DIM) + RMS_EPS)
        for s in range(kvw // LANES):
            xs = x[:, s * LANES:(s + 1) * LANES]
            xr = jnp.where(first_half, pltpu.roll(xs, 96, 1), pltpu.roll(xs, 32, 1))
            qr = ((xs * qcos[s] + xr * qsin[s]) * rinv[:, s * LANES:(s + 1) * LANES]).astype(BF16)
            for i in range(tile // blk):
                qb_s[i, c * (kvw // LANES) + s] = qr[i * blk:(i + 1) * blk]

    rowi = lax.broadcasted_iota(jnp.int32, (blk, 2 * blk), 0)
    coli = lax.broadcasted_iota(jnp.int32, (blk, 2 * blk), 1)
    band = jnp.logical_and(coli - rowi >= 0, coli - rowi <= WINDOW)
    _, lo64_b = _lane_masks(blk)

    def attn_pair(idx):
        i = idx // B_KV_HEADS
        g = idx % B_KV_HEADS
        r0 = pl.multiple_of(i * blk, blk)
        first_col = jnp.where(l * tile + r0 > 0, 0, WINDOW)
        mask = jnp.logical_and(band, coli >= first_col)
        kpad = jnp.concatenate([ktop_s[g, pl.ds(r0, 2 * blk), :], kbot_s[g, pl.ds(r0, 2 * blk), :]], axis=0)
        vpad = jnp.concatenate([vtop_s[g, pl.ds(r0, 2 * blk), :], vbot_s[g, pl.ds(r0, 2 * blk), :]], axis=0)
        qg = qb_s[i, pl.ds(g * pairs, pairs)].reshape(pairs * blk, LANES)
        s = _dot_nt(qg, kpad)
        ps, esinks = [], []
        for j in range(pairs):
            pj, ej = [], []
            for half in range(2):
                sink = sink_ref[g * B_GROUP + 2 * j + half] * LOG2E
                sh = jnp.where(mask, s[j * blk:(j + 1) * blk, half * 2 * blk:(half + 1) * 2 * blk], -jnp.inf)
                m = jnp.maximum(jnp.max(sh, axis=-1, keepdims=True), sink)
                pj.append(jnp.exp2(sh - m).astype(BF16))
                ej.append(jnp.exp2(sink - m))
            ps.append(jnp.concatenate(pj, axis=1))
            esinks.append(jnp.where(lo64_b, ej[0], ej[1]))
        o = _dot(jnp.concatenate(ps, axis=0), vpad)
        for j in range(pairs):
            oj = o[j * blk:(j + 1) * blk]
            osl_s[g * pairs + j, pl.ds(r0, blk), :] = oj[:, :LANES] / (oj[:, LANES:] + esinks[j])

    def attn_body(k, carry):
        gc = _dot(ub_s[...], wg_ref[k])
        for s in range(gate_cols // LANES):
            gsl_s[k * (gate_cols // LANES) + s] = gc[:, s * LANES:(s + 1) * LANES]
        attn_pair(2 * k)
        attn_pair(2 * k + 1)
        return carry

    lax.fori_loop(0, n_iter // 2, attn_body, 0, unroll=4)

    for buf in (ktop_s, kbot_s):
        buf[:, 0:WINDOW, :] = buf[:, tile:tile + WINDOW, :]
    for buf in (vtop_s, vbot_s):
        buf[:, 0:WINDOW, 0:LANES] = buf[:, tile:tile + WINDOW, 0:LANES]

    y = h
    per = n_slabs // OUT_PROJ_CHUNKS
    for c in range(OUT_PROJ_CHUNKS):
        for sl in range(c * per, (c + 1) * per):
            og_s[:, sl * LANES:(sl + 1) * LANES] = (osl_s[sl] * _silu(gsl_s[sl])).astype(BF16)
        cols = slice(c * per * LANES, (c + 1) * per * LANES)
        y = y + _dot(og_s[:, cols], wout_ref[cols, :])
    y_ref[...] = y


def _swa_prompt(h, kv_norm, wkv, k_norm, bd, cos, sin, b_norm, wq, wg, q_norm, sinks, wout):
    bsz, seq, d = h.shape
    tile = min(PROMPT_TILE, seq)
    assert seq % tile == 0 and tile % WINDOW == 0 and seq >= WINDOW
    n_iter = (tile // WINDOW) * B_KV_HEADS
    gate_cols = BRANCH_WIDTH // (n_iter // 2)
    assert gate_cols % LANES == 0
    wg = wg.reshape(d, n_iter // 2, gate_cols).transpose(1, 0, 2)
    const = lambda shape: pl.BlockSpec(shape, lambda b, l: (0,) * len(shape), pipeline_mode=pl.Buffered(1))
    k_scratch = pltpu.VMEM((B_KV_HEADS, WINDOW + tile, LANES), BF16)
    v_scratch = pltpu.VMEM((B_KV_HEADS, WINDOW + tile, 2 * LANES), BF16)
    n_slabs = BRANCH_WIDTH // LANES
    return pl.pallas_call(
        _swa_prompt_kernel,
        grid=(bsz, seq // tile),
        in_specs=[
            pl.BlockSpec((None, tile, d), lambda b, l: (b, l, 0)),
            const(kv_norm.shape), const(wkv.shape), const(k_norm.shape), const(bd.shape),
            pl.BlockSpec((tile, LANES), lambda b, l: (l, 0)),
            pl.BlockSpec((tile, LANES), lambda b, l: (l, 0)),
            const(b_norm.shape), const(wq.shape), const(wg.shape), const(q_norm.shape),
            pl.BlockSpec(memory_space=pltpu.SMEM),
            const(wout.shape),
        ],
        out_specs=[
            pl.BlockSpec((None, tile, d), lambda b, l: (b, l, 0)),
            pl.BlockSpec((None, WINDOW, B_KV_WIDTH), lambda b, l: (b, 0, 0)),
            pl.BlockSpec((None, WINDOW, B_KV_WIDTH), lambda b, l: (b, 0, 0)),
        ],
        out_shape=[
            jax.ShapeDtypeStruct((bsz, seq, d), F32),
            jax.ShapeDtypeStruct((bsz, WINDOW, B_KV_WIDTH), F32),
            jax.ShapeDtypeStruct((bsz, WINDOW, B_KV_WIDTH), F32),
        ],
        scratch_shapes=[
            k_scratch, k_scratch, v_scratch, v_scratch,
            pltpu.VMEM((tile // WINDOW, n_slabs, WINDOW, LANES), BF16),
            pltpu.VMEM((tile, d), BF16),
            pltpu.VMEM((tile, 2 * B_KV_WIDTH), F32),
            pltpu.VMEM((tile, BRANCH_WIDTH), F32),
            pltpu.VMEM((n_slabs, tile, LANES), F32),
            pltpu.VMEM((n_slabs, tile, LANES), F32),
            pltpu.VMEM((tile, BRANCH_WIDTH), BF16),
        ],
        compiler_params=pltpu.CompilerParams(
            dimension_semantics=("arbitrary", "arbitrary"), vmem_limit_bytes=VMEM_LIMIT_BYTES),
        name="swa_prompt",
    )(h, kv_norm, wkv, k_norm, bd, cos, sin, b_norm, wq, wg, q_norm, sinks, wout)


def _gla_proj_sample_kernel(x_ref, g_ref, wqk_ref, wv_ref, wg_ref, wlow_ref, qk_ref, v_ref, gate_ref, glow_ref):
    u = _rmsnorm(x_ref[...], g_ref[...]).astype(BF16)
    qk_ref[...] = _dot(u, wqk_ref[...])
    v_ref[...] = _dot(u, wv_ref[...])
    gate_ref[...] = _dot(u, wg_ref[...])
    glow_ref[...] = _dot(u, wlow_ref[...])


def _gla_proj_sample(x, g, wqk, wv, wg, wlow):
    m = x.shape[0]
    return pl.pallas_call(
        _gla_proj_sample_kernel,
        out_shape=[jax.ShapeDtypeStruct((m, w.shape[1]), F32) for w in (wqk, wv, wg, wlow)],
        compiler_params=pltpu.CompilerParams(vmem_limit_bytes=VMEM_LIMIT_BYTES),
        name="gla_proj_sample",
    )(x, g, wqk, wv, wg, wlow)


def _gla_out_kernel(o_ref, gate_ref, x_ref, on_ref, w_ref, y_ref, og_s):
    for h in range(A_HEADS):
        vs = slice(h * A_HEAD_DV, (h + 1) * A_HEAD_DV)
        o = _rmsnorm(o_ref[:, vs], on_ref[...])
        og_s[:, vs] = (o * _silu(gate_ref[:, vs])).astype(BF16)
    y_ref[...] = _dot(og_s[...], w_ref[...]) + x_ref[...]


def _gla_out(o, gate, x, out_norm, wout):
    m = o.shape[0]
    return pl.pallas_call(
        _gla_out_kernel,
        out_shape=jax.ShapeDtypeStruct(x.shape, F32),
        scratch_shapes=[pltpu.VMEM((m, BRANCH_WIDTH), BF16)],
        compiler_params=pltpu.CompilerParams(vmem_limit_bytes=VMEM_LIMIT_BYTES),
        name="gla_out_sample",
    )(o, gate, x, out_norm, wout)


def _out_proj_kernel(og_ref, x_ref, w_ref, y_ref):
    y_ref[...] = _dot(og_ref[...].astype(BF16), w_ref[...]) + x_ref[...]


def _out_proj(og, x, wout):
    return pl.pallas_call(
        _out_proj_kernel,
        out_shape=jax.ShapeDtypeStruct(x.shape, F32),
        compiler_params=pltpu.CompilerParams(vmem_limit_bytes=VMEM_LIMIT_BYTES),
        name="swa_out_sample",
    )(og, x, wout)


def _gla_sample_kernel(qk_ref, v_ref, glow_ref, wg2_ref, bg_ref, st_ref, o_ref, nst_ref, *, seq):
    nb = st_ref.shape[0]
    dk, dv, kw = A_HEAD_DK, A_HEAD_DV, A_KEY_WIDTH
    scale = A_HEAD_DK ** -0.5
    per_tile = SUBLANES // seq

    def one_seq(i, qk, v, loga):
        q = qk[:, 0:kw] * scale
        k = qk[:, kw:2 * kw]
        bs = [loga[0:1]]
        for t in range(1, seq):
            bs.append(bs[-1] + loga[t:t + 1])
        bmat = jnp.concatenate(bs, axis=0)
        blast = bs[-1]
        qt = q * jnp.exp(bmat)
        kd = k * jnp.exp(blast - bmat)
        eblast = jnp.exp(blast)

        outs = [[jnp.zeros((1, dv), F32) for _ in range(A_HEADS)] for _ in range(seq)]
        for t in range(seq):
            for s in range(t + 1):
                w = q[t:t + 1] * k[s:s + 1] * jnp.exp(bs[t] - bs[s])
                for h in range(A_HEADS):
                    a = jnp.sum(w[:, h * dk:(h + 1) * dk], axis=-1, keepdims=True)
                    outs[t][h] = outs[t][h] + a * v[s:s + 1, h * dv:(h + 1) * dv]
        intra = jnp.concatenate([jnp.concatenate(outs[t], axis=1) for t in range(seq)], axis=0)

        qt8 = jnp.concatenate([qt, jnp.zeros((SUBLANES - seq, kw), F32)], axis=0).astype(BF16)
        inter = []
        for h in range(A_HEADS):
            ks = slice(h * dk, (h + 1) * dk)
            state = st_ref[i, h]
            inter.append(_dot(qt8[:, ks], state.astype(BF16))[0:seq])
            m = jnp.concatenate([kd[:, ks], eblast[:, ks], jnp.zeros((dk - seq - 1, dk), F32)], axis=0)
            mt = jnp.transpose(m)
            vpad = jnp.concatenate([v[:, h * dv:(h + 1) * dv], jnp.zeros((dk - seq, dv), F32)], axis=0)
            nst_ref[i, h] = state * mt[:, seq:seq + 1] + _dot(mt.astype(BF16), vpad.astype(BF16))
        return intra + jnp.concatenate(inter, axis=1)

    def per_tile_body(j, carry):
        rows = pl.ds(pl.multiple_of(j * SUBLANES, SUBLANES), SUBLANES)
        qk8, v8 = qk_ref[rows, :], v_ref[rows, :]
        pre = _dot(glow_ref[rows, :].astype(BF16), wg2_ref[...]) + bg_ref[...]
        loga8 = _log_sigmoid(pre) * (1.0 / A_GATE_NORMALIZER)
        outs = []
        for p in range(per_tile):
            r = slice(p * seq, (p + 1) * seq)
            outs.append(one_seq(j * per_tile + p, qk8[r], v8[r], loga8[r]))
        o_ref[rows, :] = jnp.concatenate(outs, axis=0)
        return carry

    lax.fori_loop(0, nb // per_tile, per_tile_body, 0)


def _gla_sample(qk, v, glow, wg2, bg, state, seq):
    nbatch = state.shape[0]
    nb = min(SAMPLE_BATCH_BLOCK // 2, nbatch)
    assert nbatch % nb == 0 and SUBLANES % seq == 0 and (nb * seq) % SUBLANES == 0
    st_spec = pl.BlockSpec((nb, A_HEADS, A_HEAD_DK, A_HEAD_DV), lambda b: (b, 0, 0, 0))
    tok_spec = lambda a: pl.BlockSpec((nb * seq, a.shape[1]), lambda b: (b, 0))
    return pl.pallas_call(
        functools.partial(_gla_sample_kernel, seq=seq),
        grid=(nbatch // nb,),
        in_specs=[
            tok_spec(qk), tok_spec(v), tok_spec(glow),
            pl.BlockSpec(wg2.shape, lambda b: (0, 0)),
            pl.BlockSpec(bg.shape, lambda b: (0, 0)),
            st_spec,
        ],
        out_specs=[pl.BlockSpec((nb * seq, BRANCH_WIDTH), lambda b: (b, 0)), st_spec],
        out_shape=[
            jax.ShapeDtypeStruct((nbatch * seq, BRANCH_WIDTH), F32),
            jax.ShapeDtypeStruct(state.shape, F32),
        ],
        compiler_params=pltpu.CompilerParams(
            dimension_semantics=("arbitrary",), vmem_limit_bytes=VMEM_LIMIT_BYTES),
        name="gla_sample",
    )(qk, v, glow, wg2, bg, state)


def _swa_pre_sample_kernel(h_ref, kvn_ref, wkv_ref, kn_ref, bd_ref, cos_ref, sin_ref, bn_ref, wq_ref, wg_ref, qn_ref,
                           k_ref, v_ref, q_ref, gate_ref):
    m = h_ref.shape[0]
    kvw = B_KV_WIDTH
    h = h_ref[...]
    cos = cos_ref[...]
    sin = sin_ref[...]
    first_half, _ = _lane_masks(m)
    bd = bd_ref[...]

    kv = _dot(_rmsnorm(h, kvn_ref[...]).astype(BF16), wkv_ref[...])
    kn = _head_norm(kv[:, :kvw], bd, kn_ref[...])
    for s in range(kvw // LANES):
        k_ref[:, s * LANES:(s + 1) * LANES] = _rope_slab(kn[:, s * LANES:(s + 1) * LANES], cos, sin, first_half)
    v_ref[...] = kv[:, kvw:]

    ub = _rmsnorm(h, bn_ref[...]).astype(BF16)
    _, lo64 = _lane_masks(m)
    qgain = qn_ref[...] * (B_HEAD_DIM ** -0.5 * LOG2E)
    gate = _dot(ub, wg_ref[...])
    q_slabs, g_slabs = [], []
    for c in range(BRANCH_WIDTH // kvw):
        qc = _head_norm(_dot(ub, wq_ref[:, c * kvw:(c + 1) * kvw]), bd, qgain)
        for s in range(kvw // LANES):
            q_slabs.append(_rope_slab(qc[:, s * LANES:(s + 1) * LANES], cos, sin, first_half))
            g_slabs.append(gate[:, c * kvw + s * LANES:c * kvw + (s + 1) * LANES])
    per_group = B_GROUP // 2
    for r in range(B_GROUP):
        for p in range(B_KV_HEADS // 2):
            a = 2 * p * per_group + r // 2
            b = (2 * p + 1) * per_group + r // 2
            lanes = slice((2 * r + p) * LANES, (2 * r + p + 1) * LANES)
            q_ref[:, lanes] = _regroup_slab(q_slabs[a], q_slabs[b], r % 2, lo64)
            gate_ref[:, lanes] = _regroup_slab(g_slabs[a], g_slabs[b], r % 2, lo64)


def _swa_pre_sample(h, kv_norm, wkv, k_norm, bd, cos, sin, b_norm, wq, wg, q_norm):
    m = h.shape[0]
    return pl.pallas_call(
        _swa_pre_sample_kernel,
        out_shape=[
            jax.ShapeDtypeStruct((m, B_KV_WIDTH), F32),
            jax.ShapeDtypeStruct((m, B_KV_WIDTH), F32),
            jax.ShapeDtypeStruct((m, BRANCH_WIDTH), F32),
            jax.ShapeDtypeStruct((m, BRANCH_WIDTH), F32),
        ],
        compiler_params=pltpu.CompilerParams(vmem_limit_bytes=VMEM_LIMIT_BYTES),
        name="swa_pre_sample",
    )(h, kv_norm, wkv, k_norm, bd, cos, sin, b_norm, wq, wg, q_norm)


def _swa_sample_kernel(q_ref, gate_ref, kn_ref, vn_ref, ck_ref, cv_ref, sink_ref, og_ref, nk_ref, nv_ref, *, seq):
    nb = ck_ref.shape[0]
    per_tile = SUBLANES // seq
    kvw = B_KV_WIDTH
    trows = SUBLANES
    nrows = B_GROUP * B_KV_HEADS * trows
    rowi = lax.broadcasted_iota(jnp.int32, (nrows, 2 * WINDOW), 0)
    coli = lax.broadcasted_iota(jnp.int32, (nrows, 2 * WINDOW), 1)
    dpos = coli - rowi % trows
    mask = jnp.logical_and(dpos >= 0, dpos <= WINDOW)
    lane8 = lax.broadcasted_iota(jnp.int32, (trows, kvw), 1) // B_HEAD_DIM
    grow = (lax.broadcasted_iota(jnp.int32, (nrows, kvw), 0) // trows) % B_KV_HEADS
    gkeep = grow == lax.broadcasted_iota(jnp.int32, (nrows, kvw), 1) // B_HEAD_DIM
    sink = sink_ref[...][:, 0:1] * LOG2E
    zq = jnp.zeros((trows - seq, BRANCH_WIDTH), F32)
    lo64_q = lax.broadcasted_iota(jnp.int32, (seq, LANES), 1) < B_HEAD_DIM
    zk = jnp.zeros((WINDOW - seq, kvw), F32)

    def one_seq(i, q4, gate4, k_new, v_new):
        ck = ck_ref[i]
        cv = cv_ref[i]
        nk_ref[i, 0:WINDOW - seq, :] = ck[seq:, :]
        nk_ref[i, WINDOW - seq:WINDOW, :] = k_new
        nv_ref[i, 0:WINDOW - seq, :] = cv[seq:, :]
        nv_ref[i, WINDOW - seq:WINDOW, :] = v_new

        q8 = jnp.concatenate([q4, zq], axis=0)
        pieces = []
        for r in range(B_GROUP):
            slab = q8[:, r * kvw:(r + 1) * kvw]
            for g in range(B_KV_HEADS):
                pieces.append(jnp.where(lane8 == g, slab, 0.0))
        qrows = jnp.concatenate(pieces, axis=0).astype(BF16)
        kpad = jnp.concatenate([k_new, zk], axis=0).astype(BF16)
        vpad = jnp.concatenate([v_new, zk], axis=0).astype(BF16)
        s = jnp.concatenate([_dot_nt(qrows, ck.astype(BF16)), _dot_nt(qrows, kpad)], axis=1)
        s = jnp.where(mask, s, -jnp.inf)
        m = jnp.maximum(jnp.max(s, axis=-1, keepdims=True), sink)
        p = jnp.exp2(s - m)
        den = jnp.sum(p, axis=-1, keepdims=True) + jnp.exp2(sink - m)
        pb = p.astype(BF16)
        o = _dot(pb[:, :WINDOW], cv.astype(BF16)) + _dot(pb[:, WINDOW:], vpad)
        o = jnp.where(gkeep, o / den, 0.0)
        outs = []
        for r in range(B_GROUP):
            acc = o[r * B_KV_HEADS * trows:r * B_KV_HEADS * trows + trows]
            for g in range(1, B_KV_HEADS):
                acc = acc + o[(r * B_KV_HEADS + g) * trows:(r * B_KV_HEADS + g + 1) * trows]
            outs.append(acc)
        gated = jnp.concatenate(outs, axis=1)[0:seq] * _silu(gate4)
        slabs = []
        for g in range(B_KV_HEADS):
            for j in range(B_GROUP // 2):
                a = gated[:, (4 * j + g // 2) * LANES:(4 * j + g // 2 + 1) * LANES]
                b = gated[:, (4 * j + 2 + g // 2) * LANES:(4 * j + 2 + g // 2 + 1) * LANES]
                slabs.append(_regroup_slab(a, b, g % 2, lo64_q))
        return jnp.concatenate(slabs, axis=1)

    def per_tile_body(j, carry):
        rows = pl.ds(pl.multiple_of(j * SUBLANES, SUBLANES), SUBLANES)
        q8, gate8, kn8, vn8 = q_ref[rows, :], gate_ref[rows, :], kn_ref[rows, :], vn_ref[rows, :]
        outs = []
        for p in range(per_tile):
            r = slice(p * seq, (p + 1) * seq)
            outs.append(one_seq(j * per_tile + p, q8[r], gate8[r], kn8[r], vn8[r]))
        og_ref[rows, :] = jnp.concatenate(outs, axis=0)
        return carry

    lax.fori_loop(0, nb // per_tile, per_tile_body, 0, unroll=4)


def _swa_sample(q, gate, k_new, v_new, ck, cv, sink_rows, seq):
    nbatch = ck.shape[0]
    nb = min(SAMPLE_BATCH_BLOCK, nbatch)
    assert nbatch % nb == 0 and ck.shape[1] == WINDOW and SUBLANES % seq == 0 and (nb * seq) % SUBLANES == 0
    cache_spec = pl.BlockSpec((nb, WINDOW, B_KV_WIDTH), lambda b: (b, 0, 0))
    wide_spec = pl.BlockSpec((nb * seq, BRANCH_WIDTH), lambda b: (b, 0))
    new_spec = pl.BlockSpec((nb * seq, B_KV_WIDTH), lambda b: (b, 0))
    return pl.pallas_call(
        functools.partial(_swa_sample_kernel, seq=seq),
        grid=(nbatch // nb,),
        in_specs=[wide_spec, wide_spec, new_spec, new_spec, cache_spec, cache_spec,
                  pl.BlockSpec(sink_rows.shape, lambda b: (0, 0))],
        out_specs=[wide_spec, cache_spec, cache_spec],
        out_shape=[
            jax.ShapeDtypeStruct((nbatch * seq, BRANCH_WIDTH), F32),
            jax.ShapeDtypeStruct(ck.shape, F32),
            jax.ShapeDtypeStruct(cv.shape, F32),
        ],
        compiler_params=pltpu.CompilerParams(
            dimension_semantics=("arbitrary",), vmem_limit_bytes=VMEM_LIMIT_BYTES),
        name="swa_sample",
    )(q, gate, k_new, v_new, ck, cv, sink_rows)


def _rope_tables(first_pos, n_pos, repeat=1):
    half = B_HEAD_DIM // 2
    inv_freq = ROPE_THETA ** (-np.arange(half, dtype=np.float64) / half)
    ang = (first_pos + np.arange(n_pos, dtype=np.float64))[:, None] * inv_freq[None, :]
    cos, sin = np.cos(ang), np.sin(ang)
    cos_t = np.tile(np.concatenate([cos, cos, cos, cos], axis=1), (repeat, 1))
    sin_t = np.tile(np.concatenate([-sin, sin, -sin, sin], axis=1), (repeat, 1))
    return jnp.asarray(cos_t, F32), jnp.asarray(sin_t, F32)


def kernel(x_prompt, x_sample, state_gla, cache_swa_k, cache_swa_v, a_norm, a_w_in, a_w_gate2, a_b_gate,
           a_out_norm, a_w_out, kv_norm, w_k, w_v, k_norm, b_norm, b_w_in, b_q_norm, b_sinks, b_w_out):
    assert a_norm.shape[0] == 1 and b_norm.shape[0] == 1
    bsz_p, seq_p, d = x_prompt.shape
    bsz_s, seq_s, _ = x_sample.shape
    kw, bw = A_KEY_WIDTH, BRANCH_WIDTH

    w_in = a_w_in[0]
    wa_qk = w_in[:, :2 * kw].astype(BF16)
    wa_v = w_in[:, 2 * kw:2 * kw + bw].astype(BF16)
    wa_g = w_in[:, 2 * kw + bw:2 * kw + 2 * bw].astype(BF16)
    wa_low = jnp.pad(w_in[:, 2 * kw + 2 * bw:].astype(BF16), ((0, 0), (0, LANES - A_GATE_RANK)))
    wa_g2 = jnp.pad(a_w_gate2[0].astype(BF16), ((0, LANES - A_GATE_RANK), (0, 0)))
    a_bg = a_b_gate[0][None, :]
    a_n = a_norm[0][None, :]
    a_on = a_out_norm[0][None, :]
    wa_out = a_w_out[0].astype(BF16)
    w_kv = jnp.concatenate([w_k.astype(BF16), w_v.astype(BF16)], axis=1)
    wb_q = b_w_in[0][:, :bw].astype(BF16)
    wb_g = b_w_in[0][:, bw:].astype(BF16)
    wb_out = b_w_out[0].astype(BF16)
    kvn = kv_norm[None, :]
    bn = b_norm[0][None, :]
    kn_t = jnp.tile(k_norm, B_KV_WIDTH // B_HEAD_DIM)[None, :]
    qn_t = jnp.tile(b_q_norm[0], B_KV_WIDTH // B_HEAD_DIM)[None, :]
    sinks = b_sinks[0]
    grp = np.arange(B_KV_WIDTH) // B_HEAD_DIM
    bd = jnp.asarray(grp[:, None] == grp[None, :], BF16)

    cos_p, sin_p = _rope_tables(0, seq_p)
    h1_p, st_p = _gla_prompt(x_prompt, a_n, wa_qk, wa_v, wa_g, wa_low, wa_g2, a_bg, a_on, wa_out)
    y_p, kc_p, vc_p = _swa_prompt(h1_p, kvn, w_kv, kn_t, bd, cos_p, sin_p, bn, wb_q, wb_g, qn_t, sinks, wb_out)

    m = bsz_s * seq_s
    xs = x_sample.reshape(m, d)
    qk_a, v_a, gate_a, glow_a = _gla_proj_sample(xs, a_n, wa_qk, wa_v, wa_g, wa_low)
    o_s, st_s = _gla_sample(qk_a, v_a, glow_a, wa_g2, a_bg, state_gla[0], seq_s)
    h1_s = _gla_out(o_s, gate_a, xs, a_on, wa_out)

    sink_rows = jnp.broadcast_to(
        sinks.reshape(B_KV_HEADS, B_GROUP).T[:, :, None, None], (B_GROUP, B_KV_HEADS, SUBLANES, LANES)
    ).reshape(B_GROUP * B_KV_HEADS * SUBLANES, LANES)
    cos_s, sin_s = _rope_tables(PAST_LEN, seq_s, repeat=bsz_s)
    k_new, v_new, q_s, gate_b = _swa_pre_sample(h1_s, kvn, w_kv, kn_t, bd, cos_s, sin_s, bn, wb_q, wb_g, qn_t)
    og_s, nk_s, nv_s = _swa_sample(
        q_s, gate_b, k_new, v_new,
        cache_swa_k.reshape(bsz_s, WINDOW, B_KV_WIDTH), cache_swa_v.reshape(bsz_s, WINDOW, B_KV_WIDTH), sink_rows, seq_s)
    y_s = _out_proj(og_s, h1_s, wb_out)

    return (y_p, y_s.reshape(bsz_s, seq_s, d),
            st_p[None], st_s[None],
            kc_p.reshape(bsz_p, WINDOW, B_KV_HEADS, B_HEAD_DIM), vc_p.reshape(bsz_p, WINDOW, B_KV_HEADS, B_HEAD_DIM),
            nk_s.reshape(bsz_s, WINDOW, B_KV_HEADS, B_HEAD_DIM), nv_s.reshape(bsz_s, WINDOW, B_KV_HEADS, B_HEAD_DIM))
```

```python
import functools

import jax
import jax.numpy as jnp
import numpy as np
from jax import lax
from jax.experimental import pallas as pl
from jax.experimental.pallas import tpu as pltpu

F32 = jnp.float32
BF16 = jnp.bfloat16

A_HEADS = 4
A_HEAD_DK = 128
A_HEAD_DV = 512
A_KEY_WIDTH = A_HEADS * A_HEAD_DK
BRANCH_WIDTH = A_HEADS * A_HEAD_DV
A_GATE_RANK = 16
A_GATE_NORMALIZER = 16.0
B_HEAD_DIM = 64
B_HEADS = BRANCH_WIDTH // B_HEAD_DIM
B_KV_HEADS = 4
B_GROUP = B_HEADS // B_KV_HEADS
B_KV_WIDTH = B_KV_HEADS * B_HEAD_DIM
WINDOW = 128
ROPE_THETA = 10000.0
RMS_EPS = 1e-6
PAST_LEN = 16384

LANES = 128
SUBLANES = 8
VMEM_LIMIT_BYTES = 60 * 1024 * 1024

PROMPT_TILE = 512
GLA_CHUNK = 256
GLA_SAFE_DECAY = 80.0
SAMPLE_BATCH_BLOCK = 8
OUT_PROJ_CHUNKS = 4
LOG2E = 1.4426950408889634


def _dot(a, b):
    return jnp.dot(a, b, preferred_element_type=F32)


def _dot_nt(a, b):
    return lax.dot_general(a, b, (((1,), (1,)), ((), ())), preferred_element_type=F32)


def _split2(x):
    hi = x.astype(BF16)
    lo = (x - hi.astype(F32)).astype(BF16)
    return hi, lo


def _rmsnorm(x, g):
    ms = jnp.mean(x * x, axis=-1, keepdims=True)
    return x * lax.rsqrt(ms + RMS_EPS) * g


def _log_sigmoid(x):
    return jnp.minimum(x, 0.0) - jnp.log1p(jnp.exp(-jnp.abs(x)))


def _silu(x):
    return x * (1.0 / (1.0 + jnp.exp(-x)))


def _head_norm(x, bd, gain):
    hi, lo = _split2(x * x)
    ss = _dot(hi, bd) + _dot(lo, bd)
    return x * lax.rsqrt(ss * (1.0 / B_HEAD_DIM) + RMS_EPS) * gain


def _rope_slab(x, cos, sin_signed, first_half):
    xr = jnp.where(first_half, pltpu.roll(x, 96, 1), pltpu.roll(x, 32, 1))
    return x * cos + xr * sin_signed


def _lane_masks(rows):
    lane = lax.broadcasted_iota(jnp.int32, (rows, LANES), 1)
    first_half = (lane % B_HEAD_DIM) < (B_HEAD_DIM // 2)
    lo64 = lane < B_HEAD_DIM
    return first_half, lo64


def _split_group(slab, g, lo64):
    swapped = pltpu.roll(slab, B_HEAD_DIM, 1)
    zero = jnp.zeros_like(slab)
    if g % 2 == 0:
        return jnp.where(lo64, slab, zero), jnp.where(lo64, zero, swapped)
    return jnp.where(lo64, swapped, zero), jnp.where(lo64, zero, slab)


def _regroup_slab(a, b, odd, lo64):
    if odd:
        return jnp.where(lo64, pltpu.roll(a, B_HEAD_DIM, 1), b)
    return jnp.where(lo64, a, pltpu.roll(b, B_HEAD_DIM, 1))


def _gla_prompt_kernel(x_ref, an_ref, wqk_ref, wv_ref, wg_ref, wlow_ref, wg2_ref, bg_ref, on_ref, wout_ref,
                       h1_ref, st_ref,
                       qk_s, v_s, gate_s, b_s, o_s, oi_s):
    tile = x_ref.shape[0]
    chunk = GLA_CHUNK
    n_chunks = tile // chunk
    dk, dv, kw = A_HEAD_DK, A_HEAD_DV, A_KEY_WIDTH

    @pl.when(pl.program_id(1) == 0)
    def _():
        st_ref[...] = jnp.zeros_like(st_ref)

    x = x_ref[...]
    u = _rmsnorm(x, an_ref[...]).astype(BF16)
    qk_s[...] = _dot(u, wqk_ref[...])
    v_s[...] = _dot(u, wv_ref[...])
    gate_s[...] = _dot(u, wg_ref[...])
    glow = _dot(u, wlow_ref[...]).astype(BF16)
    pre = _dot(glow, wg2_ref[...]) + bg_ref[...]
    loga = _log_sigmoid(pre) * (1.0 / A_GATE_NORMALIZER)

    row = lax.broadcasted_iota(jnp.int32, (chunk, chunk), 0)
    col = lax.broadcasted_iota(jnp.int32, (chunk, chunk), 1)
    lower = row >= col
    tri = jnp.where(lower, 1.0, 0.0).astype(BF16)
    for c in range(n_chunks):
        hi, lo = _split2(loga[c * chunk:(c + 1) * chunk])
        b_s[c * chunk:(c + 1) * chunk, :] = _dot(tri, hi) + _dot(tri, lo)
    safe = jnp.min(b_s[...]) >= -GLA_SAFE_DECAY

    scale = A_HEAD_DK ** -0.5
    for c in range(n_chunks):
        rows = slice(c * chunk, (c + 1) * chunk)
        b = b_s[rows, :]
        eb = jnp.exp(b)
        enb = jnp.exp(-b)
        blast = b[chunk - 1:chunk, :]
        ekl = jnp.exp(blast - b)
        eblast = jnp.exp(blast)
        for h in range(A_HEADS):
            ks = slice(h * dk, (h + 1) * dk)
            vs = slice(h * dv, (h + 1) * dv)
            q = qk_s[rows, ks] * scale
            k = qk_s[rows, kw + h * dk: kw + (h + 1) * dk]
            vh = v_s[rows, vs].astype(BF16)
            state = st_ref[h]
            qt = (q * eb[:, ks]).astype(BF16)
            o_inter = _dot(qt, state.astype(BF16))
            a = _dot_nt(qt, (k * enb[:, ks]).astype(BF16))
            a = jnp.where(lower, a, 0.0).astype(BF16)
            oi_s[rows, vs] = o_inter
            o_s[rows, vs] = o_inter + _dot(a, vh)
            kd_t = jnp.transpose(k * ekl[:, ks]).astype(BF16)
            dec = jnp.transpose(jnp.broadcast_to(eblast[:, ks], (dk, dk)))
            st_ref[h] = state * jnp.concatenate([dec] * (dv // dk), axis=1) + _dot(kd_t, vh)

    def finish():
        y = x
        for h in range(A_HEADS):
            vs = slice(h * dv, (h + 1) * dv)
            o = _rmsnorm(o_s[:, vs], on_ref[...])
            og = (o * _silu(gate_s[:, vs])).astype(BF16)
            y = y + _dot(og, wout_ref[vs, :])
        h1_ref[...] = y

    finish()

    @pl.when(jnp.logical_not(safe))
    def _():
        trow = lax.broadcasted_iota(jnp.int32, (chunk, dk), 0)
        sub8 = lax.broadcasted_iota(jnp.int32, (SUBLANES, 1), 0)
        for c in range(n_chunks):
            rows = slice(c * chunk, (c + 1) * chunk)
            for h in range(A_HEADS):
                ks = slice(h * dk, (h + 1) * dk)
                vs = slice(h * dv, (h + 1) * dv)
                q = qk_s[rows, ks] * scale
                b = b_s[rows, ks]

                def body(s, acc, c=c, h=h, q=q, b=b):
                    r8 = pl.multiple_of(c * chunk + (s // SUBLANES) * SUBLANES, SUBLANES)
                    pick = sub8 == s % SUBLANES

                    def row_of(ref, lanes):
                        return jnp.sum(jnp.where(pick, ref[pl.ds(r8, SUBLANES), lanes], 0.0), axis=0, keepdims=True)

                    brow = row_of(b_s, slice(h * dk, (h + 1) * dk))
                    krow = row_of(qk_s, slice(kw + h * dk, kw + (h + 1) * dk))
                    vrow = row_of(v_s, slice(h * dv, (h + 1) * dv))
                    w = jnp.where(trow >= s, jnp.exp(jnp.minimum(b - brow, 0.0)), 0.0)
                    colv = jnp.sum(q * krow * w, axis=-1, keepdims=True)
                    return acc + colv * vrow

                o_s[rows, vs] = oi_s[rows, vs] + lax.fori_loop(0, chunk, body, jnp.zeros((chunk, dv), F32))
        finish()


def _gla_prompt(x, a_norm, wqk, wv, wg, wlow, wg2, bg, out_norm, wout):
    bsz, seq, d = x.shape
    tile = min(PROMPT_TILE, seq)
    assert seq % tile == 0 and tile % GLA_CHUNK == 0
    const = lambda shape: pl.BlockSpec(shape, lambda b, l: (0,) * len(shape), pipeline_mode=pl.Buffered(1))
    return pl.pallas_call(
        _gla_prompt_kernel,
        grid=(bsz, seq // tile),
        in_specs=[
            pl.BlockSpec((None, tile, d), lambda b, l: (b, l, 0)),
            const(a_norm.shape), const(wqk.shape), const(wv.shape), const(wg.shape), const(wlow.shape),
            const(wg2.shape), const(bg.shape), const(out_norm.shape), const(wout.shape),
        ],
        out_specs=[
            pl.BlockSpec((None, tile, d), lambda b, l: (b, l, 0)),
            pl.BlockSpec((None, A_HEADS, A_HEAD_DK, A_HEAD_DV), lambda b, l: (b, 0, 0, 0)),
        ],
        out_shape=[
            jax.ShapeDtypeStruct((bsz, seq, d), F32),
            jax.ShapeDtypeStruct((bsz, A_HEADS, A_HEAD_DK, A_HEAD_DV), F32),
        ],
        scratch_shapes=[
            pltpu.VMEM((tile, 2 * A_KEY_WIDTH), F32),
            pltpu.VMEM((tile, BRANCH_WIDTH), F32),
            pltpu.VMEM((tile, BRANCH_WIDTH), F32),
            pltpu.VMEM((tile, A_KEY_WIDTH), F32),
            pltpu.VMEM((tile, BRANCH_WIDTH), F32),
            pltpu.VMEM((tile, BRANCH_WIDTH), F32),
        ],
        compiler_params=pltpu.CompilerParams(
            dimension_semantics=("arbitrary", "arbitrary"), vmem_limit_bytes=VMEM_LIMIT_BYTES),
        name="gla_prompt",
    )(x, a_norm, wqk, wv, wg, wlow, wg2, bg, out_norm, wout)


def _swa_prompt_kernel(h_ref, kvn_ref, wkv_ref, kn_ref, bd_ref, cos_ref, sin_ref, bn_ref, wq_ref, wg_ref,
                       qn_ref, sink_ref, wout_ref,
                       y_ref, kc_ref, vc_ref,
                       ktop_s, kbot_s, vtop_s, vbot_s, qb_s, ub_s, kv_s, q_s, gsl_s, osl_s, og_s):
    tile = h_ref.shape[0]
    l = pl.program_id(1)
    blk = WINDOW
    kvw = B_KV_WIDTH
    pairs = B_GROUP // 2
    n_slabs = BRANCH_WIDTH // LANES
    n_iter = (tile // blk) * B_KV_HEADS
    gate_cols = wg_ref.shape[2]

    @pl.when(l == 0)
    def _():
        zeros = jnp.zeros((B_KV_HEADS, WINDOW, LANES), BF16)
        ktop_s[:, 0:WINDOW, :] = zeros
        kbot_s[:, 0:WINDOW, :] = zeros
        vtop_s[:, 0:WINDOW, 0:LANES] = zeros
        vbot_s[:, 0:WINDOW, 0:LANES] = zeros
        lane = lax.broadcasted_iota(jnp.int32, (B_KV_HEADS, WINDOW + tile, LANES), 2)
        vtop_s[:, :, LANES:] = jnp.where(lane < B_HEAD_DIM, 1.0, 0.0).astype(BF16)
        vbot_s[:, :, LANES:] = jnp.where(lane < B_HEAD_DIM, 0.0, 1.0).astype(BF16)

    h = h_ref[...]
    cos = cos_ref[...]
    sin = sin_ref[...]
    first_half, lo64 = _lane_masks(tile)
    bd = bd_ref[...]

    kv_s[...] = _dot(_rmsnorm(h, kvn_ref[...]).astype(BF16), wkv_ref[...])
    ub_s[...] = _rmsnorm(h, bn_ref[...]).astype(BF16)

    kn = _head_norm(kv_s[:, :kvw], bd, kn_ref[...])
    v = kv_s[:, kvw:]
    k_slabs = [_rope_slab(kn[:, s * LANES:(s + 1) * LANES], cos, sin, first_half) for s in range(kvw // LANES)]

    for s in range(kvw // LANES):
        kv_s[:, s * LANES:(s + 1) * LANES] = k_slabs[s]

    for g in range(B_KV_HEADS):
        s = g // 2
        top, bot = _split_group(k_slabs[s], g, lo64)
        ktop_s[g, WINDOW:WINDOW + tile, :] = top.astype(BF16)
        kbot_s[g, WINDOW:WINDOW + tile, :] = bot.astype(BF16)
        top, bot = _split_group(v[:, s * LANES:(s + 1) * LANES], g, lo64)
        vtop_s[g, WINDOW:WINDOW + tile, 0:LANES] = top.astype(BF16)
        vbot_s[g, WINDOW:WINDOW + tile, 0:LANES] = bot.astype(BF16)

    qscale = qn_ref[...] * (B_HEAD_DIM ** -0.5 * LOG2E)
    qcos = [cos * qscale[:, s * LANES:(s + 1) * LANES] for s in range(kvw // LANES)]
    qsin = [sin * jnp.where(first_half[0:1], pltpu.roll(qscale[:, s * LANES:(s + 1) * LANES], 96, 1),
                            pltpu.roll(qscale[:, s * LANES:(s + 1) * LANES], 32, 1)) for s in range(kvw // LANES)]

    q_s[...] = _dot(ub_s[...], wq_ref[...])
    for c in range(BRANCH_WIDTH // kvw):
        x = q_s[:, c * kvw:(c + 1) * kvw]
        hi, lo = _split2(x * x)
        rinv = lax.rsqrt((_dot(hi, bd) + _dot(lo, bd)) * (1.0 / B_HEAD_DIM) + RMS_EPS)
        for s in range(kvw // LANES):
            xs = x[:, s * LANES:(s + 1) * LANES]
            xr = jnp.where(first_half, pltpu.roll(xs, 96, 1), pltpu.roll(xs, 32, 1))
            qr = ((xs * qcos[s] + xr * qsin[s]) * rinv[:, s * LANES:(s + 1) * LANES]).astype(BF16)
            for i in range(tile // blk):
                qb_s[i, c * (kvw // LANES) + s] = qr[i * blk:(i + 1) * blk]

    rowi = lax.broadcasted_iota(jnp.int32, (blk, 2 * blk), 0)
    coli = lax.broadcasted_iota(jnp.int32, (blk, 2 * blk), 1)
    band = jnp.logical_and(coli - rowi >= 0, coli - rowi <= WINDOW)
    _, lo64_b = _lane_masks(blk)

    def attn_pair(idx):
        i = idx // B_KV_HEADS
        g = idx % B_KV_HEADS
        r0 = pl.multiple_of(i * blk, blk)
        first_col = jnp.where(l * tile + r0 > 0, 0, WINDOW)
        mask = jnp.logical_and(band, coli >= first_col)
        kpad = jnp.concatenate([ktop_s[g, pl.ds(r0, 2 * blk), :], kbot_s[g, pl.ds(r0, 2 * blk), :]], axis=0)
        vpad = jnp.concatenate([vtop_s[g, pl.ds(r0, 2 * blk), :], vbot_s[g, pl.ds(r0, 2 * blk), :]], axis=0)
        qg = qb_s[i, pl.ds(g * pairs, pairs)].reshape(pairs * blk, LANES)
        s = _dot_nt(qg, kpad)
        ps, esinks = [], []
        for j in range(pairs):
            pj, ej = [], []
            for half in range(2):
                sink = sink_ref[g * B_GROUP + 2 * j + half] * LOG2E
                sh = jnp.where(mask, s[j * blk:(j + 1) * blk, half * 2 * blk:(half + 1) * 2 * blk], -jnp.inf)
                m = jnp.maximum(jnp.max(sh, axis=-1, keepdims=True), sink)
                pj.append(jnp.exp2(sh - m).astype(BF16))
                ej.append(jnp.exp2(sink - m))
            ps.append(jnp.concatenate(pj, axis=1))
            esinks.append(jnp.where(lo64_b, ej[0], ej[1]))
        o = _dot(jnp.concatenate(ps, axis=0), vpad)
        for j in range(pairs):
            oj = o[j * blk:(j + 1) * blk]
            osl_s[g * pairs + j, pl.ds(r0, blk), :] = oj[:, :LANES] / (oj[:, LANES:] + esinks[j])

    def attn_body(k, carry):
        gc = _dot(ub_s[...], wg_ref[k])
        for s in range(gate_cols // LANES):
            gsl_s[k * (gate_cols // LANES) + s] = gc[:, s * LANES:(s + 1) * LANES]
        attn_pair(2 * k)
        attn_pair(2 * k + 1)
        return carry

    lax.fori_loop(0, n_iter // 2, attn_body, 0, unroll=4)

    for buf in (ktop_s, kbot_s):
        buf[:, 0:WINDOW, :] = buf[:, tile:tile + WINDOW, :]
    for buf in (vtop_s, vbot_s):
        buf[:, 0:WINDOW, 0:LANES] = buf[:, tile:tile + WINDOW, 0:LANES]

    y = h
    per = n_slabs // OUT_PROJ_CHUNKS
    for c in range(OUT_PROJ_CHUNKS):
        for sl in range(c * per, (c + 1) * per):
            og_s[:, sl * LANES:(sl + 1) * LANES] = (osl_s[sl] * _silu(gsl_s[sl])).astype(BF16)
        cols = slice(c * per * LANES, (c + 1) * per * LANES)
        y = y + _dot(og_s[:, cols], wout_ref[cols, :])
    y_ref[...] = y

    @pl.when(l == pl.num_programs(1) - 1)
    def _():
        kc_ref[...] = kv_s[tile - WINDOW:, :kvw]
        vc_ref[...] = kv_s[tile - WINDOW:, kvw:]


def _swa_prompt(h, kv_norm, wkv, k_norm, bd, cos, sin, b_norm, wq, wg, q_norm, sinks, wout):
    bsz, seq, d = h.shape
    tile = min(PROMPT_TILE, seq)
    assert seq % tile == 0 and tile % WINDOW == 0 and seq >= WINDOW
    n_iter = (tile // WINDOW) * B_KV_HEADS
    gate_cols = BRANCH_WIDTH // (n_iter // 2)
    assert gate_cols % LANES == 0
    wg = wg.reshape(d, n_iter // 2, gate_cols).transpose(1, 0, 2)
    const = lambda shape: pl.BlockSpec(shape, lambda b, l: (0,) * len(shape), pipeline_mode=pl.Buffered(1))
    k_scratch = pltpu.VMEM((B_KV_HEADS, WINDOW + tile, LANES), BF16)
    v_scratch = pltpu.VMEM((B_KV_HEADS, WINDOW + tile, 2 * LANES), BF16)
    n_slabs = BRANCH_WIDTH // LANES
    return pl.pallas_call(
        _swa_prompt_kernel,
        grid=(bsz, seq // tile),
        in_specs=[
            pl.BlockSpec((None, tile, d), lambda b, l: (b, l, 0)),
            const(kv_norm.shape), const(wkv.shape), const(k_norm.shape), const(bd.shape),
            pl.BlockSpec((tile, LANES), lambda b, l: (l, 0)),
            pl.BlockSpec((tile, LANES), lambda b, l: (l, 0)),
            const(b_norm.shape), const(wq.shape), const(wg.shape), const(q_norm.shape),
            pl.BlockSpec(memory_space=pltpu.SMEM),
            const(wout.shape),
        ],
        out_specs=[
            pl.BlockSpec((None, tile, d), lambda b, l: (b, l, 0)),
            pl.BlockSpec((None, WINDOW, B_KV_WIDTH), lambda b, l: (b, 0, 0)),
            pl.BlockSpec((None, WINDOW, B_KV_WIDTH), lambda b, l: (b, 0, 0)),
        ],
        out_shape=[
            jax.ShapeDtypeStruct((bsz, seq, d), F32),
            jax.ShapeDtypeStruct((bsz, WINDOW, B_KV_WIDTH), F32),
            jax.ShapeDtypeStruct((bsz, WINDOW, B_KV_WIDTH), F32),
        ],
        scratch_shapes=[
            k_scratch, k_scratch, v_scratch, v_scratch,
            pltpu.VMEM((tile // WINDOW, n_slabs, WINDOW, LANES), BF16),
            pltpu.VMEM((tile, d), BF16),
            pltpu.VMEM((tile, 2 * B_KV_WIDTH), F32),
            pltpu.VMEM((tile, BRANCH_WIDTH), F32),
            pltpu.VMEM((n_slabs, tile, LANES), F32),
            pltpu.VMEM((n_slabs, tile, LANES), F32),
            pltpu.VMEM((tile, BRANCH_WIDTH), BF16),
        ],
        compiler_params=pltpu.CompilerParams(
            dimension_semantics=("arbitrary", "arbitrary"), vmem_limit_bytes=VMEM_LIMIT_BYTES),
        name="swa_prompt",
    )(h, kv_norm, wkv, k_norm, bd, cos, sin, b_norm, wq, wg, q_norm, sinks, wout)


def _gla_proj_sample_kernel(x_ref, g_ref, wqk_ref, wv_ref, wg_ref, wlow_ref, qk_ref, v_ref, gate_ref, glow_ref):
    u = _rmsnorm(x_ref[...], g_ref[...]).astype(BF16)
    qk_ref[...] = _dot(u, wqk_ref[...])
    v_ref[...] = _dot(u, wv_ref[...])
    gate_ref[...] = _dot(u, wg_ref[...])
    glow_ref[...] = _dot(u, wlow_ref[...])


def _gla_proj_sample(x, g, wqk, wv, wg, wlow):
    m = x.shape[0]
    return pl.pallas_call(
        _gla_proj_sample_kernel,
        out_shape=[jax.ShapeDtypeStruct((m, w.shape[1]), F32) for w in (wqk, wv, wg, wlow)],
        compiler_params=pltpu.CompilerParams(vmem_limit_bytes=VMEM_LIMIT_BYTES),
        name="gla_proj_sample",
    )(x, g, wqk, wv, wg, wlow)


def _gla_out_kernel(o_ref, gate_ref, x_ref, on_ref, w_ref, y_ref, og_s):
    for h in range(A_HEADS):
        vs = slice(h * A_HEAD_DV, (h + 1) * A_HEAD_DV)
        o = _rmsnorm(o_ref[:, vs], on_ref[...])
        og_s[:, vs] = (o * _silu(gate_ref[:, vs])).astype(BF16)
    y_ref[...] = _dot(og_s[...], w_ref[...]) + x_ref[...]


def _gla_out(o, gate, x, out_norm, wout):
    m = o.shape[0]
    return pl.pallas_call(
        _gla_out_kernel,
        out_shape=jax.ShapeDtypeStruct(x.shape, F32),
        scratch_shapes=[pltpu.VMEM((m, BRANCH_WIDTH), BF16)],
        compiler_params=pltpu.CompilerParams(vmem_limit_bytes=VMEM_LIMIT_BYTES),
        name="gla_out_sample",
    )(o, gate, x, out_norm, wout)


def _out_proj_kernel(og_ref, x_ref, w_ref, y_ref):
    y_ref[...] = _dot(og_ref[...].astype(BF16), w_ref[...]) + x_ref[...]


def _out_proj(og, x, wout):
    return pl.pallas_call(
        _out_proj_kernel,
        out_shape=jax.ShapeDtypeStruct(x.shape, F32),
        compiler_params=pltpu.CompilerParams(vmem_limit_bytes=VMEM_LIMIT_BYTES),
        name="swa_out_sample",
    )(og, x, wout)


def _gla_sample_kernel(qk_ref, v_ref, glow_ref, wg2_ref, bg_ref, st_ref, o_ref, nst_ref, *, seq):
    nb = st_ref.shape[0]
    dk, dv, kw = A_HEAD_DK, A_HEAD_DV, A_KEY_WIDTH
    scale = A_HEAD_DK ** -0.5
    per_tile = SUBLANES // seq

    def one_seq(i, qk, v, loga):
        q = qk[:, 0:kw] * scale
        k = qk[:, kw:2 * kw]
        bs = [loga[0:1]]
        for t in range(1, seq):
            bs.append(bs[-1] + loga[t:t + 1])
        bmat = jnp.concatenate(bs, axis=0)
        blast = bs[-1]
        qt = q * jnp.exp(bmat)
        kd = k * jnp.exp(blast - bmat)
        eblast = jnp.exp(blast)

        outs = [[jnp.zeros((1, dv), F32) for _ in range(A_HEADS)] for _ in range(seq)]
        for t in range(seq):
            for s in range(t + 1):
                w = q[t:t + 1] * k[s:s + 1] * jnp.exp(bs[t] - bs[s])
                for h in range(A_HEADS):
                    a = jnp.sum(w[:, h * dk:(h + 1) * dk], axis=-1, keepdims=True)
                    outs[t][h] = outs[t][h] + a * v[s:s + 1, h * dv:(h + 1) * dv]
        intra = jnp.concatenate([jnp.concatenate(outs[t], axis=1) for t in range(seq)], axis=0)

        qt8 = jnp.concatenate([qt, jnp.zeros((SUBLANES - seq, kw), F32)], axis=0).astype(BF16)
        inter = []
        for h in range(A_HEADS):
            ks = slice(h * dk, (h + 1) * dk)
            state = st_ref[i, h]
            inter.append(_dot(qt8[:, ks], state.astype(BF16))[0:seq])
            m = jnp.concatenate([kd[:, ks], eblast[:, ks], jnp.zeros((dk - seq - 1, dk), F32)], axis=0)
            mt = jnp.transpose(m)
            vpad = jnp.concatenate([v[:, h * dv:(h + 1) * dv], jnp.zeros((dk - seq, dv), F32)], axis=0)
            nst_ref[i, h] = state * mt[:, seq:seq + 1] + _dot(mt.astype(BF16), vpad.astype(BF16))
        return intra + jnp.concatenate(inter, axis=1)

    def per_tile_body(j, carry):
        rows = pl.ds(pl.multiple_of(j * SUBLANES, SUBLANES), SUBLANES)
        qk8, v8 = qk_ref[rows, :], v_ref[rows, :]
        pre = _dot(glow_ref[rows, :].astype(BF16), wg2_ref[...]) + bg_ref[...]
        loga8 = _log_sigmoid(pre) * (1.0 / A_GATE_NORMALIZER)
        outs = []
        for p in range(per_tile):
            r = slice(p * seq, (p + 1) * seq)
            outs.append(one_seq(j * per_tile + p, qk8[r], v8[r], loga8[r]))
        o_ref[rows, :] = jnp.concatenate(outs, axis=0)
        return carry

    lax.fori_loop(0, nb // per_tile, per_tile_body, 0)


def _gla_sample(qk, v, glow, wg2, bg, state, seq):
    nbatch = state.shape[0]
    nb = min(SAMPLE_BATCH_BLOCK // 2, nbatch)
    assert nbatch % nb == 0 and SUBLANES % seq == 0 and (nb * seq) % SUBLANES == 0
    st_spec = pl.BlockSpec((nb, A_HEADS, A_HEAD_DK, A_HEAD_DV), lambda b: (b, 0, 0, 0))
    tok_spec = lambda a: pl.BlockSpec((nb * seq, a.shape[1]), lambda b: (b, 0))
    return pl.pallas_call(
        functools.partial(_gla_sample_kernel, seq=seq),
        grid=(nbatch // nb,),
        in_specs=[
            tok_spec(qk), tok_spec(v), tok_spec(glow),
            pl.BlockSpec(wg2.shape, lambda b: (0, 0)),
            pl.BlockSpec(bg.shape, lambda b: (0, 0)),
            st_spec,
        ],
        out_specs=[pl.BlockSpec((nb * seq, BRANCH_WIDTH), lambda b: (b, 0)), st_spec],
        out_shape=[
            jax.ShapeDtypeStruct((nbatch * seq, BRANCH_WIDTH), F32),
            jax.ShapeDtypeStruct(state.shape, F32),
        ],
        compiler_params=pltpu.CompilerParams(
            dimension_semantics=("arbitrary",), vmem_limit_bytes=VMEM_LIMIT_BYTES),
        name="gla_sample",
    )(qk, v, glow, wg2, bg, state)


def _swa_pre_sample_kernel(h_ref, kvn_ref, wkv_ref, kn_ref, bd_ref, cos_ref, sin_ref, bn_ref, wq_ref, wg_ref, qn_ref,
                           k_ref, v_ref, q_ref, gate_ref):
    m = h_ref.shape[0]
    kvw = B_KV_WIDTH
    h = h_ref[...]
    cos = cos_ref[...]
    sin = sin_ref[...]
    first_half, _ = _lane_masks(m)
    bd = bd_ref[...]

    kv = _dot(_rmsnorm(h, kvn_ref[...]).astype(BF16), wkv_ref[...])
    kn = _head_norm(kv[:, :kvw], bd, kn_ref[...])
    for s in range(kvw // LANES):
        k_ref[:, s * LANES:(s + 1) * LANES] = _rope_slab(kn[:, s * LANES:(s + 1) * LANES], cos, sin, first_half)
    v_ref[...] = kv[:, kvw:]

    ub = _rmsnorm(h, bn_ref[...]).astype(BF16)
    _, lo64 = _lane_masks(m)
    qgain = qn_ref[...] * (B_HEAD_DIM ** -0.5 * LOG2E)
    gate = _dot(ub, wg_ref[...])
    q_slabs, g_slabs = [], []
    for c in range(BRANCH_WIDTH // kvw):
        qc = _head_norm(_dot(ub, wq_ref[:, c * kvw:(c + 1) * kvw]), bd, qgain)
        for s in range(kvw // LANES):
            q_slabs.append(_rope_slab(qc[:, s * LANES:(s + 1) * LANES], cos, sin, first_half))
            g_slabs.append(gate[:, c * kvw + s * LANES:c * kvw + (s + 1) * LANES])
    per_group = B_GROUP // 2
    for r in range(B_GROUP):
        for p in range(B_KV_HEADS // 2):
            a = 2 * p * per_group + r // 2
            b = (2 * p + 1) * per_group + r // 2
            lanes = slice((2 * r + p) * LANES, (2 * r + p + 1) * LANES)
            q_ref[:, lanes] = _regroup_slab(q_slabs[a], q_slabs[b], r % 2, lo64)
            gate_ref[:, lanes] = _regroup_slab(g_slabs[a], g_slabs[b], r % 2, lo64)


def _swa_pre_sample(h, kv_norm, wkv, k_norm, bd, cos, sin, b_norm, wq, wg, q_norm):
    m = h.shape[0]
    return pl.pallas_call(
        _swa_pre_sample_kernel,
        out_shape=[
            jax.ShapeDtypeStruct((m, B_KV_WIDTH), F32),
            jax.ShapeDtypeStruct((m, B_KV_WIDTH), F32),
            jax.ShapeDtypeStruct((m, BRANCH_WIDTH), F32),
            jax.ShapeDtypeStruct((m, BRANCH_WIDTH), F32),
        ],
        compiler_params=pltpu.CompilerParams(vmem_limit_bytes=VMEM_LIMIT_BYTES),
        name="swa_pre_sample",
    )(h, kv_norm, wkv, k_norm, bd, cos, sin, b_norm, wq, wg, q_norm)


def _swa_sample_kernel(q_ref, gate_ref, kn_ref, vn_ref, ck_ref, cv_ref, sink_ref, og_ref, nk_ref, nv_ref, *, seq):
    nb = ck_ref.shape[0]
    per_tile = SUBLANES // seq
    kvw = B_KV_WIDTH
    trows = SUBLANES
    nrows = B_GROUP * B_KV_HEADS * trows
    rowi = lax.broadcasted_iota(jnp.int32, (nrows, 2 * WINDOW), 0)
    coli = lax.broadcasted_iota(jnp.int32, (nrows, 2 * WINDOW), 1)
    dpos = coli - rowi % trows
    mask = jnp.logical_and(dpos >= 0, dpos <= WINDOW)
    lane8 = lax.broadcasted_iota(jnp.int32, (trows, kvw), 1) // B_HEAD_DIM
    grow = (lax.broadcasted_iota(jnp.int32, (nrows, kvw), 0) // trows) % B_KV_HEADS
    gkeep = grow == lax.broadcasted_iota(jnp.int32, (nrows, kvw), 1) // B_HEAD_DIM
    sink = sink_ref[...][:, 0:1] * LOG2E
    zq = jnp.zeros((trows - seq, BRANCH_WIDTH), F32)
    lo64_q = lax.broadcasted_iota(jnp.int32, (seq, LANES), 1) < B_HEAD_DIM
    zk = jnp.zeros((WINDOW - seq, kvw), F32)

    def one_seq(i, q4, gate4, k_new, v_new):
        ck = ck_ref[i]
        cv = cv_ref[i]
        nk_ref[i, 0:WINDOW - seq, :] = ck[seq:, :]
        nk_ref[i, WINDOW - seq:WINDOW, :] = k_new
        nv_ref[i, 0:WINDOW - seq, :] = cv[seq:, :]
        nv_ref[i, WINDOW - seq:WINDOW, :] = v_new

        q8 = jnp.concatenate([q4, zq], axis=0)
        pieces = []
        for r in range(B_GROUP):
            slab = q8[:, r * kvw:(r + 1) * kvw]
            for g in range(B_KV_HEADS):
                pieces.append(jnp.where(lane8 == g, slab, 0.0))
        qrows = jnp.concatenate(pieces, axis=0).astype(BF16)
        kpad = jnp.concatenate([k_new, zk], axis=0).astype(BF16)
        vpad = jnp.concatenate([v_new, zk], axis=0).astype(BF16)
        s = jnp.concatenate([_dot_nt(qrows, ck.astype(BF16)), _dot_nt(qrows, kpad)], axis=1)
        s = jnp.where(mask, s, -jnp.inf)
        m = jnp.maximum(jnp.max(s, axis=-1, keepdims=True), sink)
        p = jnp.exp2(s - m)
        den = jnp.sum(p, axis=-1, keepdims=True) + jnp.exp2(sink - m)
        pb = p.astype(BF16)
        o = _dot(pb[:, :WINDOW], cv.astype(BF16)) + _dot(pb[:, WINDOW:], vpad)
        o = jnp.where(gkeep, o / den, 0.0)
        outs = []
        for r in range(B_GROUP):
            acc = o[r * B_KV_HEADS * trows:r * B_KV_HEADS * trows + trows]
            for g in range(1, B_KV_HEADS):
                acc = acc + o[(r * B_KV_HEADS + g) * trows:(r * B_KV_HEADS + g + 1) * trows]
            outs.append(acc)
        gated = jnp.concatenate(outs, axis=1)[0:seq] * _silu(gate4)
        slabs = []
        for g in range(B_KV_HEADS):
            for j in range(B_GROUP // 2):
                a = gated[:, (4 * j + g // 2) * LANES:(4 * j + g // 2 + 1) * LANES]
                b = gated[:, (4 * j + 2 + g // 2) * LANES:(4 * j + 2 + g // 2 + 1) * LANES]
                slabs.append(_regroup_slab(a, b, g % 2, lo64_q))
        return jnp.concatenate(slabs, axis=1)

    def per_tile_body(j, carry):
        rows = pl.ds(pl.multiple_of(j * SUBLANES, SUBLANES), SUBLANES)
        q8, gate8, kn8, vn8 = q_ref[rows, :], gate_ref[rows, :], kn_ref[rows, :], vn_ref[rows, :]
        outs = []
        for p in range(per_tile):
            r = slice(p * seq, (p + 1) * seq)
            outs.append(one_seq(j * per_tile + p, q8[r], gate8[r], kn8[r], vn8[r]))
        og_ref[rows, :] = jnp.concatenate(outs, axis=0)
        return carry

    lax.fori_loop(0, nb // per_tile, per_tile_body, 0, unroll=4)


def _swa_sample(q, gate, k_new, v_new, ck, cv, sink_rows, seq):
    nbatch = ck.shape[0]
    nb = min(SAMPLE_BATCH_BLOCK, nbatch)
    assert nbatch % nb == 0 and ck.shape[1] == WINDOW and SUBLANES % seq == 0 and (nb * seq) % SUBLANES == 0
    cache_spec = pl.BlockSpec((nb, WINDOW, B_KV_WIDTH), lambda b: (b, 0, 0))
    wide_spec = pl.BlockSpec((nb * seq, BRANCH_WIDTH), lambda b: (b, 0))
    new_spec = pl.BlockSpec((nb * seq, B_KV_WIDTH), lambda b: (b, 0))
    return pl.pallas_call(
        functools.partial(_swa_sample_kernel, seq=seq),
        grid=(nbatch // nb,),
        in_specs=[wide_spec, wide_spec, new_spec, new_spec, cache_spec, cache_spec,
                  pl.BlockSpec(sink_rows.shape, lambda b: (0, 0))],
        out_specs=[wide_spec, cache_spec, cache_spec],
        out_shape=[
            jax.ShapeDtypeStruct((nbatch * seq, BRANCH_WIDTH), F32),
            jax.ShapeDtypeStruct(ck.shape, F32),
            jax.ShapeDtypeStruct(cv.shape, F32),
        ],
        compiler_params=pltpu.CompilerParams(
            dimension_semantics=("arbitrary",), vmem_limit_bytes=VMEM_LIMIT_BYTES),
        name="swa_sample",
    )(q, gate, k_new, v_new, ck, cv, sink_rows)


def _rope_tables(first_pos, n_pos, repeat=1):
    half = B_HEAD_DIM // 2
    inv_freq = ROPE_THETA ** (-np.arange(half, dtype=np.float64) / half)
    ang = (first_pos + np.arange(n_pos, dtype=np.float64))[:, None] * inv_freq[None, :]
    cos, sin = np.cos(ang), np.sin(ang)
    cos_t = np.tile(np.concatenate([cos, cos, cos, cos], axis=1), (repeat, 1))
    sin_t = np.tile(np.concatenate([-sin, sin, -sin, sin], axis=1), (repeat, 1))
    return jnp.asarray(cos_t, F32), jnp.asarray(sin_t, F32)


def kernel(x_prompt, x_sample, state_gla, cache_swa_k, cache_swa_v, a_norm, a_w_in, a_w_gate2, a_b_gate,
           a_out_norm, a_w_out, kv_norm, w_k, w_v, k_norm, b_norm, b_w_in, b_q_norm, b_sinks, b_w_out):
    assert a_norm.shape[0] == 1 and b_norm.shape[0] == 1
    bsz_p, seq_p, d = x_prompt.shape
    bsz_s, seq_s, _ = x_sample.shape
    kw, bw = A_KEY_WIDTH, BRANCH_WIDTH

    w_in = a_w_in[0]
    wa_qk = w_in[:, :2 * kw].astype(BF16)
    wa_v = w_in[:, 2 * kw:2 * kw + bw].astype(BF16)
    wa_g = w_in[:, 2 * kw + bw:2 * kw + 2 * bw].astype(BF16)
    wa_low = jnp.pad(w_in[:, 2 * kw + 2 * bw:].astype(BF16), ((0, 0), (0, LANES - A_GATE_RANK)))
    wa_g2 = jnp.pad(a_w_gate2[0].astype(BF16), ((0, LANES - A_GATE_RANK), (0, 0)))
    a_bg = a_b_gate[0][None, :]
    a_n = a_norm[0][None, :]
    a_on = a_out_norm[0][None, :]
    wa_out = a_w_out[0].astype(BF16)
    w_kv = jnp.concatenate([w_k.astype(BF16), w_v.astype(BF16)], axis=1)
    wb_q = b_w_in[0][:, :bw].astype(BF16)
    wb_g = b_w_in[0][:, bw:].astype(BF16)
    wb_out = b_w_out[0].astype(BF16)
    kvn = kv_norm[None, :]
    bn = b_norm[0][None, :]
    kn_t = jnp.tile(k_norm, B_KV_WIDTH // B_HEAD_DIM)[None, :]
    qn_t = jnp.tile(b_q_norm[0], B_KV_WIDTH // B_HEAD_DIM)[None, :]
    sinks = b_sinks[0]
    grp = np.arange(B_KV_WIDTH) // B_HEAD_DIM
    bd = jnp.asarray(grp[:, None] == grp[None, :], BF16)

    cos_p, sin_p = _rope_tables(0, seq_p)
    h1_p, st_p = _gla_prompt(x_prompt, a_n, wa_qk, wa_v, wa_g, wa_low, wa_g2, a_bg, a_on, wa_out)
    y_p, kc_p, vc_p = _swa_prompt(h1_p, kvn, w_kv, kn_t, bd, cos_p, sin_p, bn, wb_q, wb_g, qn_t, sinks, wb_out)

    m = bsz_s * seq_s
    xs = x_sample.reshape(m, d)
    qk_a, v_a, gate_a, glow_a = _gla_proj_sample(xs, a_n, wa_qk, wa_v, wa_g, wa_low)
    o_s, st_s = _gla_sample(qk_a, v_a, glow_a, wa_g2, a_bg, state_gla[0], seq_s)
    h1_s = _gla_out(o_s, gate_a, xs, a_on, wa_out)

    sink_rows = jnp.broadcast_to(
        sinks.reshape(B_KV_HEADS, B_GROUP).T[:, :, None, None], (B_GROUP, B_KV_HEADS, SUBLANES, LANES)
    ).reshape(B_GROUP * B_KV_HEADS * SUBLANES, LANES)
    cos_s, sin_s = _rope_tables(PAST_LEN, seq_s, repeat=bsz_s)
    k_new, v_new, q_s, gate_b = _swa_pre_sample(h1_s, kvn, w_kv, kn_t, bd, cos_s, sin_s, bn, wb_q, wb_g, qn_t)
    og_s, nk_s, nv_s = _swa_sample(
        q_s, gate_b, k_new, v_new,
        cache_swa_k.reshape(bsz_s, WINDOW, B_KV_WIDTH), cache_swa_v.reshape(bsz_s, WINDOW, B_KV_WIDTH), sink_rows, seq_s)
    y_s = _out_proj(og_s, h1_s, wb_out)

    return (y_p, y_s.reshape(bsz_s, seq_s, d),
            st_p[None], st_s[None],
            kc_p.reshape(bsz_p, WINDOW, B_KV_HEADS, B_HEAD_DIM), vc_p.reshape(bsz_p, WINDOW, B_KV_HEADS, B_HEAD_DIM),
            nk_s.reshape(bsz_s, WINDOW, B_KV_HEADS, B_HEAD_DIM), nv_s.reshape(bsz_s, WINDOW, B_KV_HEADS, B_HEAD_DIM))
```

```python
import functools

import jax
import jax.numpy as jnp
import numpy as np
from jax import lax
from jax.experimental import pallas as pl
from jax.experimental.pallas import tpu as pltpu

F32 = jnp.float32
BF16 = jnp.bfloat16

A_HEADS = 4
A_HEAD_DK = 128
A_HEAD_DV = 512
A_KEY_WIDTH = A_HEADS * A_HEAD_DK
BRANCH_WIDTH = A_HEADS * A_HEAD_DV
A_GATE_RANK = 16
A_GATE_NORMALIZER = 16.0
B_HEAD_DIM = 64
B_HEADS = BRANCH_WIDTH // B_HEAD_DIM
B_KV_HEADS = 4
B_GROUP = B_HEADS // B_KV_HEADS
B_KV_WIDTH = B_KV_HEADS * B_HEAD_DIM
WINDOW = 128
ROPE_THETA = 10000.0
RMS_EPS = 1e-6
PAST_LEN = 16384

LANES = 128
SUBLANES = 8
VMEM_LIMIT_BYTES = 60 * 1024 * 1024

PROMPT_TILE = 512
GLA_CHUNK = 256
GLA_SAFE_DECAY = 80.0
SAMPLE_BATCH_BLOCK = 8
OUT_PROJ_CHUNKS = 4
LOG2E = 1.4426950408889634


def _dot(a, b):
    return jnp.dot(a, b, preferred_element_type=F32)


def _dot_nt(a, b):
    return lax.dot_general(a, b, (((1,), (1,)), ((), ())), preferred_element_type=F32)


def _split2(x):
    hi = x.astype(BF16)
    lo = (x - hi.astype(F32)).astype(BF16)
    return hi, lo


def _rmsnorm(x, g):
    ms = jnp.mean(x * x, axis=-1, keepdims=True)
    return x * lax.rsqrt(ms + RMS_EPS) * g


def _log_sigmoid(x):
    return jnp.minimum(x, 0.0) - jnp.log1p(jnp.exp(-jnp.abs(x)))


def _silu(x):
    return x * (1.0 / (1.0 + jnp.exp(-x)))


def _head_norm(x, bd, gain):
    hi, lo = _split2(x * x)
    ss = _dot(hi, bd) + _dot(lo, bd)
    return x * lax.rsqrt(ss * (1.0 / B_HEAD_DIM) + RMS_EPS) * gain


def _rope_slab(x, cos, sin_signed, first_half):
    xr = jnp.where(first_half, pltpu.roll(x, 96, 1), pltpu.roll(x, 32, 1))
    return x * cos + xr * sin_signed


def _lane_masks(rows):
    lane = lax.broadcasted_iota(jnp.int32, (rows, LANES), 1)
    first_half = (lane % B_HEAD_DIM) < (B_HEAD_DIM // 2)
    lo64 = lane < B_HEAD_DIM
    return first_half, lo64


def _split_group(slab, g, lo64):
    swapped = pltpu.roll(slab, B_HEAD_DIM, 1)
    zero = jnp.zeros_like(slab)
    if g % 2 == 0:
        return jnp.where(lo64, slab, zero), jnp.where(lo64, zero, swapped)
    return jnp.where(lo64, swapped, zero), jnp.where(lo64, zero, slab)


def _regroup_slab(a, b, odd, lo64):
    if odd:
        return jnp.where(lo64, pltpu.roll(a, B_HEAD_DIM, 1), b)
    return jnp.where(lo64, a, pltpu.roll(b, B_HEAD_DIM, 1))


def _gla_prompt_kernel(x_ref, an_ref, wqk_ref, wv_ref, wg_ref, wlow_ref, wg2_ref, bg_ref, on_ref, wout_ref,
                       h1_ref, st_ref,
                       qk_s, v_s, gate_s, b_s, o_s, oi_s):
    tile = x_ref.shape[0]
    chunk = GLA_CHUNK
    n_chunks = tile // chunk
    dk, dv, kw = A_HEAD_DK, A_HEAD_DV, A_KEY_WIDTH

    @pl.when(pl.program_id(1) == 0)
    def _():
        st_ref[...] = jnp.zeros_like(st_ref)

    x = x_ref[...]
    u = _rmsnorm(x, an_ref[...]).astype(BF16)
    qk_s[...] = _dot(u, wqk_ref[...])
    v_s[...] = _dot(u, wv_ref[...])
    gate_s[...] = _dot(u, wg_ref[...])
    glow = _dot(u, wlow_ref[...]).astype(BF16)
    pre = _dot(glow, wg2_ref[...]) + bg_ref[...]
    loga = _log_sigmoid(pre) * (1.0 / A_GATE_NORMALIZER)

    row = lax.broadcasted_iota(jnp.int32, (chunk, chunk), 0)
    col = lax.broadcasted_iota(jnp.int32, (chunk, chunk), 1)
    lower = row >= col
    tri = jnp.where(lower, 1.0, 0.0).astype(BF16)
    for c in range(n_chunks):
        hi, lo = _split2(loga[c * chunk:(c + 1) * chunk])
        b_s[c * chunk:(c + 1) * chunk, :] = _dot(tri, hi) + _dot(tri, lo)
    safe = jnp.min(b_s[...]) >= -GLA_SAFE_DECAY

    scale = A_HEAD_DK ** -0.5
    for c in range(n_chunks):
        rows = slice(c * chunk, (c + 1) * chunk)
        b = b_s[rows, :]
        eb = jnp.exp(b)
        enb = jnp.exp(-b)
        blast = b[chunk - 1:chunk, :]
        ekl = jnp.exp(blast - b)
        eblast = jnp.exp(blast)
        for h in range(A_HEADS):
            ks = slice(h * dk, (h + 1) * dk)
            vs = slice(h * dv, (h + 1) * dv)
            q = qk_s[rows, ks] * scale
            k = qk_s[rows, kw + h * dk: kw + (h + 1) * dk]
            vh = v_s[rows, vs].astype(BF16)
            state = st_ref[h]
            qt = (q * eb[:, ks]).astype(BF16)
            o_inter = _dot(qt, state.astype(BF16))
            a = _dot_nt(qt, (k * enb[:, ks]).astype(BF16))
            a = jnp.where(lower, a, 0.0).astype(BF16)
            oi_s[rows, vs] = o_inter
            o_s[rows, vs] = o_inter + _dot(a, vh)
            kd_t = jnp.transpose(k * ekl[:, ks]).astype(BF16)
            dec = jnp.transpose(jnp.broadcast_to(eblast[:, ks], (dk, dk)))
            st_ref[h] = state * jnp.concatenate([dec] * (dv // dk), axis=1) + _dot(kd_t, vh)

    def finish():
        y = x
        for h in range(A_HEADS):
            vs = slice(h * dv, (h + 1) * dv)
            o = _rmsnorm(o_s[:, vs], on_ref[...])
            og = (o * _silu(gate_s[:, vs])).astype(BF16)
            y = y + _dot(og, wout_ref[vs, :])
        h1_ref[...] = y

    finish()

    @pl.when(jnp.logical_not(safe))
    def _():
        trow = lax.broadcasted_iota(jnp.int32, (chunk, dk), 0)
        sub8 = lax.broadcasted_iota(jnp.int32, (SUBLANES, 1), 0)
        for c in range(n_chunks):
            rows = slice(c * chunk, (c + 1) * chunk)
            for h in range(A_HEADS):
                ks = slice(h * dk, (h + 1) * dk)
                vs = slice(h * dv, (h + 1) * dv)
                q = qk_s[rows, ks] * scale
                b = b_s[rows, ks]

                def body(s, acc, c=c, h=h, q=q, b=b):
                    r8 = pl.multiple_of(c * chunk + (s // SUBLANES) * SUBLANES, SUBLANES)
                    pick = sub8 == s % SUBLANES

                    def row_of(ref, lanes):
                        return jnp.sum(jnp.where(pick, ref[pl.ds(r8, SUBLANES), lanes], 0.0), axis=0, keepdims=True)

                    brow = row_of(b_s, slice(h * dk, (h + 1) * dk))
                    krow = row_of(qk_s, slice(kw + h * dk, kw + (h + 1) * dk))
                    vrow = row_of(v_s, slice(h * dv, (h + 1) * dv))
                    w = jnp.where(trow >= s, jnp.exp(jnp.minimum(b - brow, 0.0)), 0.0)
                    colv = jnp.sum(q * krow * w, axis=-1, keepdims=True)
                    return acc + colv * vrow

                o_s[rows, vs] = oi_s[rows, vs] + lax.fori_loop(0, chunk, body, jnp.zeros((chunk, dv), F32))
        finish()


def _gla_prompt(x, a_norm, wqk, wv, wg, wlow, wg2, bg, out_norm, wout):
    bsz, seq, d = x.shape
    tile = min(PROMPT_TILE, seq)
    assert seq % tile == 0 and tile % GLA_CHUNK == 0
    const = lambda shape: pl.BlockSpec(shape, lambda b, l: (0,) * len(shape), pipeline_mode=pl.Buffered(1))
    return pl.pallas_call(
        _gla_prompt_kernel,
        grid=(bsz, seq // tile),
        in_specs=[
            pl.BlockSpec((None, tile, d), lambda b, l: (b, l, 0)),
            const(a_norm.shape), const(wqk.shape), const(wv.shape), const(wg.shape), const(wlow.shape),
            const(wg2.shape), const(bg.shape), const(out_norm.shape), const(wout.shape),
        ],
        out_specs=[
            pl.BlockSpec((None, tile, d), lambda b, l: (b, l, 0)),
            pl.BlockSpec((None, A_HEADS, A_HEAD_DK, A_HEAD_DV), lambda b, l: (b, 0, 0, 0)),
        ],
        out_shape=[
            jax.ShapeDtypeStruct((bsz, seq, d), F32),
            jax.ShapeDtypeStruct((bsz, A_HEADS, A_HEAD_DK, A_HEAD_DV), F32),
        ],
        scratch_shapes=[
            pltpu.VMEM((tile, 2 * A_KEY_WIDTH), F32),
            pltpu.VMEM((tile, BRANCH_WIDTH), F32),
            pltpu.VMEM((tile, BRANCH_WIDTH), F32),
            pltpu.VMEM((tile, A_KEY_WIDTH), F32),
            pltpu.VMEM((tile, BRANCH_WIDTH), F32),
            pltpu.VMEM((tile, BRANCH_WIDTH), F32),
        ],
        compiler_params=pltpu.CompilerParams(
            dimension_semantics=("arbitrary", "arbitrary"), vmem_limit_bytes=VMEM_LIMIT_BYTES),
        name="gla_prompt",
    )(x, a_norm, wqk, wv, wg, wlow, wg2, bg, out_norm, wout)


def _swa_prompt_kernel(h_ref, kvn_ref, wkv_ref, kn_ref, bd_ref, cos_ref, sin_ref, bn_ref, wq_ref, wg_ref,
                       qn_ref, sink_ref, wout_ref,
                       y_ref, kc_ref, vc_ref,
                       ktop_s, kbot_s, vtop_s, vbot_s, qb_s, ub_s, kv_s, q_s, gsl_s, osl_s, og_s):
    tile = h_ref.shape[0]
    l = pl.program_id(1)
    blk = WINDOW
    kvw = B_KV_WIDTH
    pairs = B_GROUP // 2
    n_slabs = BRANCH_WIDTH // LANES
    n_iter = (tile // blk) * B_KV_HEADS
    gate_cols = wg_ref.shape[2]

    @pl.when(l == 0)
    def _():
        zeros = jnp.zeros((B_KV_HEADS, WINDOW, LANES), BF16)
        ktop_s[:, 0:WINDOW, :] = zeros
        kbot_s[:, 0:WINDOW, :] = zeros
        vtop_s[:, 0:WINDOW, 0:LANES] = zeros
        vbot_s[:, 0:WINDOW, 0:LANES] = zeros
        lane = lax.broadcasted_iota(jnp.int32, (B_KV_HEADS, WINDOW + tile, LANES), 2)
        vtop_s[:, :, LANES:] = jnp.where(lane < B_HEAD_DIM, 1.0, 0.0).astype(BF16)
        vbot_s[:, :, LANES:] = jnp.where(lane < B_HEAD_DIM, 0.0, 1.0).astype(BF16)

    h = h_ref[...]
    cos = cos_ref[...]
    sin = sin_ref[...]
    first_half, lo64 = _lane_masks(tile)
    bd = bd_ref[...]

    kv_s[...] = _dot(_rmsnorm(h, kvn_ref[...]).astype(BF16), wkv_ref[...])
    ub_s[...] = _rmsnorm(h, bn_ref[...]).astype(BF16)

    kn = _head_norm(kv_s[:, :kvw], bd, kn_ref[...])
    v = kv_s[:, kvw:]
    k_slabs = [_rope_slab(kn[:, s * LANES:(s + 1) * LANES], cos, sin, first_half) for s in range(kvw // LANES)]

    for s in range(kvw // LANES):
        kv_s[:, s * LANES:(s + 1) * LANES] = k_slabs[s]

    for g in range(B_KV_HEADS):
        s = g // 2
        top, bot = _split_group(k_slabs[s], g, lo64)
        ktop_s[g, WINDOW:WINDOW + tile, :] = top.astype(BF16)
        kbot_s[g, WINDOW:WINDOW + tile, :] = bot.astype(BF16)
        top, bot = _split_group(v[:, s * LANES:(s + 1) * LANES], g, lo64)
        vtop_s[g, WINDOW:WINDOW + tile, 0:LANES] = top.astype(BF16)
        vbot_s[g, WINDOW:WINDOW + tile, 0:LANES] = bot.astype(BF16)

    qscale = qn_ref[...] * (B_HEAD_DIM ** -0.5 * LOG2E)
    qcos = [cos * qscale[:, s * LANES:(s + 1) * LANES] for s in range(kvw // LANES)]
    qsin = [sin * jnp.where(first_half[0:1], pltpu.roll(qscale[:, s * LANES:(s + 1) * LANES], 96, 1),
                            pltpu.roll(qscale[:, s * LANES:(s + 1) * LANES], 32, 1)) for s in range(kvw // LANES)]

    q_s[...] = _dot(ub_s[...], wq_ref[...])
    for c in range(BRANCH_WIDTH // kvw):
        x = q_s[:, c * kvw:(c + 1) * kvw]
        hi, lo = _split2(x * x)
        rinv = lax.rsqrt((_dot(hi, bd) + _dot(lo, bd)) * (1.0 / B_HEAD_DIM) + RMS_EPS)
        for s in range(kvw // LANES):
            xs = x[:, s * LANES:(s + 1) * LANES]
            xr = jnp.where(first_half, pltpu.roll(xs, 96, 1), pltpu.roll(xs, 32, 1))
            qr = ((xs * qcos[s] + xr * qsin[s]) * rinv[:, s * LANES:(s + 1) * LANES]).astype(BF16)
            for i in range(tile // blk):
                qb_s[i, c * (kvw // LANES) + s] = qr[i * blk:(i + 1) * blk]

    rowi = lax.broadcasted_iota(jnp.int32, (blk, 2 * blk), 0)
    coli = lax.broadcasted_iota(jnp.int32, (blk, 2 * blk), 1)
    band = jnp.logical_and(coli - rowi >= 0, coli - rowi <= WINDOW)
    _, lo64_b = _lane_masks(blk)

    def attn_pair(idx):
        i = idx // B_KV_HEADS
        g = idx % B_KV_HEADS
        r0 = pl.multiple_of(i * blk, blk)
        first_col = jnp.where(l * tile + r0 > 0, 0, WINDOW)
        mask = jnp.logical_and(band, coli >= first_col)
        kpad = jnp.concatenate([ktop_s[g, pl.ds(r0, 2 * blk), :], kbot_s[g, pl.ds(r0, 2 * blk), :]], axis=0)
        vpad = jnp.concatenate([vtop_s[g, pl.ds(r0, 2 * blk), :], vbot_s[g, pl.ds(r0, 2 * blk), :]], axis=0)
        qg = qb_s[i, pl.ds(g * pairs, pairs)].reshape(pairs * blk, LANES)
        s = _dot_nt(qg, kpad)
        ps, esinks = [], []
        for j in range(pairs):
            pj, ej = [], []
            for half in range(2):
                sink = sink_ref[g * B_GROUP + 2 * j + half] * LOG2E
                sh = jnp.where(mask, s[j * blk:(j + 1) * blk, half * 2 * blk:(half + 1) * 2 * blk], -jnp.inf)
                m = jnp.maximum(jnp.max(sh, axis=-1, keepdims=True), sink)
                pj.append(jnp.exp2(sh - m).astype(BF16))
                ej.append(jnp.exp2(sink - m))
            ps.append(jnp.concatenate(pj, axis=1))
            esinks.append(jnp.where(lo64_b, ej[0], ej[1]))
        o = _dot(jnp.concatenate(ps, axis=0), vpad)
        for j in range(pairs):
            oj = o[j * blk:(j + 1) * blk]
            osl_s[g * pairs + j, pl.ds(r0, blk), :] = oj[:, :LANES] / (oj[:, LANES:] + esinks[j])

    def attn_body(k, carry):
        gc = _dot(ub_s[...], wg_ref[k])
        for s in range(gate_cols // LANES):
            gsl_s[k * (gate_cols // LANES) + s] = gc[:, s * LANES:(s + 1) * LANES]
        attn_pair(2 * k)
        attn_pair(2 * k + 1)
        return carry

    lax.fori_loop(0, n_iter // 2, attn_body, 0, unroll=4)

    for buf in (ktop_s, kbot_s):
        buf[:, 0:WINDOW, :] = buf[:, tile:tile + WINDOW, :]
    for buf in (vtop_s, vbot_s):
        buf[:, 0:WINDOW, 0:LANES] = buf[:, tile:tile + WINDOW, 0:LANES]

    y = h
    per = n_slabs // OUT_PROJ_CHUNKS
    for c in range(OUT_PROJ_CHUNKS):
        for sl in range(c * per, (c + 1) * per):
            og_s[:, sl * LANES:(sl + 1) * LANES] = (osl_s[sl] * _silu(gsl_s[sl])).astype(BF16)
        cols = slice(c * per * LANES, (c + 1) * per * LANES)
        y = y + _dot(og_s[:, cols], wout_ref[cols, :])
    y_ref[...] = y

    @pl.when(l == pl.num_programs(1) - 1)
    def _():
        kc_ref[...] = kv_s[tile - WINDOW:, :kvw]
        vc_ref[...] = kv_s[tile - WINDOW:, kvw:]


def _swa_prompt(h, kv_norm, wkv, k_norm, bd, cos, sin, b_norm, wq, wg, q_norm, sinks, wout):
    bsz, seq, d = h.shape
    tile = min(PROMPT_TILE, seq)
    assert seq % tile == 0 and tile % WINDOW == 0 and seq >= WINDOW
    n_iter = (tile // WINDOW) * B_KV_HEADS
    gate_cols = BRANCH_WIDTH // (n_iter // 2)
    assert gate_cols % LANES == 0
    wg = wg.reshape(d, n_iter // 2, gate_cols).transpose(1, 0, 2)
    const = lambda shape: pl.BlockSpec(shape, lambda b, l: (0,) * len(shape), pipeline_mode=pl.Buffered(1))
    k_scratch = pltpu.VMEM((B_KV_HEADS, WINDOW + tile, LANES), BF16)
    v_scratch = pltpu.VMEM((B_KV_HEADS, WINDOW + tile, 2 * LANES), BF16)
    n_slabs = BRANCH_WIDTH // LANES
    return pl.pallas_call(
        _swa_prompt_kernel,
        grid=(bsz, seq // tile),
        in_specs=[
            pl.BlockSpec((None, tile, d), lambda b, l: (b, l, 0)),
            const(kv_norm.shape), const(wkv.shape), const(k_norm.shape), const(bd.shape),
            pl.BlockSpec((tile, LANES), lambda b, l: (l, 0)),
            pl.BlockSpec((tile, LANES), lambda b, l: (l, 0)),
            const(b_norm.shape), const(wq.shape), const(wg.shape), const(q_norm.shape),
            pl.BlockSpec(memory_space=pltpu.SMEM),
            const(wout.shape),
        ],
        out_specs=[
            pl.BlockSpec((None, tile, d), lambda b, l: (b, l, 0)),
            pl.BlockSpec((None, WINDOW, B_KV_WIDTH), lambda b, l: (b, 0, 0)),
            pl.BlockSpec((None, WINDOW, B_KV_WIDTH), lambda b, l: (b, 0, 0)),
        ],
        out_shape=[
            jax.ShapeDtypeStruct((bsz, seq, d), F32),
            jax.ShapeDtypeStruct((bsz, WINDOW, B_KV_WIDTH), F32),
            jax.ShapeDtypeStruct((bsz, WINDOW, B_KV_WIDTH), F32),
        ],
        scratch_shapes=[
            k_scratch, k_scratch, v_scratch, v_scratch,
            pltpu.VMEM((tile // WINDOW, n_slabs, WINDOW, LANES), BF16),
            pltpu.VMEM((tile, d), BF16),
            pltpu.VMEM((tile, 2 * B_KV_WIDTH), F32),
            pltpu.VMEM((tile, BRANCH_WIDTH), F32),
            pltpu.VMEM((n_slabs, tile, LANES), F32),
            pltpu.VMEM((n_slabs, tile, LANES), F32),
            pltpu.VMEM((tile, BRANCH_WIDTH), BF16),
        ],
        compiler_params=pltpu.CompilerParams(
            dimension_semantics=("arbitrary", "arbitrary"), vmem_limit_bytes=VMEM_LIMIT_BYTES),
        name="swa_prompt",
    )(h, kv_norm, wkv, k_norm, bd, cos, sin, b_norm, wq, wg, q_norm, sinks, wout)


def _gla_proj_sample_kernel(x_ref, g_ref, wqk_ref, wv_ref, wg_ref, wlow_ref, qk_ref, v_ref, gate_ref, glow_ref):
    u = _rmsnorm(x_ref[...], g_ref[...]).astype(BF16)
    qk_ref[...] = _dot(u, wqk_ref[...])
    v_ref[...] = _dot(u, wv_ref[...])
    gate_ref[...] = _dot(u, wg_ref[...])
    glow_ref[...] = _dot(u, wlow_ref[...])


def _gla_proj_sample(x, g, wqk, wv, wg, wlow):
    m = x.shape[0]
    return pl.pallas_call(
        _gla_proj_sample_kernel,
        out_shape=[jax.ShapeDtypeStruct((m, w.shape[1]), F32) for w in (wqk, wv, wg, wlow)],
        compiler_params=pltpu.CompilerParams(vmem_limit_bytes=VMEM_LIMIT_BYTES),
        name="gla_proj_sample",
    )(x, g, wqk, wv, wg, wlow)


def _gla_out_kernel(o_ref, gate_ref, x_ref, on_ref, w_ref, y_ref, og_s):
    for h in range(A_HEADS):
        vs = slice(h * A_HEAD_DV, (h + 1) * A_HEAD_DV)
        o = _rmsnorm(o_ref[:, vs], on_ref[...])
        og_s[:, vs] = (o * _silu(gate_ref[:, vs])).astype(BF16)
    y_ref[...] = _dot(og_s[...], w_ref[...]) + x_ref[...]


def _gla_out(o, gate, x, out_norm, wout):
    m = o.shape[0]
    return pl.pallas_call(
        _gla_out_kernel,
        out_shape=jax.ShapeDtypeStruct(x.shape, F32),
        scratch_shapes=[pltpu.VMEM((m, BRANCH_WIDTH), BF16)],
        compiler_params=pltpu.CompilerParams(vmem_limit_bytes=VMEM_LIMIT_BYTES),
        name="gla_out_sample",
    )(o, gate, x, out_norm, wout)


def _out_proj_kernel(og_ref, x_ref, w_ref, y_ref):
    y_ref[...] = _dot(og_ref[...].astype(BF16), w_ref[...]) + x_ref[...]


def _out_proj(og, x, wout):
    return pl.pallas_call(
        _out_proj_kernel,
        out_shape=jax.ShapeDtypeStruct(x.shape, F32),
        compiler_params=pltpu.CompilerParams(vmem_limit_bytes=VMEM_LIMIT_BYTES),
        name="swa_out_sample",
    )(og, x, wout)


def _gla_sample_kernel(qk_ref, v_ref, glow_ref, wg2_ref, bg_ref, st_ref, o_ref, nst_ref, *, seq):
    nb = st_ref.shape[0]
    dk, dv, kw = A_HEAD_DK, A_HEAD_DV, A_KEY_WIDTH
    scale = A_HEAD_DK ** -0.5
    per_tile = SUBLANES // seq

    def one_seq(i, qk, v, loga):
        q = qk[:, 0:kw] * scale
        k = qk[:, kw:2 * kw]
        bs = [loga[0:1]]
        for t in range(1, seq):
            bs.append(bs[-1] + loga[t:t + 1])
        bmat = jnp.concatenate(bs, axis=0)
        blast = bs[-1]
        qt = q * jnp.exp(bmat)
        kd = k * jnp.exp(blast - bmat)
        eblast = jnp.exp(blast)

        outs = [[jnp.zeros((1, dv), F32) for _ in range(A_HEADS)] for _ in range(seq)]
        for t in range(seq):
            for s in range(t + 1):
                w = q[t:t + 1] * k[s:s + 1] * jnp.exp(bs[t] - bs[s])
                for h in range(A_HEADS):
                    a = jnp.sum(w[:, h * dk:(h + 1) * dk], axis=-1, keepdims=True)
                    outs[t][h] = outs[t][h] + a * v[s:s + 1, h * dv:(h + 1) * dv]
        intra = jnp.concatenate([jnp.concatenate(outs[t], axis=1) for t in range(seq)], axis=0)

        qt8 = jnp.concatenate([qt, jnp.zeros((SUBLANES - seq, kw), F32)], axis=0).astype(BF16)
        inter = []
        for h in range(A_HEADS):
            ks = slice(h * dk, (h + 1) * dk)
            state = st_ref[i, h]
            inter.append(_dot(qt8[:, ks], state.astype(BF16))[0:seq])
            m = jnp.concatenate([kd[:, ks], eblast[:, ks], jnp.zeros((dk - seq - 1, dk), F32)], axis=0)
            mt = jnp.transpose(m)
            vpad = jnp.concatenate([v[:, h * dv:(h + 1) * dv], jnp.zeros((dk - seq, dv), F32)], axis=0)
            nst_ref[i, h] = state * mt[:, seq:seq + 1] + _dot(mt.astype(BF16), vpad.astype(BF16))
        return intra + jnp.concatenate(inter, axis=1)

    def per_tile_body(j, carry):
        rows = pl.ds(pl.multiple_of(j * SUBLANES, SUBLANES), SUBLANES)
        qk8, v8 = qk_ref[rows, :], v_ref[rows, :]
        pre = _dot(glow_ref[rows, :].astype(BF16), wg2_ref[...]) + bg_ref[...]
        loga8 = _log_sigmoid(pre) * (1.0 / A_GATE_NORMALIZER)
        outs = []
        for p in range(per_tile):
            r = slice(p * seq, (p + 1) * seq)
            outs.append(one_seq(j * per_tile + p, qk8[r], v8[r], loga8[r]))
        o_ref[rows, :] = jnp.concatenate(outs, axis=0)
        return carry

    lax.fori_loop(0, nb // per_tile, per_tile_body, 0)


def _gla_sample(qk, v, glow, wg2, bg, state, seq):
    nbatch = state.shape[0]
    nb = min(SAMPLE_BATCH_BLOCK, nbatch)
    assert nbatch % nb == 0 and SUBLANES % seq == 0 and (nb * seq) % SUBLANES == 0
    st_spec = pl.BlockSpec((nb, A_HEADS, A_HEAD_DK, A_HEAD_DV), lambda b: (b, 0, 0, 0))
    tok_spec = lambda a: pl.BlockSpec((nb * seq, a.shape[1]), lambda b: (b, 0))
    return pl.pallas_call(
        functools.partial(_gla_sample_kernel, seq=seq),
        grid=(nbatch // nb,),
        in_specs=[
            tok_spec(qk), tok_spec(v), tok_spec(glow),
            pl.BlockSpec(wg2.shape, lambda b: (0, 0)),
            pl.BlockSpec(bg.shape, lambda b: (0, 0)),
            st_spec,
        ],
        out_specs=[pl.BlockSpec((nb * seq, BRANCH_WIDTH), lambda b: (b, 0)), st_spec],
        out_shape=[
            jax.ShapeDtypeStruct((nbatch * seq, BRANCH_WIDTH), F32),
            jax.ShapeDtypeStruct(state.shape, F32),
        ],
        compiler_params=pltpu.CompilerParams(
            dimension_semantics=("arbitrary",), vmem_limit_bytes=VMEM_LIMIT_BYTES),
        name="gla_sample",
    )(qk, v, glow, wg2, bg, state)


def _swa_pre_sample_kernel(h_ref, kvn_ref, wkv_ref, kn_ref, bd_ref, cos_ref, sin_ref, bn_ref, wq_ref, wg_ref, qn_ref,
                           k_ref, v_ref, q_ref, gate_ref):
    m = h_ref.shape[0]
    kvw = B_KV_WIDTH
    h = h_ref[...]
    cos = cos_ref[...]
    sin = sin_ref[...]
    first_half, _ = _lane_masks(m)
    bd = bd_ref[...]

    kv = _dot(_rmsnorm(h, kvn_ref[...]).astype(BF16), wkv_ref[...])
    kn = _head_norm(kv[:, :kvw], bd, kn_ref[...])
    for s in range(kvw // LANES):
        k_ref[:, s * LANES:(s + 1) * LANES] = _rope_slab(kn[:, s * LANES:(s + 1) * LANES], cos, sin, first_half)
    v_ref[...] = kv[:, kvw:]

    ub = _rmsnorm(h, bn_ref[...]).astype(BF16)
    _, lo64 = _lane_masks(m)
    qgain = qn_ref[...] * (B_HEAD_DIM ** -0.5 * LOG2E)
    gate = _dot(ub, wg_ref[...])
    q_slabs, g_slabs = [], []
    for c in range(BRANCH_WIDTH // kvw):
        qc = _head_norm(_dot(ub, wq_ref[:, c * kvw:(c + 1) * kvw]), bd, qgain)
        for s in range(kvw // LANES):
            q_slabs.append(_rope_slab(qc[:, s * LANES:(s + 1) * LANES], cos, sin, first_half))
            g_slabs.append(gate[:, c * kvw + s * LANES:c * kvw + (s + 1) * LANES])
    per_group = B_GROUP // 2
    for r in range(B_GROUP):
        for p in range(B_KV_HEADS // 2):
            a = 2 * p * per_group + r // 2
            b = (2 * p + 1) * per_group + r // 2
            lanes = slice((2 * r + p) * LANES, (2 * r + p + 1) * LANES)
            q_ref[:, lanes] = _regroup_slab(q_slabs[a], q_slabs[b], r % 2, lo64)
            gate_ref[:, lanes] = _regroup_slab(g_slabs[a], g_slabs[b], r % 2, lo64)


def _swa_pre_sample(h, kv_norm, wkv, k_norm, bd, cos, sin, b_norm, wq, wg, q_norm):
    m = h.shape[0]
    return pl.pallas_call(
        _swa_pre_sample_kernel,
        out_shape=[
            jax.ShapeDtypeStruct((m, B_KV_WIDTH), F32),
            jax.ShapeDtypeStruct((m, B_KV_WIDTH), F32),
            jax.ShapeDtypeStruct((m, BRANCH_WIDTH), F32),
            jax.ShapeDtypeStruct((m, BRANCH_WIDTH), F32),
        ],
        compiler_params=pltpu.CompilerParams(vmem_limit_bytes=VMEM_LIMIT_BYTES),
        name="swa_pre_sample",
    )(h, kv_norm, wkv, k_norm, bd, cos, sin, b_norm, wq, wg, q_norm)


def _swa_sample_kernel(q_ref, gate_ref, kn_ref, vn_ref, ck_ref, cv_ref, sink_ref, og_ref, nk_ref, nv_ref, *, seq):
    nb = ck_ref.shape[0]
    per_tile = SUBLANES // seq
    kvw = B_KV_WIDTH
    trows = SUBLANES
    nrows = B_GROUP * B_KV_HEADS * trows
    rowi = lax.broadcasted_iota(jnp.int32, (nrows, 2 * WINDOW), 0)
    coli = lax.broadcasted_iota(jnp.int32, (nrows, 2 * WINDOW), 1)
    dpos = coli - rowi % trows
    mask = jnp.logical_and(dpos >= 0, dpos <= WINDOW)
    lane8 = lax.broadcasted_iota(jnp.int32, (trows, kvw), 1) // B_HEAD_DIM
    grow = (lax.broadcasted_iota(jnp.int32, (nrows, kvw), 0) // trows) % B_KV_HEADS
    gkeep = grow == lax.broadcasted_iota(jnp.int32, (nrows, kvw), 1) // B_HEAD_DIM
    sink = sink_ref[...][:, 0:1] * LOG2E
    zq = jnp.zeros((trows - seq, BRANCH_WIDTH), F32)
    lo64_q = lax.broadcasted_iota(jnp.int32, (seq, LANES), 1) < B_HEAD_DIM
    zk = jnp.zeros((WINDOW - seq, kvw), F32)

    def one_seq(i, q4, gate4, k_new, v_new):
        ck = ck_ref[i]
        cv = cv_ref[i]
        nk_ref[i, 0:WINDOW - seq, :] = ck[seq:, :]
        nk_ref[i, WINDOW - seq:WINDOW, :] = k_new
        nv_ref[i, 0:WINDOW - seq, :] = cv[seq:, :]
        nv_ref[i, WINDOW - seq:WINDOW, :] = v_new

        q8 = jnp.concatenate([q4, zq], axis=0)
        pieces = []
        for r in range(B_GROUP):
            slab = q8[:, r * kvw:(r + 1) * kvw]
            for g in range(B_KV_HEADS):
                pieces.append(jnp.where(lane8 == g, slab, 0.0))
        qrows = jnp.concatenate(pieces, axis=0).astype(BF16)
        kpad = jnp.concatenate([k_new, zk], axis=0).astype(BF16)
        vpad = jnp.concatenate([v_new, zk], axis=0).astype(BF16)
        s = jnp.concatenate([_dot_nt(qrows, ck.astype(BF16)), _dot_nt(qrows, kpad)], axis=1)
        s = jnp.where(mask, s, -jnp.inf)
        m = jnp.maximum(jnp.max(s, axis=-1, keepdims=True), sink)
        p = jnp.exp2(s - m)
        den = jnp.sum(p, axis=-1, keepdims=True) + jnp.exp2(sink - m)
        pb = p.astype(BF16)
        o = _dot(pb[:, :WINDOW], cv.astype(BF16)) + _dot(pb[:, WINDOW:], vpad)
        o = jnp.where(gkeep, o / den, 0.0)
        outs = []
        for r in range(B_GROUP):
            acc = o[r * B_KV_HEADS * trows:r * B_KV_HEADS * trows + trows]
            for g in range(1, B_KV_HEADS):
                acc = acc + o[(r * B_KV_HEADS + g) * trows:(r * B_KV_HEADS + g + 1) * trows]
            outs.append(acc)
        gated = jnp.concatenate(outs, axis=1)[0:seq] * _silu(gate4)
        slabs = []
        for g in range(B_KV_HEADS):
            for j in range(B_GROUP // 2):
                a = gated[:, (4 * j + g // 2) * LANES:(4 * j + g // 2 + 1) * LANES]
                b = gated[:, (4 * j + 2 + g // 2) * LANES:(4 * j + 2 + g // 2 + 1) * LANES]
                slabs.append(_regroup_slab(a, b, g % 2, lo64_q))
        return jnp.concatenate(slabs, axis=1)

    def per_tile_body(j, carry):
        rows = pl.ds(pl.multiple_of(j * SUBLANES, SUBLANES), SUBLANES)
        q8, gate8, kn8, vn8 = q_ref[rows, :], gate_ref[rows, :], kn_ref[rows, :], vn_ref[rows, :]
        outs = []
        for p in range(per_tile):
            r = slice(p * seq, (p + 1) * seq)
            outs.append(one_seq(j * per_tile + p, q8[r], gate8[r], kn8[r], vn8[r]))
        og_ref[rows, :] = jnp.concatenate(outs, axis=0)
        return carry

    lax.fori_loop(0, nb // per_tile, per_tile_body, 0, unroll=4)


def _swa_sample(q, gate, k_new, v_new, ck, cv, sink_rows, seq):
    nbatch = ck.shape[0]
    nb = min(SAMPLE_BATCH_BLOCK, nbatch)
    assert nbatch % nb == 0 and ck.shape[1] == WINDOW and SUBLANES % seq == 0 and (nb * seq) % SUBLANES == 0
    cache_spec = pl.BlockSpec((nb, WINDOW, B_KV_WIDTH), lambda b: (b, 0, 0))
    wide_spec = pl.BlockSpec((nb * seq, BRANCH_WIDTH), lambda b: (b, 0))
    new_spec = pl.BlockSpec((nb * seq, B_KV_WIDTH), lambda b: (b, 0))
    return pl.pallas_call(
        functools.partial(_swa_sample_kernel, seq=seq),
        grid=(nbatch // nb,),
        in_specs=[wide_spec, wide_spec, new_spec, new_spec, cache_spec, cache_spec,
                  pl.BlockSpec(sink_rows.shape, lambda b: (0, 0))],
        out_specs=[wide_spec, cache_spec, cache_spec],
        out_shape=[
            jax.ShapeDtypeStruct((nbatch * seq, BRANCH_WIDTH), F32),
            jax.ShapeDtypeStruct(ck.shape, F32),
            jax.ShapeDtypeStruct(cv.shape, F32),
        ],
        compiler_params=pltpu.CompilerParams(
            dimension_semantics=("arbitrary",), vmem_limit_bytes=VMEM_LIMIT_BYTES),
        name="swa_sample",
    )(q, gate, k_new, v_new, ck, cv, sink_rows)


def _rope_tables(first_pos, n_pos, repeat=1):
    half = B_HEAD_DIM // 2
    inv_freq = ROPE_THETA ** (-np.arange(half, dtype=np.float64) / half)
    ang = (first_pos + np.arange(n_pos, dtype=np.float64))[:, None] * inv_freq[None, :]
    cos, sin = np.cos(ang), np.sin(ang)
    cos_t = np.tile(np.concatenate([cos, cos, cos, cos], axis=1), (repeat, 1))
    sin_t = np.tile(np.concatenate([-sin, sin, -sin, sin], axis=1), (repeat, 1))
    return jnp.asarray(cos_t, F32), jnp.asarray(sin_t, F32)


def kernel(x_prompt, x_sample, state_gla, cache_swa_k, cache_swa_v, a_norm, a_w_in, a_w_gate2, a_b_gate,
           a_out_norm, a_w_out, kv_norm, w_k, w_v, k_norm, b_norm, b_w_in, b_q_norm, b_sinks, b_w_out):
    assert a_norm.shape[0] == 1 and b_norm.shape[0] == 1
    bsz_p, seq_p, d = x_prompt.shape
    bsz_s, seq_s, _ = x_sample.shape
    kw, bw = A_KEY_WIDTH, BRANCH_WIDTH

    w_in = a_w_in[0]
    wa_qk = w_in[:, :2 * kw].astype(BF16)
    wa_v = w_in[:, 2 * kw:2 * kw + bw].astype(BF16)
    wa_g = w_in[:, 2 * kw + bw:2 * kw + 2 * bw].astype(BF16)
    wa_low = jnp.pad(w_in[:, 2 * kw + 2 * bw:].astype(BF16), ((0, 0), (0, LANES - A_GATE_RANK)))
    wa_g2 = jnp.pad(a_w_gate2[0].astype(BF16), ((0, LANES - A_GATE_RANK), (0, 0)))
    a_bg = a_b_gate[0][None, :]
    a_n = a_norm[0][None, :]
    a_on = a_out_norm[0][None, :]
    wa_out = a_w_out[0].astype(BF16)
    w_kv = jnp.concatenate([w_k.astype(BF16), w_v.astype(BF16)], axis=1)
    wb_q = b_w_in[0][:, :bw].astype(BF16)
    wb_g = b_w_in[0][:, bw:].astype(BF16)
    wb_out = b_w_out[0].astype(BF16)
    kvn = kv_norm[None, :]
    bn = b_norm[0][None, :]
    kn_t = jnp.tile(k_norm, B_KV_WIDTH // B_HEAD_DIM)[None, :]
    qn_t = jnp.tile(b_q_norm[0], B_KV_WIDTH // B_HEAD_DIM)[None, :]
    sinks = b_sinks[0]
    grp = np.arange(B_KV_WIDTH) // B_HEAD_DIM
    bd = jnp.asarray(grp[:, None] == grp[None, :], BF16)

    cos_p, sin_p = _rope_tables(0, seq_p)
    h1_p, st_p = _gla_prompt(x_prompt, a_n, wa_qk, wa_v, wa_g, wa_low, wa_g2, a_bg, a_on, wa_out)
    y_p, kc_p, vc_p = _swa_prompt(h1_p, kvn, w_kv, kn_t, bd, cos_p, sin_p, bn, wb_q, wb_g, qn_t, sinks, wb_out)

    m = bsz_s * seq_s
    xs = x_sample.reshape(m, d)
    qk_a, v_a, gate_a, glow_a = _gla_proj_sample(xs, a_n, wa_qk, wa_v, wa_g, wa_low)
    o_s, st_s = _gla_sample(qk_a, v_a, glow_a, wa_g2, a_bg, state_gla[0], seq_s)
    h1_s = _gla_out(o_s, gate_a, xs, a_on, wa_out)

    sink_rows = jnp.broadcast_to(
        sinks.reshape(B_KV_HEADS, B_GROUP).T[:, :, None, None], (B_GROUP, B_KV_HEADS, SUBLANES, LANES)
    ).reshape(B_GROUP * B_KV_HEADS * SUBLANES, LANES)
    cos_s, sin_s = _rope_tables(PAST_LEN, seq_s, repeat=bsz_s)
    k_new, v_new, q_s, gate_b = _swa_pre_sample(h1_s, kvn, w_kv, kn_t, bd, cos_s, sin_s, bn, wb_q, wb_g, qn_t)
    og_s, nk_s, nv_s = _swa_sample(
        q_s, gate_b, k_new, v_new,
        cache_swa_k.reshape(bsz_s, WINDOW, B_KV_WIDTH), cache_swa_v.reshape(bsz_s, WINDOW, B_KV_WIDTH), sink_rows, seq_s)
    y_s = _out_proj(og_s, h1_s, wb_out)

    return (y_p, y_s.reshape(bsz_s, seq_s, d),
            st_p[None], st_s[None],
            kc_p.reshape(bsz_p, WINDOW, B_KV_HEADS, B_HEAD_DIM), vc_p.reshape(bsz_p, WINDOW, B_KV_HEADS, B_HEAD_DIM),
            nk_s.reshape(bsz_s, WINDOW, B_KV_HEADS, B_HEAD_DIM), nv_s.reshape(bsz_s, WINDOW, B_KV_HEADS, B_HEAD_DIM))
```

```python
import functools

import jax
import jax.numpy as jnp
import numpy as np
from jax import lax
from jax.experimental import pallas as pl
from jax.experimental.pallas import tpu as pltpu

F32 = jnp.float32
BF16 = jnp.bfloat16

A_HEADS = 4
A_HEAD_DK = 128
A_HEAD_DV = 512
A_KEY_WIDTH = A_HEADS * A_HEAD_DK
BRANCH_WIDTH = A_HEADS * A_HEAD_DV
A_GATE_RANK = 16
A_GATE_NORMALIZER = 16.0
B_HEAD_DIM = 64
B_HEADS = BRANCH_WIDTH // B_HEAD_DIM
B_KV_HEADS = 4
B_GROUP = B_HEADS // B_KV_HEADS
B_KV_WIDTH = B_KV_HEADS * B_HEAD_DIM
WINDOW = 128
ROPE_THETA = 10000.0
RMS_EPS = 1e-6
PAST_LEN = 16384

LANES = 128
SUBLANES = 8
VMEM_LIMIT_BYTES = 60 * 1024 * 1024

PROMPT_TILE = 512
GLA_CHUNK = 256
GLA_SAFE_DECAY = 80.0
SAMPLE_BATCH_BLOCK = 8
OUT_PROJ_CHUNKS = 4
LOG2E = 1.4426950408889634


def _dot(a, b):
    return jnp.dot(a, b, preferred_element_type=F32)


def _dot_nt(a, b):
    return lax.dot_general(a, b, (((1,), (1,)), ((), ())), preferred_element_type=F32)


def _split2(x):
    hi = x.astype(BF16)
    lo = (x - hi.astype(F32)).astype(BF16)
    return hi, lo


def _rmsnorm(x, g):
    ms = jnp.mean(x * x, axis=-1, keepdims=True)
    return x * lax.rsqrt(ms + RMS_EPS) * g


def _log_sigmoid(x):
    return jnp.minimum(x, 0.0) - jnp.log1p(jnp.exp(-jnp.abs(x)))


def _silu(x):
    return x * (1.0 / (1.0 + jnp.exp(-x)))


def _head_norm(x, bd, gain):
    hi, lo = _split2(x * x)
    ss = _dot(hi, bd) + _dot(lo, bd)
    return x * lax.rsqrt(ss * (1.0 / B_HEAD_DIM) + RMS_EPS) * gain


def _rope_slab(x, cos, sin_signed, first_half):
    xr = jnp.where(first_half, pltpu.roll(x, 96, 1), pltpu.roll(x, 32, 1))
    return x * cos + xr * sin_signed


def _lane_masks(rows):
    lane = lax.broadcasted_iota(jnp.int32, (rows, LANES), 1)
    first_half = (lane % B_HEAD_DIM) < (B_HEAD_DIM // 2)
    lo64 = lane < B_HEAD_DIM
    return first_half, lo64


def _split_group(slab, g, lo64):
    swapped = pltpu.roll(slab, B_HEAD_DIM, 1)
    zero = jnp.zeros_like(slab)
    if g % 2 == 0:
        return jnp.where(lo64, slab, zero), jnp.where(lo64, zero, swapped)
    return jnp.where(lo64, swapped, zero), jnp.where(lo64, zero, slab)


def _regroup_slab(a, b, odd, lo64):
    if odd:
        return jnp.where(lo64, pltpu.roll(a, B_HEAD_DIM, 1), b)
    return jnp.where(lo64, a, pltpu.roll(b, B_HEAD_DIM, 1))


def _gla_prompt_kernel(x_ref, an_ref, wqk_ref, wv_ref, wg_ref, wlow_ref, wg2_ref, bg_ref, on_ref, wout_ref,
                       h1_ref, st_ref,
                       qk_s, v_s, gate_s, b_s, o_s, oi_s):
    tile = x_ref.shape[0]
    chunk = GLA_CHUNK
    n_chunks = tile // chunk
    dk, dv, kw = A_HEAD_DK, A_HEAD_DV, A_KEY_WIDTH

    @pl.when(pl.program_id(1) == 0)
    def _():
        st_ref[...] = jnp.zeros_like(st_ref)

    x = x_ref[...]
    u = _rmsnorm(x, an_ref[...]).astype(BF16)
    qk_s[...] = _dot(u, wqk_ref[...])
    v_s[...] = _dot(u, wv_ref[...])
    gate_s[...] = _dot(u, wg_ref[...])
    glow = _dot(u, wlow_ref[...]).astype(BF16)
    pre = _dot(glow, wg2_ref[...]) + bg_ref[...]
    loga = _log_sigmoid(pre) * (1.0 / A_GATE_NORMALIZER)

    row = lax.broadcasted_iota(jnp.int32, (chunk, chunk), 0)
    col = lax.broadcasted_iota(jnp.int32, (chunk, chunk), 1)
    lower = row >= col
    tri = jnp.where(lower, 1.0, 0.0).astype(BF16)
    for c in range(n_chunks):
        hi, lo = _split2(loga[c * chunk:(c + 1) * chunk])
        b_s[c * chunk:(c + 1) * chunk, :] = _dot(tri, hi) + _dot(tri, lo)
    safe = jnp.min(b_s[...]) >= -GLA_SAFE_DECAY

    scale = A_HEAD_DK ** -0.5
    for c in range(n_chunks):
        rows = slice(c * chunk, (c + 1) * chunk)
        b = b_s[rows, :]
        eb = jnp.exp(b)
        enb = jnp.exp(-b)
        blast = b[chunk - 1:chunk, :]
        ekl = jnp.exp(blast - b)
        eblast = jnp.exp(blast)
        for h in range(A_HEADS):
            ks = slice(h * dk, (h + 1) * dk)
            vs = slice(h * dv, (h + 1) * dv)
            q = qk_s[rows, ks] * scale
            k = qk_s[rows, kw + h * dk: kw + (h + 1) * dk]
            vh = v_s[rows, vs].astype(BF16)
            state = st_ref[h]
            qt = (q * eb[:, ks]).astype(BF16)
            o_inter = _dot(qt, state.astype(BF16))
            a = _dot_nt(qt, (k * enb[:, ks]).astype(BF16))
            a = jnp.where(lower, a, 0.0).astype(BF16)
            oi_s[rows, vs] = o_inter
            o_s[rows, vs] = o_inter + _dot(a, vh)
            kd_t = jnp.transpose(k * ekl[:, ks]).astype(BF16)
            dec = jnp.transpose(jnp.broadcast_to(eblast[:, ks], (dk, dk)))
            st_ref[h] = state * jnp.concatenate([dec] * (dv // dk), axis=1) + _dot(kd_t, vh)

    def finish():
        y = x
        for h in range(A_HEADS):
            vs = slice(h * dv, (h + 1) * dv)
            o = _rmsnorm(o_s[:, vs], on_ref[...])
            og = (o * _silu(gate_s[:, vs])).astype(BF16)
            y = y + _dot(og, wout_ref[vs, :])
        h1_ref[...] = y

    finish()

    @pl.when(jnp.logical_not(safe))
    def _():
        trow = lax.broadcasted_iota(jnp.int32, (chunk, dk), 0)
        sub8 = lax.broadcasted_iota(jnp.int32, (SUBLANES, 1), 0)
        for c in range(n_chunks):
            rows = slice(c * chunk, (c + 1) * chunk)
            for h in range(A_HEADS):
                ks = slice(h * dk, (h + 1) * dk)
                vs = slice(h * dv, (h + 1) * dv)
                q = qk_s[rows, ks] * scale
                b = b_s[rows, ks]

                def body(s, acc, c=c, h=h, q=q, b=b):
                    r8 = pl.multiple_of(c * chunk + (s // SUBLANES) * SUBLANES, SUBLANES)
                    pick = sub8 == s % SUBLANES

                    def row_of(ref, lanes):
                        return jnp.sum(jnp.where(pick, ref[pl.ds(r8, SUBLANES), lanes], 0.0), axis=0, keepdims=True)

                    brow = row_of(b_s, slice(h * dk, (h + 1) * dk))
                    krow = row_of(qk_s, slice(kw + h * dk, kw + (h + 1) * dk))
                    vrow = row_of(v_s, slice(h * dv, (h + 1) * dv))
                    w = jnp.where(trow >= s, jnp.exp(jnp.minimum(b - brow, 0.0)), 0.0)
                    colv = jnp.sum(q * krow * w, axis=-1, keepdims=True)
                    return acc + colv * vrow

                o_s[rows, vs] = oi_s[rows, vs] + lax.fori_loop(0, chunk, body, jnp.zeros((chunk, dv), F32))
        finish()


def _gla_prompt(x, a_norm, wqk, wv, wg, wlow, wg2, bg, out_norm, wout):
    bsz, seq, d = x.shape
    tile = min(PROMPT_TILE, seq)
    assert seq % tile == 0 and tile % GLA_CHUNK == 0
    const = lambda shape: pl.BlockSpec(shape, lambda b, l: (0,) * len(shape), pipeline_mode=pl.Buffered(1))
    return pl.pallas_call(
        _gla_prompt_kernel,
        grid=(bsz, seq // tile),
        in_specs=[
            pl.BlockSpec((None, tile, d), lambda b, l: (b, l, 0)),
            const(a_norm.shape), const(wqk.shape), const(wv.shape), const(wg.shape), const(wlow.shape),
            const(wg2.shape), const(bg.shape), const(out_norm.shape), const(wout.shape),
        ],
        out_specs=[
            pl.BlockSpec((None, tile, d), lambda b, l: (b, l, 0)),
            pl.BlockSpec((None, A_HEADS, A_HEAD_DK, A_HEAD_DV), lambda b, l: (b, 0, 0, 0)),
        ],
        out_shape=[
            jax.ShapeDtypeStruct((bsz, seq, d), F32),
            jax.ShapeDtypeStruct((bsz, A_HEADS, A_HEAD_DK, A_HEAD_DV), F32),
        ],
        scratch_shapes=[
            pltpu.VMEM((tile, 2 * A_KEY_WIDTH), F32),
            pltpu.VMEM((tile, BRANCH_WIDTH), F32),
            pltpu.VMEM((tile, BRANCH_WIDTH), F32),
            pltpu.VMEM((tile, A_KEY_WIDTH), F32),
            pltpu.VMEM((tile, BRANCH_WIDTH), F32),
            pltpu.VMEM((tile, BRANCH_WIDTH), F32),
        ],
        compiler_params=pltpu.CompilerParams(
            dimension_semantics=("arbitrary", "arbitrary"), vmem_limit_bytes=VMEM_LIMIT_BYTES),
        name="gla_prompt",
    )(x, a_norm, wqk, wv, wg, wlow, wg2, bg, out_norm, wout)


def _swa_prompt_kernel(h_ref, kvn_ref, wkv_ref, kn_ref, bd_ref, cos_ref, sin_ref, bn_ref, wq_ref, wg_ref,
                       qn_ref, sink_ref, wout_ref,
                       y_ref, kc_ref, vc_ref,
                       ktop_s, kbot_s, vtop_s, vbot_s, qb_s, ub_s, kv_s, q_s, gsl_s, osl_s, og_s):
    tile = h_ref.shape[0]
    l = pl.program_id(1)
    blk = WINDOW
    kvw = B_KV_WIDTH
    pairs = B_GROUP // 2
    n_slabs = BRANCH_WIDTH // LANES
    n_iter = (tile // blk) * B_KV_HEADS
    gate_cols = wg_ref.shape[2]

    @pl.when(l == 0)
    def _():
        zeros = jnp.zeros((B_KV_HEADS, WINDOW, LANES), BF16)
        ktop_s[:, 0:WINDOW, :] = zeros
        kbot_s[:, 0:WINDOW, :] = zeros
        vtop_s[:, 0:WINDOW, :] = zeros
        vbot_s[:, 0:WINDOW, :] = zeros

    h = h_ref[...]
    cos = cos_ref[...]
    sin = sin_ref[...]
    first_half, lo64 = _lane_masks(tile)
    bd = bd_ref[...]

    kv_s[...] = _dot(_rmsnorm(h, kvn_ref[...]).astype(BF16), wkv_ref[...])
    ub_s[...] = _rmsnorm(h, bn_ref[...]).astype(BF16)

    kn = _head_norm(kv_s[:, :kvw], bd, kn_ref[...])
    v = kv_s[:, kvw:]
    k_slabs = [_rope_slab(kn[:, s * LANES:(s + 1) * LANES], cos, sin, first_half) for s in range(kvw // LANES)]

    for s in range(kvw // LANES):
        kv_s[:, s * LANES:(s + 1) * LANES] = k_slabs[s]

    for g in range(B_KV_HEADS):
        s = g // 2
        top, bot = _split_group(k_slabs[s], g, lo64)
        ktop_s[g, WINDOW:WINDOW + tile, :] = top.astype(BF16)
        kbot_s[g, WINDOW:WINDOW + tile, :] = bot.astype(BF16)
        top, bot = _split_group(v[:, s * LANES:(s + 1) * LANES], g, lo64)
        vtop_s[g, WINDOW:WINDOW + tile, :] = top.astype(BF16)
        vbot_s[g, WINDOW:WINDOW + tile, :] = bot.astype(BF16)

    qscale = qn_ref[...] * (B_HEAD_DIM ** -0.5 * LOG2E)
    qcos = [cos * qscale[:, s * LANES:(s + 1) * LANES] for s in range(kvw // LANES)]
    qsin = [sin * jnp.where(first_half[0:1], pltpu.roll(qscale[:, s * LANES:(s + 1) * LANES], 96, 1),
                            pltpu.roll(qscale[:, s * LANES:(s + 1) * LANES], 32, 1)) for s in range(kvw // LANES)]

    q_s[...] = _dot(ub_s[...], wq_ref[...])
    for c in range(BRANCH_WIDTH // kvw):
        x = q_s[:, c * kvw:(c + 1) * kvw]
        hi, lo = _split2(x * x)
        rinv = lax.rsqrt((_dot(hi, bd) + _dot(lo, bd)) * (1.0 / B_HEAD_DIM) + RMS_EPS)
        for s in range(kvw // LANES):
            xs = x[:, s * LANES:(s + 1) * LANES]
            xr = jnp.where(first_half, pltpu.roll(xs, 96, 1), pltpu.roll(xs, 32, 1))
            qr = ((xs * qcos[s] + xr * qsin[s]) * rinv[:, s * LANES:(s + 1) * LANES]).astype(BF16)
            for i in range(tile // blk):
                qb_s[i, c * (kvw // LANES) + s] = qr[i * blk:(i + 1) * blk]

    rowi = lax.broadcasted_iota(jnp.int32, (blk, 2 * blk), 0)
    coli = lax.broadcasted_iota(jnp.int32, (blk, 2 * blk), 1)
    band = jnp.logical_and(coli - rowi >= 0, coli - rowi <= WINDOW)
    _, lo64_b = _lane_masks(blk)

    lane_v = lax.broadcasted_iota(jnp.int32, (2 * blk, LANES), 1)
    ones_top = jnp.where(lane_v < B_HEAD_DIM, 1.0, 0.0).astype(BF16)
    ones_bot = jnp.where(lane_v < B_HEAD_DIM, 0.0, 1.0).astype(BF16)

    def attn_pair(idx):
        i = idx // B_KV_HEADS
        g = idx % B_KV_HEADS
        r0 = pl.multiple_of(i * blk, blk)
        first_col = jnp.where(l * tile + r0 > 0, 0, WINDOW)
        mask = jnp.logical_and(band, coli >= first_col)
        kpad = jnp.concatenate([ktop_s[g, pl.ds(r0, 2 * blk), :], kbot_s[g, pl.ds(r0, 2 * blk), :]], axis=0)
        vpad = jnp.concatenate([
            jnp.concatenate([vtop_s[g, pl.ds(r0, 2 * blk), :], ones_top], axis=1),
            jnp.concatenate([vbot_s[g, pl.ds(r0, 2 * blk), :], ones_bot], axis=1)], axis=0)
        qg = qb_s[i, pl.ds(g * pairs, pairs)].reshape(pairs * blk, LANES)
        s = _dot_nt(qg, kpad)
        ps, esinks = [], []
        for j in range(pairs):
            pj, ej = [], []
            for half in range(2):
                sink = sink_ref[g * B_GROUP + 2 * j + half] * LOG2E
                sh = jnp.where(mask, s[j * blk:(j + 1) * blk, half * 2 * blk:(half + 1) * 2 * blk], -jnp.inf)
                m = jnp.maximum(jnp.max(sh, axis=-1, keepdims=True), sink)
                pj.append(jnp.exp2(sh - m).astype(BF16))
                ej.append(jnp.exp2(sink - m))
            ps.append(jnp.concatenate(pj, axis=1))
            esinks.append(jnp.where(lo64_b, ej[0], ej[1]))
        o = _dot(jnp.concatenate(ps, axis=0), vpad)
        for j in range(pairs):
            oj = o[j * blk:(j + 1) * blk]
            osl_s[g * pairs + j, pl.ds(r0, blk), :] = oj[:, :LANES] / (oj[:, LANES:] + esinks[j])

    def attn_body(k, carry):
        gc = _dot(ub_s[...], wg_ref[k])
        for s in range(gate_cols // LANES):
            gsl_s[k * (gate_cols // LANES) + s] = gc[:, s * LANES:(s + 1) * LANES]
        attn_pair(2 * k)
        attn_pair(2 * k + 1)
        return carry

    lax.fori_loop(0, n_iter // 2, attn_body, 0, unroll=4)

    for buf in (ktop_s, kbot_s, vtop_s, vbot_s):
        buf[:, 0:WINDOW, :] = buf[:, tile:tile + WINDOW, :]

    y = h
    per = n_slabs // OUT_PROJ_CHUNKS
    for c in range(OUT_PROJ_CHUNKS):
        for sl in range(c * per, (c + 1) * per):
            og_s[:, sl * LANES:(sl + 1) * LANES] = (osl_s[sl] * _silu(gsl_s[sl])).astype(BF16)
        cols = slice(c * per * LANES, (c + 1) * per * LANES)
        y = y + _dot(og_s[:, cols], wout_ref[cols, :])
    y_ref[...] = y

    @pl.when(l == pl.num_programs(1) - 1)
    def _():
        kc_ref[...] = kv_s[tile - WINDOW:, :kvw]
        vc_ref[...] = kv_s[tile - WINDOW:, kvw:]


def _swa_prompt(h, kv_norm, wkv, k_norm, bd, cos, sin, b_norm, wq, wg, q_norm, sinks, wout):
    bsz, seq, d = h.shape
    tile = min(PROMPT_TILE, seq)
    assert seq % tile == 0 and tile % WINDOW == 0 and seq >= WINDOW
    n_iter = (tile // WINDOW) * B_KV_HEADS
    gate_cols = BRANCH_WIDTH // (n_iter // 2)
    assert gate_cols % LANES == 0
    wg = wg.reshape(d, n_iter // 2, gate_cols).transpose(1, 0, 2)
    const = lambda shape: pl.BlockSpec(shape, lambda b, l: (0,) * len(shape), pipeline_mode=pl.Buffered(1))
    k_scratch = pltpu.VMEM((B_KV_HEADS, WINDOW + tile, LANES), BF16)
    n_slabs = BRANCH_WIDTH // LANES
    return pl.pallas_call(
        _swa_prompt_kernel,
        grid=(bsz, seq // tile),
        in_specs=[
            pl.BlockSpec((None, tile, d), lambda b, l: (b, l, 0)),
            const(kv_norm.shape), const(wkv.shape), const(k_norm.shape), const(bd.shape),
            pl.BlockSpec((tile, LANES), lambda b, l: (l, 0)),
            pl.BlockSpec((tile, LANES), lambda b, l: (l, 0)),
            const(b_norm.shape), const(wq.shape), const(wg.shape), const(q_norm.shape),
            pl.BlockSpec(memory_space=pltpu.SMEM),
            const(wout.shape),
        ],
        out_specs=[
            pl.BlockSpec((None, tile, d), lambda b, l: (b, l, 0)),
            pl.BlockSpec((None, WINDOW, B_KV_WIDTH), lambda b, l: (b, 0, 0)),
            pl.BlockSpec((None, WINDOW, B_KV_WIDTH), lambda b, l: (b, 0, 0)),
        ],
        out_shape=[
            jax.ShapeDtypeStruct((bsz, seq, d), F32),
            jax.ShapeDtypeStruct((bsz, WINDOW, B_KV_WIDTH), F32),
            jax.ShapeDtypeStruct((bsz, WINDOW, B_KV_WIDTH), F32),
        ],
        scratch_shapes=[
            k_scratch, k_scratch, k_scratch, k_scratch,
            pltpu.VMEM((tile // WINDOW, n_slabs, WINDOW, LANES), BF16),
            pltpu.VMEM((tile, d), BF16),
            pltpu.VMEM((tile, 2 * B_KV_WIDTH), F32),
            pltpu.VMEM((tile, BRANCH_WIDTH), F32),
            pltpu.VMEM((n_slabs, tile, LANES), F32),
            pltpu.VMEM((n_slabs, tile, LANES), F32),
            pltpu.VMEM((tile, BRANCH_WIDTH), BF16),
        ],
        compiler_params=pltpu.CompilerParams(
            dimension_semantics=("arbitrary", "arbitrary"), vmem_limit_bytes=VMEM_LIMIT_BYTES),
        name="swa_prompt",
    )(h, kv_norm, wkv, k_norm, bd, cos, sin, b_norm, wq, wg, q_norm, sinks, wout)


def _gla_proj_sample_kernel(x_ref, g_ref, wqk_ref, wv_ref, wg_ref, wlow_ref, qk_ref, v_ref, gate_ref, glow_ref):
    u = _rmsnorm(x_ref[...], g_ref[...]).astype(BF16)
    qk_ref[...] = _dot(u, wqk_ref[...])
    v_ref[...] = _dot(u, wv_ref[...])
    gate_ref[...] = _dot(u, wg_ref[...])
    glow_ref[...] = _dot(u, wlow_ref[...])


def _gla_proj_sample(x, g, wqk, wv, wg, wlow):
    m = x.shape[0]
    return pl.pallas_call(
        _gla_proj_sample_kernel,
        out_shape=[jax.ShapeDtypeStruct((m, w.shape[1]), F32) for w in (wqk, wv, wg, wlow)],
        compiler_params=pltpu.CompilerParams(vmem_limit_bytes=VMEM_LIMIT_BYTES),
        name="gla_proj_sample",
    )(x, g, wqk, wv, wg, wlow)


def _gla_out_kernel(o_ref, gate_ref, x_ref, on_ref, w_ref, y_ref, og_s):
    for h in range(A_HEADS):
        vs = slice(h * A_HEAD_DV, (h + 1) * A_HEAD_DV)
        o = _rmsnorm(o_ref[:, vs], on_ref[...])
        og_s[:, vs] = (o * _silu(gate_ref[:, vs])).astype(BF16)
    y_ref[...] = _dot(og_s[...], w_ref[...]) + x_ref[...]


def _gla_out(o, gate, x, out_norm, wout):
    m = o.shape[0]
    return pl.pallas_call(
        _gla_out_kernel,
        out_shape=jax.ShapeDtypeStruct(x.shape, F32),
        scratch_shapes=[pltpu.VMEM((m, BRANCH_WIDTH), BF16)],
        compiler_params=pltpu.CompilerParams(vmem_limit_bytes=VMEM_LIMIT_BYTES),
        name="gla_out_sample",
    )(o, gate, x, out_norm, wout)


def _out_proj_kernel(og_ref, x_ref, w_ref, y_ref):
    y_ref[...] = _dot(og_ref[...].astype(BF16), w_ref[...]) + x_ref[...]


def _out_proj(og, x, wout):
    return pl.pallas_call(
        _out_proj_kernel,
        out_shape=jax.ShapeDtypeStruct(x.shape, F32),
        compiler_params=pltpu.CompilerParams(vmem_limit_bytes=VMEM_LIMIT_BYTES),
        name="swa_out_sample",
    )(og, x, wout)


def _gla_sample_kernel(qk_ref, v_ref, glow_ref, wg2_ref, bg_ref, st_ref, o_ref, nst_ref, *, seq):
    nb = st_ref.shape[0]
    dk, dv, kw = A_HEAD_DK, A_HEAD_DV, A_KEY_WIDTH
    scale = A_HEAD_DK ** -0.5
    per_tile = SUBLANES // seq

    def one_seq(i, qk, v, loga):
        q = qk[:, 0:kw] * scale
        k = qk[:, kw:2 * kw]
        bs = [loga[0:1]]
        for t in range(1, seq):
            bs.append(bs[-1] + loga[t:t + 1])
        bmat = jnp.concatenate(bs, axis=0)
        blast = bs[-1]
        qt = q * jnp.exp(bmat)
        kd = k * jnp.exp(blast - bmat)
        eblast = jnp.exp(blast)

        outs = [[jnp.zeros((1, dv), F32) for _ in range(A_HEADS)] for _ in range(seq)]
        for t in range(seq):
            for s in range(t + 1):
                w = q[t:t + 1] * k[s:s + 1] * jnp.exp(bs[t] - bs[s])
                for h in range(A_HEADS):
                    a = jnp.sum(w[:, h * dk:(h + 1) * dk], axis=-1, keepdims=True)
                    outs[t][h] = outs[t][h] + a * v[s:s + 1, h * dv:(h + 1) * dv]
        intra = jnp.concatenate([jnp.concatenate(outs[t], axis=1) for t in range(seq)], axis=0)

        qt8 = jnp.concatenate([qt, jnp.zeros((SUBLANES - seq, kw), F32)], axis=0).astype(BF16)
        inter = []
        for h in range(A_HEADS):
            ks = slice(h * dk, (h + 1) * dk)
            state = st_ref[i, h]
            inter.append(_dot(qt8[:, ks], state.astype(BF16))[0:seq])
            m = jnp.concatenate([kd[:, ks], eblast[:, ks], jnp.zeros((dk - seq - 1, dk), F32)], axis=0)
            mt = jnp.transpose(m)
            vpad = jnp.concatenate([v[:, h * dv:(h + 1) * dv], jnp.zeros((dk - seq, dv), F32)], axis=0)
            nst_ref[i, h] = state * mt[:, seq:seq + 1] + _dot(mt.astype(BF16), vpad.astype(BF16))
        return intra + jnp.concatenate(inter, axis=1)

    def per_tile_body(j, carry):
        rows = pl.ds(pl.multiple_of(j * SUBLANES, SUBLANES), SUBLANES)
        qk8, v8 = qk_ref[rows, :], v_ref[rows, :]
        pre = _dot(glow_ref[rows, :].astype(BF16), wg2_ref[...]) + bg_ref[...]
        loga8 = _log_sigmoid(pre) * (1.0 / A_GATE_NORMALIZER)
        outs = []
        for p in range(per_tile):
            r = slice(p * seq, (p + 1) * seq)
            outs.append(one_seq(j * per_tile + p, qk8[r], v8[r], loga8[r]))
        o_ref[rows, :] = jnp.concatenate(outs, axis=0)
        return carry

    lax.fori_loop(0, nb // per_tile, per_tile_body, 0)


def _gla_sample(qk, v, glow, wg2, bg, state, seq):
    nbatch = state.shape[0]
    nb = min(SAMPLE_BATCH_BLOCK, nbatch)
    assert nbatch % nb == 0 and SUBLANES % seq == 0 and (nb * seq) % SUBLANES == 0
    st_spec = pl.BlockSpec((nb, A_HEADS, A_HEAD_DK, A_HEAD_DV), lambda b: (b, 0, 0, 0))
    tok_spec = lambda a: pl.BlockSpec((nb * seq, a.shape[1]), lambda b: (b, 0))
    return pl.pallas_call(
        functools.partial(_gla_sample_kernel, seq=seq),
        grid=(nbatch // nb,),
        in_specs=[
            tok_spec(qk), tok_spec(v), tok_spec(glow),
            pl.BlockSpec(wg2.shape, lambda b: (0, 0)),
            pl.BlockSpec(bg.shape, lambda b: (0, 0)),
            st_spec,
        ],
        out_specs=[pl.BlockSpec((nb * seq, BRANCH_WIDTH), lambda b: (b, 0)), st_spec],
        out_shape=[
            jax.ShapeDtypeStruct((nbatch * seq, BRANCH_WIDTH), F32),
            jax.ShapeDtypeStruct(state.shape, F32),
        ],
        compiler_params=pltpu.CompilerParams(
            dimension_semantics=("arbitrary",), vmem_limit_bytes=VMEM_LIMIT_BYTES),
        name="gla_sample",
    )(qk, v, glow, wg2, bg, state)


def _swa_pre_sample_kernel(h_ref, kvn_ref, wkv_ref, kn_ref, bd_ref, cos_ref, sin_ref, bn_ref, wq_ref, wg_ref, qn_ref,
                           k_ref, v_ref, q_ref, gate_ref):
    m = h_ref.shape[0]
    kvw = B_KV_WIDTH
    h = h_ref[...]
    cos = cos_ref[...]
    sin = sin_ref[...]
    first_half, _ = _lane_masks(m)
    bd = bd_ref[...]

    kv = _dot(_rmsnorm(h, kvn_ref[...]).astype(BF16), wkv_ref[...])
    kn = _head_norm(kv[:, :kvw], bd, kn_ref[...])
    for s in range(kvw // LANES):
        k_ref[:, s * LANES:(s + 1) * LANES] = _rope_slab(kn[:, s * LANES:(s + 1) * LANES], cos, sin, first_half)
    v_ref[...] = kv[:, kvw:]

    ub = _rmsnorm(h, bn_ref[...]).astype(BF16)
    _, lo64 = _lane_masks(m)
    qgain = qn_ref[...] * (B_HEAD_DIM ** -0.5 * LOG2E)
    gate = _dot(ub, wg_ref[...])
    q_slabs, g_slabs = [], []
    for c in range(BRANCH_WIDTH // kvw):
        qc = _head_norm(_dot(ub, wq_ref[:, c * kvw:(c + 1) * kvw]), bd, qgain)
        for s in range(kvw // LANES):
            q_slabs.append(_rope_slab(qc[:, s * LANES:(s + 1) * LANES], cos, sin, first_half))
            g_slabs.append(gate[:, c * kvw + s * LANES:c * kvw + (s + 1) * LANES])
    per_group = B_GROUP // 2
    for r in range(B_GROUP):
        for p in range(B_KV_HEADS // 2):
            a = 2 * p * per_group + r // 2
            b = (2 * p + 1) * per_group + r // 2
            lanes = slice((2 * r + p) * LANES, (2 * r + p + 1) * LANES)
            q_ref[:, lanes] = _regroup_slab(q_slabs[a], q_slabs[b], r % 2, lo64)
            gate_ref[:, lanes] = _regroup_slab(g_slabs[a], g_slabs[b], r % 2, lo64)


def _swa_pre_sample(h, kv_norm, wkv, k_norm, bd, cos, sin, b_norm, wq, wg, q_norm):
    m = h.shape[0]
    return pl.pallas_call(
        _swa_pre_sample_kernel,
        out_shape=[
            jax.ShapeDtypeStruct((m, B_KV_WIDTH), F32),
            jax.ShapeDtypeStruct((m, B_KV_WIDTH), F32),
            jax.ShapeDtypeStruct((m, BRANCH_WIDTH), F32),
            jax.ShapeDtypeStruct((m, BRANCH_WIDTH), F32),
        ],
        compiler_params=pltpu.CompilerParams(vmem_limit_bytes=VMEM_LIMIT_BYTES),
        name="swa_pre_sample",
    )(h, kv_norm, wkv, k_norm, bd, cos, sin, b_norm, wq, wg, q_norm)


def _swa_sample_kernel(q_ref, gate_ref, kn_ref, vn_ref, ck_ref, cv_ref, sink_ref, og_ref, nk_ref, nv_ref, *, seq):
    nb = ck_ref.shape[0]
    per_tile = SUBLANES // seq
    kvw = B_KV_WIDTH
    trows = SUBLANES
    nrows = B_GROUP * B_KV_HEADS * trows
    rowi = lax.broadcasted_iota(jnp.int32, (nrows, 2 * WINDOW), 0)
    coli = lax.broadcasted_iota(jnp.int32, (nrows, 2 * WINDOW), 1)
    dpos = coli - rowi % trows
    mask = jnp.logical_and(dpos >= 0, dpos <= WINDOW)
    lane8 = lax.broadcasted_iota(jnp.int32, (trows, kvw), 1) // B_HEAD_DIM
    grow = (lax.broadcasted_iota(jnp.int32, (nrows, kvw), 0) // trows) % B_KV_HEADS
    gkeep = grow == lax.broadcasted_iota(jnp.int32, (nrows, kvw), 1) // B_HEAD_DIM
    sink = sink_ref[...][:, 0:1] * LOG2E
    zq = jnp.zeros((trows - seq, BRANCH_WIDTH), F32)
    lo64_q = lax.broadcasted_iota(jnp.int32, (seq, LANES), 1) < B_HEAD_DIM
    zk = jnp.zeros((WINDOW - seq, kvw), F32)

    def one_seq(i, q4, gate4, k_new, v_new):
        ck = ck_ref[i]
        cv = cv_ref[i]
        nk_ref[i, 0:WINDOW - seq, :] = ck[seq:, :]
        nk_ref[i, WINDOW - seq:WINDOW, :] = k_new
        nv_ref[i, 0:WINDOW - seq, :] = cv[seq:, :]
        nv_ref[i, WINDOW - seq:WINDOW, :] = v_new

        q8 = jnp.concatenate([q4, zq], axis=0)
        pieces = []
        for r in range(B_GROUP):
            slab = q8[:, r * kvw:(r + 1) * kvw]
            for g in range(B_KV_HEADS):
                pieces.append(jnp.where(lane8 == g, slab, 0.0))
        qrows = jnp.concatenate(pieces, axis=0).astype(BF16)
        kpad = jnp.concatenate([k_new, zk], axis=0).astype(BF16)
        vpad = jnp.concatenate([v_new, zk], axis=0).astype(BF16)
        s = jnp.concatenate([_dot_nt(qrows, ck.astype(BF16)), _dot_nt(qrows, kpad)], axis=1)
        s = jnp.where(mask, s, -jnp.inf)
        m = jnp.maximum(jnp.max(s, axis=-1, keepdims=True), sink)
        p = jnp.exp2(s - m)
        den = jnp.sum(p, axis=-1, keepdims=True) + jnp.exp2(sink - m)
        pb = p.astype(BF16)
        o = _dot(pb[:, :WINDOW], cv.astype(BF16)) + _dot(pb[:, WINDOW:], vpad)
        o = jnp.where(gkeep, o / den, 0.0)
        outs = []
        for r in range(B_GROUP):
            acc = o[r * B_KV_HEADS * trows:r * B_KV_HEADS * trows + trows]
            for g in range(1, B_KV_HEADS):
                acc = acc + o[(r * B_KV_HEADS + g) * trows:(r * B_KV_HEADS + g + 1) * trows]
            outs.append(acc)
        gated = jnp.concatenate(outs, axis=1)[0:seq] * _silu(gate4)
        slabs = []
        for g in range(B_KV_HEADS):
            for j in range(B_GROUP // 2):
                a = gated[:, (4 * j + g // 2) * LANES:(4 * j + g // 2 + 1) * LANES]
                b = gated[:, (4 * j + 2 + g // 2) * LANES:(4 * j + 2 + g // 2 + 1) * LANES]
                slabs.append(_regroup_slab(a, b, g % 2, lo64_q))
        return jnp.concatenate(slabs, axis=1)

    def per_tile_body(j, carry):
        rows = pl.ds(pl.multiple_of(j * SUBLANES, SUBLANES), SUBLANES)
        q8, gate8, kn8, vn8 = q_ref[rows, :], gate_ref[rows, :], kn_ref[rows, :], vn_ref[rows, :]
        outs = []
        for p in range(per_tile):
            r = slice(p * seq, (p + 1) * seq)
            outs.append(one_seq(j * per_tile + p, q8[r], gate8[r], kn8[r], vn8[r]))
        og_ref[rows, :] = jnp.concatenate(outs, axis=0)
        return carry

    lax.fori_loop(0, nb // per_tile, per_tile_body, 0, unroll=4)


def _swa_sample(q, gate, k_new, v_new, ck, cv, sink_rows, seq):
    nbatch = ck.shape[0]
    nb = min(SAMPLE_BATCH_BLOCK, nbatch)
    assert nbatch % nb == 0 and ck.shape[1] == WINDOW and SUBLANES % seq == 0 and (nb * seq) % SUBLANES == 0
    cache_spec = pl.BlockSpec((nb, WINDOW, B_KV_WIDTH), lambda b: (b, 0, 0))
    wide_spec = pl.BlockSpec((nb * seq, BRANCH_WIDTH), lambda b: (b, 0))
    new_spec = pl.BlockSpec((nb * seq, B_KV_WIDTH), lambda b: (b, 0))
    return pl.pallas_call(
        functools.partial(_swa_sample_kernel, seq=seq),
        grid=(nbatch // nb,),
        in_specs=[wide_spec, wide_spec, new_spec, new_spec, cache_spec, cache_spec,
                  pl.BlockSpec(sink_rows.shape, lambda b: (0, 0))],
        out_specs=[wide_spec, cache_spec, cache_spec],
        out_shape=[
            jax.ShapeDtypeStruct((nbatch * seq, BRANCH_WIDTH), F32),
            jax.ShapeDtypeStruct(ck.shape, F32),
            jax.ShapeDtypeStruct(cv.shape, F32),
        ],
        compiler_params=pltpu.CompilerParams(
            dimension_semantics=("arbitrary",), vmem_limit_bytes=VMEM_LIMIT_BYTES),
        name="swa_sample",
    )(q, gate, k_new, v_new, ck, cv, sink_rows)


def _rope_tables(first_pos, n_pos, repeat=1):
    half = B_HEAD_DIM // 2
    inv_freq = ROPE_THETA ** (-np.arange(half, dtype=np.float64) / half)
    ang = (first_pos + np.arange(n_pos, dtype=np.float64))[:, None] * inv_freq[None, :]
    cos, sin = np.cos(ang), np.sin(ang)
    cos_t = np.tile(np.concatenate([cos, cos, cos, cos], axis=1), (repeat, 1))
    sin_t = np.tile(np.concatenate([-sin, sin, -sin, sin], axis=1), (repeat, 1))
    return jnp.asarray(cos_t, F32), jnp.asarray(sin_t, F32)


def kernel(x_prompt, x_sample, state_gla, cache_swa_k, cache_swa_v, a_norm, a_w_in, a_w_gate2, a_b_gate,
           a_out_norm, a_w_out, kv_norm, w_k, w_v, k_norm, b_norm, b_w_in, b_q_norm, b_sinks, b_w_out):
    assert a_norm.shape[0] == 1 and b_norm.shape[0] == 1
    bsz_p, seq_p, d = x_prompt.shape
    bsz_s, seq_s, _ = x_sample.shape
    kw, bw = A_KEY_WIDTH, BRANCH_WIDTH

    w_in = a_w_in[0]
    wa_qk = w_in[:, :2 * kw].astype(BF16)
    wa_v = w_in[:, 2 * kw:2 * kw + bw].astype(BF16)
    wa_g = w_in[:, 2 * kw + bw:2 * kw + 2 * bw].astype(BF16)
    wa_low = jnp.pad(w_in[:, 2 * kw + 2 * bw:].astype(BF16), ((0, 0), (0, LANES - A_GATE_RANK)))
    wa_g2 = jnp.pad(a_w_gate2[0].astype(BF16), ((0, LANES - A_GATE_RANK), (0, 0)))
    a_bg = a_b_gate[0][None, :]
    a_n = a_norm[0][None, :]
    a_on = a_out_norm[0][None, :]
    wa_out = a_w_out[0].astype(BF16)
    w_kv = jnp.concatenate([w_k.astype(BF16), w_v.astype(BF16)], axis=1)
    wb_q = b_w_in[0][:, :bw].astype(BF16)
    wb_g = b_w_in[0][:, bw:].astype(BF16)
    wb_out = b_w_out[0].astype(BF16)
    kvn = kv_norm[None, :]
    bn = b_norm[0][None, :]
    kn_t = jnp.tile(k_norm, B_KV_WIDTH // B_HEAD_DIM)[None, :]
    qn_t = jnp.tile(b_q_norm[0], B_KV_WIDTH // B_HEAD_DIM)[None, :]
    sinks = b_sinks[0]
    grp = np.arange(B_KV_WIDTH) // B_HEAD_DIM
    bd = jnp.asarray(grp[:, None] == grp[None, :], BF16)

    cos_p, sin_p = _rope_tables(0, seq_p)
    h1_p, st_p = _gla_prompt(x_prompt, a_n, wa_qk, wa_v, wa_g, wa_low, wa_g2, a_bg, a_on, wa_out)
    y_p, kc_p, vc_p = _swa_prompt(h1_p, kvn, w_kv, kn_t, bd, cos_p, sin_p, bn, wb_q, wb_g, qn_t, sinks, wb_out)

    m = bsz_s * seq_s
    xs = x_sample.reshape(m, d)
    qk_a, v_a, gate_a, glow_a = _gla_proj_sample(xs, a_n, wa_qk, wa_v, wa_g, wa_low)
    o_s, st_s = _gla_sample(qk_a, v_a, glow_a, wa_g2, a_bg, state_gla[0], seq_s)
    h1_s = _gla_out(o_s, gate_a, xs, a_on, wa_out)

    sink_rows = jnp.broadcast_to(
        sinks.reshape(B_KV_HEADS, B_GROUP).T[:, :, None, None], (B_GROUP, B_KV_HEADS, SUBLANES, LANES)
    ).reshape(B_GROUP * B_KV_HEADS * SUBLANES, LANES)
    cos_s, sin_s = _rope_tables(PAST_LEN, seq_s, repeat=bsz_s)
    k_new, v_new, q_s, gate_b = _swa_pre_sample(h1_s, kvn, w_kv, kn_t, bd, cos_s, sin_s, bn, wb_q, wb_g, qn_t)
    og_s, nk_s, nv_s = _swa_sample(
        q_s, gate_b, k_new, v_new,
        cache_swa_k.reshape(bsz_s, WINDOW, B_KV_WIDTH), cache_swa_v.reshape(bsz_s, WINDOW, B_KV_WIDTH), sink_rows, seq_s)
    y_s = _out_proj(og_s, h1_s, wb_out)

    return (y_p, y_s.reshape(bsz_s, seq_s, d),
            st_p[None], st_s[None],
            kc_p.reshape(bsz_p, WINDOW, B_KV_HEADS, B_HEAD_DIM), vc_p.reshape(bsz_p, WINDOW, B_KV_HEADS, B_HEAD_DIM),
            nk_s.reshape(bsz_s, WINDOW, B_KV_HEADS, B_HEAD_DIM), nv_s.reshape(bsz_s, WINDOW, B_KV_HEADS, B_HEAD_DIM))
```
